```python
import math
import jax, jax.numpy as jnp
from jax import lax
import numpy as np

D_MODEL = 1024
BATCH = 1
SEQ = 16384
DEPTH = 4
DEC_BATCH = 32
DEC_SEQ = 2048
PAST_LEN = 128

N_MIXERS = 3
N_ATTN_LAYERS = (DEPTH + 2) // 3
N_CONV_LAYERS = (DEPTH + 1) // 3
N_HYENA_LAYERS = DEPTH // 3

N_HEADS = 8
HEAD_DIM = 64
Q_BLOCK = 128

CONV_WIDTH = 31

HYENA_ORDER = 2
SHORT_WIDTH = 3
POS_BANDS = 16
POS_EMB_DIM = 1 + 2 * POS_BANDS
FILTER_WIDTH = 64
FAST_DECAY_PCT = 0.3
SLOW_DECAY_PCT = 1.5
DECAY_TARGET = 1e-2
MAX_DECAY = math.log(DECAY_TARGET) / FAST_DECAY_PCT
MIN_DECAY = math.log(DECAY_TARGET) / SLOW_DECAY_PCT

N_GROUPS = 4
EXPERTS_PER_GROUP = 4
N_EXPERTS = N_GROUPS * EXPERTS_PER_GROUP
TOP_K = 2
EXPERT_FF = 512

PLE_DIM = 256

ALPHA = (2 * DEPTH) ** 0.25
BETA = (8 * DEPTH) ** -0.25
LN_EPS = 1e-5

kernel_name = 'hybrid_diffattn_conformer_hyena_hmoe_encoder'


def layer_norm(x, g, b):
    xf = x.astype(jnp.float32)
    mu = jnp.mean(xf, axis=-1, keepdims=True)
    var = jnp.mean(jnp.square(xf - mu), axis=-1, keepdims=True)
    return ((xf - mu) * lax.rsqrt(var + LN_EPS) * g + b).astype(x.dtype)


def rms_norm(x, g):
    xf = x.astype(jnp.float32)
    ms = jnp.mean(jnp.square(xf), axis=-1, keepdims=True)
    return (xf * lax.rsqrt(ms + LN_EPS) * g).astype(x.dtype)


def depthwise_conv(x, w, b):
    width = w.shape[0]
    pad = width // 2
    y = lax.conv_general_dilated(x, w[:, None, :].astype(x.dtype), (1,), [(pad, pad)],
                                 dimension_numbers=('NWC', 'WIO', 'NWC'),
                                 feature_group_count=x.shape[-1])
    return y + b


def alibi_slopes(n_heads):
    return 2.0 ** (-8.0 * (jnp.arange(n_heads, dtype=jnp.float32) + 1.0) / n_heads)


def diff_attention(x, w_qkv, w_o, lam_q1, lam_k1, lam_q2, lam_k2, subln_g, lambda_init):
    B, L, _ = x.shape
    f32 = jnp.float32
    n_blk = L // Q_BLOCK
    q, k, v = jnp.split(x @ w_qkv, 3, axis=-1)
    q = q.reshape(B, L, N_HEADS, 2, HEAD_DIM) * (HEAD_DIM ** -0.5)
    k = k.reshape(B, L, N_HEADS, 2, HEAD_DIM)
    v = v.reshape(B, L, N_HEADS, 2 * HEAD_DIM)
    lam = (jnp.exp(jnp.sum(lam_q1.astype(f32) * lam_k1.astype(f32)))
           - jnp.exp(jnp.sum(lam_q2.astype(f32) * lam_k2.astype(f32))) + lambda_init)
    slopes = alibi_slopes(N_HEADS)[:, None, None, None]
    k_pos = jnp.arange(L, dtype=f32)
    q_blocks = jnp.moveaxis(q.reshape(B, n_blk, Q_BLOCK, N_HEADS, 2, HEAD_DIM), 1, 0)
    starts = jnp.arange(n_blk, dtype=f32) * Q_BLOCK

    def attend(args):
        q_blk, start = args
        s = jnp.einsum('bqhcd,bkhcd->bhcqk', q_blk, k, preferred_element_type=f32)
        q_pos = start + jnp.arange(Q_BLOCK, dtype=f32)
        dist = jnp.abs(q_pos[:, None] - k_pos[None, :])
        p = jax.nn.softmax(s - slopes * dist, axis=-1)
        a = p[:, :, 0] - lam * p[:, :, 1]
        return jnp.einsum('bhqk,bkhe->bqhe', a.astype(v.dtype), v)

    o = lax.map(attend, (q_blocks, starts))
    o = jnp.moveaxis(o, 0, 1).reshape(B, L, N_HEADS, 2 * HEAD_DIM)
    o = rms_norm(o, subln_g) * (1.0 - lambda_init)
    return o.reshape(B, L, D_MODEL) @ w_o


def conformer_conv(x, w_pw1, b_pw1, w_dw, b_dw, ln_g, ln_b, w_pw2, b_pw2):
    a, g = jnp.split(x @ w_pw1 + b_pw1, 2, axis=-1)
    h = a * jax.nn.sigmoid(g)
    h = depthwise_conv(h, w_dw, b_dw)
    h = jax.nn.silu(layer_norm(h, ln_g, ln_b))
    return h @ w_pw2 + b_pw2


def hyena_filters(L, f_w1, f_b1, f_w2, f_b2, f_w3, f_b3, f_freq, f_wout):
    f32 = jnp.float32
    t = jnp.linspace(0.0, 1.0, L, dtype=f32)[:, None]
    w = 2.0 * math.pi * jnp.arange(L, dtype=f32)[:, None] / L
    bands = jnp.linspace(1e-4, POS_BANDS - 1, POS_BANDS, dtype=f32)[None, :]
    fw = bands * w
    z = jnp.concatenate([t, jnp.cos(fw), -jnp.sin(fw)], axis=-1)
    h = jnp.sin(f_freq * (z @ f_w1 + f_b1))
    h = jnp.sin(f_freq * (h @ f_w2 + f_b2))
    h = jnp.sin(f_freq * (h @ f_w3 + f_b3))
    h = (h @ f_wout).astype(f32).reshape(L, HYENA_ORDER, 2, D_MODEL)
    deltas = jnp.abs(jnp.linspace(MIN_DECAY, MAX_DECAY, D_MODEL, dtype=f32))
    decay = jnp.exp(-t * deltas)
    return h * decay[:, None, None, :]


def bidir_long_conv(v, h_fwd, h_bwd, bias):
    L = v.shape[1]
    n_fft = 2 * L
    h_two_sided = jnp.concatenate([h_fwd[:1] + h_bwd[:1], h_fwd[1:],
                                   jnp.zeros_like(h_fwd[:1]), h_bwd[:0:-1]], axis=0)
    vf = v.astype(jnp.float32)
    spec = jnp.fft.rfft(vf, n=n_fft, axis=1) * jnp.fft.rfft(h_two_sided, n=n_fft, axis=0)[None]
    y = jnp.fft.irfft(spec, n=n_fft, axis=1)[:, :L]
    return (y + vf * bias.astype(jnp.float32)).astype(v.dtype)


def hyena_operator(x, w_in, b_in, w_short, b_short, f_w1, f_b1, f_w2, f_b2, f_w3, f_b3,
                   f_freq, f_wout, f_bias, w_out, b_out):
    L = x.shape[1]
    u = depthwise_conv(x @ w_in + b_in, w_short, b_short)
    x1, x2, v = jnp.split(u, 3, axis=-1)
    h = hyena_filters(L, f_w1, f_b1, f_w2, f_b2, f_w3, f_b3, f_freq, f_wout)
    z = v
    for n, gate in enumerate((x1, x2)):
        z = gate * bidir_long_conv(z, h[:, n, 0], h[:, n, 1], f_bias[n])
    return z @ w_out + b_out


def hier_moe(x, w_group, b_group, w_expert, b_expert, w_gate, w_up, w_down):
    B, L, D = x.shape
    xt = x.reshape(-1, D)
    T = xt.shape[0]
    g_logits = (xt @ w_group + b_group).astype(jnp.float32)
    g_idx = jnp.argmax(g_logits, axis=-1)
    g_w = jnp.max(jax.nn.softmax(g_logits, axis=-1), axis=-1, keepdims=True)
    e_logits = (xt @ w_expert + b_expert).astype(jnp.float32).reshape(T, N_GROUPS, EXPERTS_PER_GROUP)
    e_logits = e_logits[jnp.arange(T), g_idx]
    top_v, top_i = lax.top_k(e_logits, TOP_K)
    top_w = jax.nn.softmax(top_v, axis=-1) * g_w
    flat = g_idx[:, None] * EXPERTS_PER_GROUP + top_i
    comb = jnp.sum(jax.nn.one_hot(flat, N_EXPERTS, dtype=jnp.float32) * top_w[..., None], axis=1)
    comb = comb.astype(x.dtype)
    y = jnp.zeros_like(xt)
    for e in range(N_EXPERTS):
        hid = jax.nn.silu(xt @ w_gate[e]) * (xt @ w_up[e])
        y = y + (hid @ w_down[e]) * comb[:, e:e + 1]
    return y.reshape(B, L, D)


def trunk(x, p, attn, conv, hyena, norms, moe, ple):
    for i in range(DEPTH):
        j = i // N_MIXERS
        kind = i % N_MIXERS
        if kind == 0:
            h = diff_attention(x, *[a[j] for a in attn], lambda_init=0.8 - 0.6 * math.exp(-0.3 * i))
        elif kind == 1:
            h = conformer_conv(x, *[a[j] for a in conv])
        else:
            h = hyena_operator(x, *[a[j] for a in hyena])
        ln1_g, ln1_b, ln2_g, ln2_b = [a[i] for a in norms]
        x = layer_norm(ALPHA * x + h, ln1_g, ln1_b)
        x = layer_norm(ALPHA * x + hier_moe(x, *[a[i] for a in moe]), ln2_g, ln2_b)
        ple_up, ple_gate = ple[0][i], ple[1][i]
        x = x + (p[i] @ ple_up) * jax.nn.sigmoid(x @ ple_gate)
    return x


def setup_inputs(seed: int = 0) -> dict:
    key = jax.random.key(seed)
    keys = iter(jax.random.split(key, 64))
    D = D_MODEL
    NA, NC, NH = N_ATTN_LAYERS, N_CONV_LAYERS, N_HYENA_LAYERS

    def normal(shape, scale):
        return jax.random.normal(next(keys), shape, jnp.float32) * scale

    def gain(shape):
        return 1.0 + normal(shape, 0.02)

    return {
        'x_prompt': normal((BATCH, SEQ, D), 1.0),
        'x_sample': normal((DEC_BATCH, DEC_SEQ, D), 1.0),
        'p_prompt': normal((DEPTH, BATCH, SEQ, PLE_DIM), 1.0),
        'p_sample': normal((DEPTH, DEC_BATCH, DEC_SEQ, PLE_DIM), 1.0),
        'attn_w_qkv': jnp.concatenate([normal((NA, D, 2 * D), D ** -0.5),
                                       normal((NA, D, D), D ** -0.5) * BETA], axis=-1),
        'attn_w_o': normal((NA, D, D), D ** -0.5) * BETA,
        'attn_lam_q1': normal((NA, HEAD_DIM), 0.1),
        'attn_lam_k1': normal((NA, HEAD_DIM), 0.1),
        'attn_lam_q2': normal((NA, HEAD_DIM), 0.1),
        'attn_lam_k2': normal((NA, HEAD_DIM), 0.1),
        'attn_subln_g': gain((NA, 2 * HEAD_DIM)),
        'conv_w_pw1': normal((NC, D, 2 * D), D ** -0.5),
        'conv_b_pw1': normal((NC, 2 * D), 0.02),
        'conv_w_dw': normal((NC, CONV_WIDTH, D), CONV_WIDTH ** -0.5),
        'conv_b_dw': normal((NC, D), 0.02),
        'conv_ln_g': gain((NC, D)),
        'conv_ln_b': normal((NC, D), 0.02),
        'conv_w_pw2': normal((NC, D, D), D ** -0.5) * BETA,
        'conv_b_pw2': normal((NC, D), 0.02),
        'hy_w_in': normal((NH, D, 3 * D), D ** -0.5),
        'hy_b_in': normal((NH, 3 * D), 0.02),
        'hy_w_short': normal((NH, SHORT_WIDTH, 3 * D), SHORT_WIDTH ** -0.5),
        'hy_b_short': normal((NH, 3 * D), 0.02),
        'hy_f_w1': normal((NH, POS_EMB_DIM, FILTER_WIDTH), POS_EMB_DIM ** -0.5),
        'hy_f_b1': normal((NH, FILTER_WIDTH), 0.02),
        'hy_f_w2': normal((NH, FILTER_WIDTH, FILTER_WIDTH), FILTER_WIDTH ** -0.5),
        'hy_f_b2': normal((NH, FILTER_WIDTH), 0.02),
        'hy_f_w3': normal((NH, FILTER_WIDTH, FILTER_WIDTH), FILTER_WIDTH ** -0.5),
        'hy_f_b3': normal((NH, FILTER_WIDTH), 0.02),
        'hy_f_freq': gain((NH, FILTER_WIDTH)),
        'hy_f_wout': normal((NH, FILTER_WIDTH, HYENA_ORDER * 2 * D), 0.05 * FILTER_WIDTH ** -0.5),
        'hy_f_bias': normal((NH, HYENA_ORDER, D), 1.0),
        'hy_w_out': normal((NH, D, D), D ** -0.5) * BETA,
        'hy_b_out': normal((NH, D), 0.02),
        'ln1_g': gain((DEPTH, D)),
        'ln1_b': normal((DEPTH, D), 0.02),
        'ln2_g': gain((DEPTH, D)),
        'ln2_b': normal((DEPTH, D), 0.02),
        'moe_w_group': normal((DEPTH, D, N_GROUPS), D ** -0.5),
        'moe_b_group': normal((DEPTH, N_GROUPS), 0.01),
        'moe_w_expert': normal((DEPTH, D, N_EXPERTS), D ** -0.5),
        'moe_b_expert': normal((DEPTH, N_EXPERTS), 0.01),
        'moe_w_gate': normal((DEPTH, N_EXPERTS, D, EXPERT_FF), D ** -0.5),
        'moe_w_up': normal((DEPTH, N_EXPERTS, D, EXPERT_FF), D ** -0.5),
        'moe_w_down': normal((DEPTH, N_EXPERTS, EXPERT_FF, D), EXPERT_FF ** -0.5) * BETA,
        'ple_w_up': normal((DEPTH, PLE_DIM, D), PLE_DIM ** -0.5),
        'ple_w_gate': normal((DEPTH, D, D), D ** -0.5),
    }


def reference(x_prompt, x_sample, p_prompt, p_sample,
              attn_w_qkv, attn_w_o, attn_lam_q1, attn_lam_k1, attn_lam_q2, attn_lam_k2, attn_subln_g,
              conv_w_pw1, conv_b_pw1, conv_w_dw, conv_b_dw, conv_ln_g, conv_ln_b, conv_w_pw2, conv_b_pw2,
              hy_w_in, hy_b_in, hy_w_short, hy_b_short, hy_f_w1, hy_f_b1, hy_f_w2, hy_f_b2, hy_f_w3, hy_f_b3,
              hy_f_freq, hy_f_wout, hy_f_bias, hy_w_out, hy_b_out,
              ln1_g, ln1_b, ln2_g, ln2_b,
              moe_w_group, moe_b_group, moe_w_expert, moe_b_expert, moe_w_gate, moe_w_up, moe_w_down,
              ple_w_up, ple_w_gate):
    attn = (attn_w_qkv, attn_w_o, attn_lam_q1, attn_lam_k1, attn_lam_q2, attn_lam_k2, attn_subln_g)
    conv = (conv_w_pw1, conv_b_pw1, conv_w_dw, conv_b_dw, conv_ln_g, conv_ln_b, conv_w_pw2, conv_b_pw2)
    hyena = (hy_w_in, hy_b_in, hy_w_short, hy_b_short, hy_f_w1, hy_f_b1, hy_f_w2, hy_f_b2, hy_f_w3, hy_f_b3,
             hy_f_freq, hy_f_wout, hy_f_bias, hy_w_out, hy_b_out)
    norms = (ln1_g, ln1_b, ln2_g, ln2_b)
    moe = (moe_w_group, moe_b_group, moe_w_expert, moe_b_expert, moe_w_gate, moe_w_up, moe_w_down)
    ple = (ple_w_up, ple_w_gate)
    y_prompt = trunk(x_prompt, p_prompt, attn, conv, hyena, norms, moe, ple)
    y_sample = trunk(x_sample, p_sample, attn, conv, hyena, norms, moe, ple)
    return (y_prompt, y_sample)
```

```python
import functools
import math

import jax
import jax.numpy as jnp
from jax import lax
from jax.experimental import pallas as pl
from jax.experimental.pallas import tpu as pltpu

F32 = jnp.float32
BF16 = jnp.bfloat16
HIGHEST = lax.Precision.HIGHEST

D_MODEL = 1024
DEPTH = 4
N_MIXERS = 3
N_HEADS = 8
HEAD_DIM = 64
CONV_WIDTH = 31
HYENA_ORDER = 2
SHORT_WIDTH = 3
POS_BANDS = 16
FILTER_WIDTH = 64
MAX_DECAY = math.log(1e-2) / 0.3
MIN_DECAY = math.log(1e-2) / 1.5
N_GROUPS = 4
EXPERTS_PER_GROUP = 4
N_EXPERTS = N_GROUPS * EXPERTS_PER_GROUP
EXPERT_FF = 512
PLE_DIM = 256
ALPHA = (2 * DEPTH) ** 0.25
LN_EPS = 1e-5
LOG2E = 1.4426950408889634

HALO = 16
CONV_ROWS = 16
ROUTE_ROWS = 24
MOE_TM = 1024
MOE_CAP = 128
NT_DIMS = (((1,), (1,)), ((), ()))


def _cp(sem, vmem_mb):
    return pltpu.CompilerParams(dimension_semantics=sem, vmem_limit_bytes=vmem_mb << 20)


def _layer_norm(z, g, b):
    mu = jnp.mean(z, axis=-1, keepdims=True)
    zc = z - mu
    var = jnp.mean(zc * zc, axis=-1, keepdims=True)
    return zc * lax.rsqrt(var + LN_EPS) * g + b


def _sigmoid(x):
    return 1.0 / (1.0 + jnp.exp(-x))


def _linear_body(x_ref, w_ref, b_ref, o_ref, *, nc, glu):
    x = x_ref[...].astype(BF16)
    n_out = o_ref.shape[-1]
    for n0 in range(0, n_out, nc):
        a = jnp.dot(x, w_ref[:, n0:n0 + nc], preferred_element_type=F32) + b_ref[:, n0:n0 + nc]
        if glu:
            g = (jnp.dot(x, w_ref[:, n_out + n0:n_out + n0 + nc], preferred_element_type=F32)
                 + b_ref[:, n_out + n0:n_out + n0 + nc])
            a = a * _sigmoid(g)
        o_ref[:, n0:n0 + nc] = a.astype(o_ref.dtype)


def linear(x, w, b, *, out_dtype, glu=False, tm=512, nc=512, name="linear"):
    t, k = x.shape
    n = w.shape[1]
    n_out = n // 2 if glu else n
    return pl.pallas_call(
        functools.partial(_linear_body, nc=nc, glu=glu),
        grid=(t // tm,),
        in_specs=[pl.BlockSpec((tm, k), lambda i: (i, 0)),
                  pl.BlockSpec((k, n), lambda i: (0, 0)),
                  pl.BlockSpec((1, n), lambda i: (0, 0))],
        out_specs=pl.BlockSpec((tm, n_out), lambda i: (i, 0)),
        out_shape=jax.ShapeDtypeStruct((t, n_out), out_dtype),
        compiler_params=_cp(("parallel",), 48),
        name=name,
    )(x, w, b)


def _attn_body(lam_ref, g_ref, q_ref, k_ref, v_ref, o_ref, m_scr, l_scr, acc_scr, *, tq, tk, lam_init):
    h = pl.program_id(1)
    qi = pl.program_id(2)
    nk = k_ref.shape[0] // tk
    slope2 = jnp.exp2(-(jnp.full((1, 1), h, jnp.int32) + 1).astype(F32)) * LOG2E

    q = q_ref[...]
    lane = lax.broadcasted_iota(jnp.int32, q.shape, 1)
    zero = jnp.zeros_like(q)
    qq = jnp.concatenate([jnp.where(lane < HEAD_DIM, q, zero),
                          jnp.where(lane >= HEAD_DIM, q, zero)], axis=0)

    m_scr[...] = jnp.full(m_scr.shape, -jnp.inf, F32)
    l_scr[...] = jnp.zeros(l_scr.shape, F32)
    acc_scr[...] = jnp.zeros(acc_scr.shape, F32)

    d0 = (lax.broadcasted_iota(jnp.int32, (tq, tk), 0)
          - lax.broadcasted_iota(jnp.int32, (tq, tk), 1)).astype(F32)

    def step(ki, carry):
        ks = pl.multiple_of(ki * tk, tk)
        kb = k_ref[pl.ds(ks, tk), :]
        vb = v_ref[pl.ds(ks, tk), :]
        s = lax.dot_general(qq, kb, NT_DIMS, preferred_element_type=F32)
        c = (qi * tq - ki * tk).astype(F32)
        bias = slope2 * jnp.abs(d0 + c)
        s = s - jnp.concatenate([bias, bias], axis=0)
        m_old = m_scr[...]
        m_new = jnp.maximum(m_old, jnp.max(s, axis=-1, keepdims=True))
        p = jnp.exp2(s - m_new)
        alpha = jnp.exp2(m_old - m_new)
        l_scr[...] = alpha * l_scr[...] + jnp.sum(p, axis=-1, keepdims=True)
        acc_scr[...] = alpha * acc_scr[...] + jnp.dot(p.astype(BF16), vb, preferred_element_type=F32)
        m_scr[...] = m_new
        return carry

    lax.fori_loop(0, nk, step, 0)

    lam = lam_ref[...]
    lam_full = (jnp.exp(jnp.sum(lam[0:1] * lam[1:2], axis=-1, keepdims=True))
                - jnp.exp(jnp.sum(lam[2:3] * lam[3:4], axis=-1, keepdims=True)) + lam_init)
    o = acc_scr[...] / l_scr[...]
    o = o[:tq] - lam_full * o[tq:]
    ms = jnp.mean(o * o, axis=-1, keepdims=True)
    o = o * lax.rsqrt(ms + LN_EPS) * g_ref[...] * (1.0 - lam_init)
    o_ref[...] = o.astype(o_ref.dtype)


def diff_attention(qkv, lam, subln_g, *, lam_init, tq=256, tk=512):
    b, l, _ = qkv.shape
    tk = min(tk, l)
    hd2 = 2 * HEAD_DIM
    return pl.pallas_call(
        functools.partial(_attn_body, tq=tq, tk=tk, lam_init=lam_init),
        grid=(b, N_HEADS, l // tq),
        in_specs=[pl.BlockSpec((4, HEAD_DIM), lambda bi, h, qi: (0, 0)),
                  pl.BlockSpec((1, hd2), lambda bi, h, qi: (0, 0)),
                  pl.BlockSpec((None, tq, hd2), lambda bi, h, qi: (bi, qi, h)),
                  pl.BlockSpec((None, l, hd2), lambda bi, h, qi: (bi, 0, N_HEADS + h)),
                  pl.BlockSpec((None, l, hd2), lambda bi, h, qi: (bi, 0, 2 * N_HEADS + h))],
        out_specs=pl.BlockSpec((None, tq, hd2), lambda bi, h, qi: (bi, qi, h)),
        out_shape=jax.ShapeDtypeStruct((b, l, D_MODEL), BF16),
        scratch_shapes=[pltpu.VMEM((2 * tq, 1), F32), pltpu.VMEM((2 * tq, 1), F32),
                        pltpu.VMEM((2 * tq, hd2), F32)],
        compiler_params=_cp(("parallel", "parallel", "parallel"), 48),
        name="diff_attention",
    )(lam, subln_g, qkv, qkv, qkv)


def _route(x1, wr, br, comb_ref):
    lt = lax.dot_general(wr, x1, NT_DIMS, precision=HIGHEST, preferred_element_type=F32) + br
    gl = [lt[g:g + 1] for g in range(N_GROUPS)]
    gmax = jnp.maximum(jnp.maximum(gl[0], gl[1]), jnp.maximum(gl[2], gl[3]))
    gidx = jnp.where(gl[0] == gmax, 0, jnp.where(gl[1] == gmax, 1, jnp.where(gl[2] == gmax, 2, 3)))
    gw = 1.0 / (jnp.exp(gl[0] - gmax) + jnp.exp(gl[1] - gmax) + jnp.exp(gl[2] - gmax) + jnp.exp(gl[3] - gmax))
    el = []
    for j in range(EXPERTS_PER_GROUP):
        acc = jnp.zeros_like(gmax)
        for g in range(N_GROUPS):
            r = N_GROUPS + g * EXPERTS_PER_GROUP + j
            acc = jnp.where(gidx == g, lt[r:r + 1], acc)
        el.append(acc)
    v1 = jnp.maximum(jnp.maximum(el[0], el[1]), jnp.maximum(el[2], el[3]))
    i1 = jnp.where(el[0] == v1, 0, jnp.where(el[1] == v1, 1, jnp.where(el[2] == v1, 2, 3)))
    neg = jnp.full_like(v1, -jnp.inf)
    el2 = [jnp.where(i1 == j, neg, el[j]) for j in range(EXPERTS_PER_GROUP)]
    v2 = jnp.maximum(jnp.maximum(el2[0], el2[1]), jnp.maximum(el2[2], el2[3]))
    i2 = jnp.where(el2[0] == v2, 0, jnp.where(el2[1] == v2, 1, jnp.where(el2[2] == v2, 2, 3)))
    e21 = jnp.exp(v2 - v1)
    w1 = gw / (1.0 + e21)
    w2 = gw * e21 / (1.0 + e21)
    zero = jnp.zeros_like(v1)
    for g in range(N_GROUPS):
        for j in range(EXPERTS_PER_GROUP):
            wj = jnp.where(i1 == j, w1, jnp.where(i2 == j, w2, zero))
            r = g * EXPERTS_PER_GROUP + j
            comb_ref[r:r + 1, :] = jnp.where(gidx == g, wj, zero)


def _proj_ln_route_body(a_ref, w_ref, b_ref, x_ref, g_ref, beta_ref, wr_ref, br_ref, x1_ref, comb_ref):
    h = jnp.dot(a_ref[...].astype(BF16), w_ref[...], preferred_element_type=F32) + b_ref[...]
    x1 = _layer_norm(ALPHA * x_ref[...] + h, g_ref[...], beta_ref[...])
    x1_ref[...] = x1
    _route(x1, wr_ref[...], br_ref[...], comb_ref)


def proj_ln_route(a, w, b, x, ln_g, ln_b, wr, br, *, tm=512):
    t, k = a.shape
    d = D_MODEL
    const = lambda i: (0, 0)
    return pl.pallas_call(
        _proj_ln_route_body,
        grid=(t // tm,),
        in_specs=[pl.BlockSpec((tm, k), lambda i: (i, 0)),
                  pl.BlockSpec((k, d), const), pl.BlockSpec((1, d), const),
                  pl.BlockSpec((tm, d), lambda i: (i, 0)),
                  pl.BlockSpec((1, d), const), pl.BlockSpec((1, d), const),
                  pl.BlockSpec((ROUTE_ROWS, d), const), pl.BlockSpec((ROUTE_ROWS, 1), const)],
        out_specs=[pl.BlockSpec((tm, d), lambda i: (i, 0)),
                   pl.BlockSpec((N_EXPERTS, tm), lambda i: (0, i))],
        out_shape=[jax.ShapeDtypeStruct((t, d), F32), jax.ShapeDtypeStruct((N_EXPERTS, t), F32)],
        compiler_params=_cp(("parallel",), 48),
        name="proj_ln_route",
    )(a, w, b, x, ln_g, ln_b, wr, br)


def _moe_body(x1_ref, comb_ref, uw_ref, lw_ref, wg_ref, wu_ref, wd_ref, g2_ref, b2_ref,
              p_ref, up_ref, gate_ref, o_ref, xb_scr, y_scr, vrow_scr, vcol_scr, *, tm, cap):
    e = pl.program_id(1)

    @pl.when(e == 0)
    def _():
        xb_scr[...] = x1_ref[...].astype(BF16)
        y_scr[...] = jnp.zeros(y_scr.shape, F32)
        member = jnp.where(comb_ref[...] > 0.0, 1.0, 0.0).astype(BF16)
        vrow_scr[...] = jnp.dot(member, uw_ref[...], preferred_element_type=F32)
        vcol_scr[...] = lax.dot_general(lw_ref[...], member, NT_DIMS, preferred_element_type=F32)

    vrow = vrow_scr[pl.ds(e, 1), :]
    sel = lax.broadcasted_iota(jnp.int32, (tm, N_EXPERTS), 1) == e
    vcol = jnp.sum(jnp.where(sel, vcol_scr[...], 0.0), axis=1, keepdims=True)
    w = comb_ref[pl.ds(e, 1), :]
    w_hi = w.astype(BF16)
    r1 = w - w_hi.astype(F32)
    w_mid = r1.astype(BF16)
    w_lo = (r1 - w_mid.astype(F32)).astype(BF16)
    prow = lax.broadcasted_iota(jnp.int32, (8, tm), 0)
    w3 = jnp.where(prow == 0, w_hi.astype(F32),
                   jnp.where(prow == 1, w_mid.astype(F32),
                             jnp.where(prow == 2, w_lo.astype(F32), 0.0))).astype(BF16)
    count = (jnp.max(vrow) + 1.0) * 0.5
    n_chunk = (count.astype(jnp.int32) + (cap - 1)) // cap

    def chunk(c, carry):
        base = (c * cap).astype(F32)
        tgt_r = 2.0 * (base + lax.broadcasted_iota(jnp.int32, (cap, 1), 0).astype(F32)) + 1.0
        tgt_c = 2.0 * (base + lax.broadcasted_iota(jnp.int32, (1, cap), 1).astype(F32)) + 1.0
        gather = jnp.where(vrow == tgt_r, 1.0, 0.0).astype(BF16)
        scatter = jnp.where(vcol == tgt_c, 1.0, 0.0).astype(BF16)
        xg = jnp.dot(gather, xb_scr[...], preferred_element_type=F32).astype(BF16)
        ws = lax.dot_general(gather, w3, NT_DIMS, preferred_element_type=F32)
        wslot = ws[:, 0:1] + ws[:, 1:2] + ws[:, 2:3]
        hg = jnp.dot(xg, wg_ref[...], preferred_element_type=F32)
        hu = jnp.dot(xg, wu_ref[...], preferred_element_type=F32)
        hid = hg * _sigmoid(hg) * hu * wslot
        o = jnp.dot(hid.astype(BF16), wd_ref[...], preferred_element_type=F32)
        y_scr[...] += jnp.dot(scatter, o.astype(BF16), preferred_element_type=F32)
        return carry

    lax.fori_loop(0, n_chunk, chunk, 0)

    @pl.when(e == N_EXPERTS - 1)
    def _():
        x2 = _layer_norm(ALPHA * x1_ref[...] + y_scr[...], g2_ref[...], b2_ref[...])
        up = jnp.dot(p_ref[...].astype(BF16), up_ref[...], preferred_element_type=F32)
        gt = jnp.dot(x2.astype(BF16), gate_ref[...], preferred_element_type=F32)
        o_ref[...] = x2 + up * _sigmoid(gt)


def moe_ple(x1, comb, uw, lw, wg, wu, wd, ln_g, ln_b, p, up, gate, *, tm=MOE_TM, cap=MOE_CAP):
    t, d = x1.shape
    ff = EXPERT_FF
    const = lambda i, e: (0, 0)
    return pl.pallas_call(
        functools.partial(_moe_body, tm=tm, cap=cap),
        grid=(t // tm, N_EXPERTS),
        in_specs=[pl.BlockSpec((tm, d), lambda i, e: (i, 0)),
                  pl.BlockSpec((N_EXPERTS, tm), lambda i, e: (0, i)),
                  pl.BlockSpec((tm, tm), const), pl.BlockSpec((tm, tm), const),
                  pl.BlockSpec((None, d, ff), lambda i, e: (e, 0, 0)),
                  pl.BlockSpec((None, d, ff), lambda i, e: (e, 0, 0)),
                  pl.BlockSpec((None, ff, d), lambda i, e: (e, 0, 0)),
                  pl.BlockSpec((1, d), const), pl.BlockSpec((1, d), const),
                  pl.BlockSpec((tm, PLE_DIM), lambda i, e: (i, 0)),
                  pl.BlockSpec((PLE_DIM, d), const), pl.BlockSpec((d, d), const)],
        out_specs=pl.BlockSpec((tm, d), lambda i, e: (i, 0)),
        out_shape=jax.ShapeDtypeStruct((t, d), F32),
        scratch_shapes=[pltpu.VMEM((tm, d), BF16), pltpu.VMEM((tm, d), F32),
                        pltpu.VMEM((N_EXPERTS, tm), F32), pltpu.VMEM((tm, N_EXPERTS), F32)],
        compiler_params=_cp(("parallel", "arbitrary"), 56),
        name="moe_ple",
    )(x1, comb, uw, lw, wg, wu, wd, ln_g, ln_b, p, up, gate)


def _dwconv_body(prev_ref, cur_ref, next_ref, w_ref, b_ref, g_ref, beta_ref, o_ref, buf_scr, *, width, tm, norm_act):
    i = pl.program_id(1)
    n = pl.num_programs(1)
    pad = width // 2
    halo = jnp.zeros(prev_ref.shape, F32)
    buf_scr[0:HALO, :] = jnp.where(i > 0, prev_ref[...], halo)
    buf_scr[HALO:HALO + tm, :] = cur_ref[...]
    buf_scr[HALO + tm:2 * HALO + tm, :] = jnp.where(i < n - 1, next_ref[...], halo)

    def rows(r, carry):
        r0 = pl.multiple_of(r * CONV_ROWS, CONV_ROWS)
        win = buf_scr[pl.ds(r0, CONV_ROWS + 2 * HALO), :]
        acc = jnp.zeros((CONV_ROWS, win.shape[1]), F32) + b_ref[...]
        for k in range(width):
            off = HALO - pad + k
            acc = acc + win[off:off + CONV_ROWS, :] * w_ref[k:k + 1, :]
        if norm_act:
            acc = _layer_norm(acc, g_ref[...], beta_ref[...])
            acc = acc * _sigmoid(acc)
        o_ref[pl.ds(r0, CONV_ROWS), :] = acc.astype(o_ref.dtype)
        return carry

    lax.fori_loop(0, tm // CONV_ROWS, rows, 0)


def dwconv(x, w, b, ln_g, ln_b, *, norm_act, out_dtype, tm=256):
    bsz, l, c = x.shape
    width = w.shape[0]
    tc = D_MODEL
    nb = tm // HALO
    last = l // HALO - 1
    return pl.pallas_call(
        functools.partial(_dwconv_body, width=width, tm=tm, norm_act=norm_act),
        grid=(bsz, l // tm, c // tc),
        in_specs=[pl.BlockSpec((None, HALO, tc), lambda bi, i, j: (bi, jnp.maximum(i * nb - 1, 0), j)),
                  pl.BlockSpec((None, tm, tc), lambda bi, i, j: (bi, i, j)),
                  pl.BlockSpec((None, HALO, tc), lambda bi, i, j: (bi, jnp.minimum((i + 1) * nb, last), j)),
                  pl.BlockSpec((width, tc), lambda bi, i, j: (0, j)),
                  pl.BlockSpec((1, tc), lambda bi, i, j: (0, j)),
                  pl.BlockSpec((1, tc), lambda bi, i, j: (0, 0)),
                  pl.BlockSpec((1, tc), lambda bi, i, j: (0, 0))],
        out_specs=pl.BlockSpec((None, None, tm, tc), lambda bi, i, j: (j, bi, i, 0)),
        out_shape=jax.ShapeDtypeStruct((c // tc, bsz, l, tc), out_dtype),
        scratch_shapes=[pltpu.VMEM((tm + 2 * HALO, tc), F32)],
        compiler_params=_cp(("parallel", "parallel", "parallel"), 32),
        name="dwconv%d" % width,
    )(x, x, x, w, b, ln_g, ln_b)


def _filter_body(w1t_ref, w1c_ref, w1s_ref, b1_ref, w2_ref, b2_ref, w3_ref, b3_ref, freq_ref, wout_ref,
                 o_ref, *, tl, length):
    i = pl.program_id(0)
    pos = (i * tl + lax.broadcasted_iota(jnp.int32, (tl, 1), 0)).astype(F32)
    t = pos / float(length - 1)
    ang = (2.0 * math.pi / length) * pos
    band = lax.broadcasted_iota(jnp.int32, (1, POS_BANDS), 1).astype(F32)
    bands = 1e-4 + band * ((POS_BANDS - 1 - 1e-4) / (POS_BANDS - 1))
    fw = bands * ang
    freq = freq_ref[...]
    dot = functools.partial(jnp.dot, precision=HIGHEST, preferred_element_type=F32)
    pre = t * w1t_ref[...] + dot(jnp.cos(fw), w1c_ref[...]) - dot(jnp.sin(fw), w1s_ref[...]) + b1_ref[...]
    hcur = jnp.sin(freq * pre)
    hcur = jnp.sin(freq * (dot(hcur, w2_ref[...]) + b2_ref[...]))
    hcur = jnp.sin(freq * (dot(hcur, w3_ref[...]) + b3_ref[...]))
    ch = lax.broadcasted_iota(jnp.int32, (1, D_MODEL), 1).astype(F32)
    deltas = jnp.abs(MIN_DECAY + ch * ((MAX_DECAY - MIN_DECAY) / (D_MODEL - 1)))
    decay = jnp.exp(-t * deltas)
    for j in range(2 * HYENA_ORDER):
        o_ref[j] = dot(hcur, wout_ref[:, j * D_MODEL:(j + 1) * D_MODEL]) * decay


def hyena_filters(length, w1, b1, w2, b2, w3, b3, freq, wout, *, tl=256):
    fwid = FILTER_WIDTH
    const = lambda i: (0, 0)
    nf = 2 * HYENA_ORDER
    return pl.pallas_call(
        functools.partial(_filter_body, tl=tl, length=length),
        grid=(length // tl,),
        in_specs=[pl.BlockSpec((1, fwid), const), pl.BlockSpec((POS_BANDS, fwid), const),
                  pl.BlockSpec((POS_BANDS, fwid), const), pl.BlockSpec((1, fwid), const),
                  pl.BlockSpec((fwid, fwid), const), pl.BlockSpec((1, fwid), const),
                  pl.BlockSpec((fwid, fwid), const), pl.BlockSpec((1, fwid), const),
                  pl.BlockSpec((1, fwid), const), pl.BlockSpec((fwid, nf * D_MODEL), const)],
        out_specs=pl.BlockSpec((nf, tl, D_MODEL), lambda i: (0, i, 0)),
        out_shape=jax.ShapeDtypeStruct((nf, length, D_MODEL), F32),
        compiler_params=_cp(("parallel",), 32),
        name="hyena_filters",
    )(w1[0:1], w1[1:1 + POS_BANDS], w1[1 + POS_BANDS:], b1, w2, b2, w3, b3, freq, wout)


def _dft_tables(n1, n2s):
    n = n1 * n2s
    m1 = (n1 // 2 + 1 + 3) // 4 * 4
    k1 = jnp.arange(m1, dtype=jnp.int32)
    valid = (k1 <= n1 // 2)
    a = jnp.arange(n1 // 2, dtype=jnp.int32)
    ang_a = (2.0 * math.pi / n1) * ((k1[:, None] * a[None, :]) % n1).astype(F32)
    ca = jnp.where(valid[:, None], jnp.cos(ang_a), 0.0)
    sa = jnp.where(valid[:, None], jnp.sin(ang_a), 0.0)
    g_a = jnp.stack([ca, -sa], axis=1).reshape(2 * m1, n1 // 2)
    weight = jnp.where(valid, jnp.where((k1 == 0) | (k1 == n1 // 2), 1.0, 2.0), 0.0)
    g_ai = jnp.stack([ca.T, -sa.T], axis=2).reshape(n1 // 2, 2 * m1)
    b = jnp.arange(n2s, dtype=jnp.int32)
    k2 = jnp.arange(n2s, dtype=jnp.int32)
    idx = (b[None, None, :] * (k2[None, :, None] * n1 + k1[:, None, None])) % n
    ang_c = (2.0 * math.pi / n) * idx.astype(F32)
    cc, sc = jnp.cos(ang_c), jnp.sin(ang_c)
    g_c = jnp.concatenate([jnp.concatenate([cc, sc], axis=2),
                           jnp.concatenate([-sc, cc], axis=2)], axis=1)
    cct = jnp.swapaxes(cc, 1, 2) * (weight / n)[:, None, None]
    sct = jnp.swapaxes(sc, 1, 2) * (weight / n)[:, None, None]
    g_ci = jnp.concatenate([jnp.concatenate([cct, -sct], axis=2),
                            jnp.concatenate([sct, cct], axis=2)], axis=1)
    return m1, g_a, g_c, g_ci, g_ai


def _bmm_body(g_ref, x_ref, o_ref, *, bt):
    for bi in range(bt):
        o_ref[bi] = jnp.dot(g_ref[...], x_ref[bi], precision=HIGHEST, preferred_element_type=F32)


def _bmm_spec_body(g1_ref, g2_ref, x_ref, hf_ref, hb_ref, o_ref, *, bt):
    half = g1_ref.shape[0] // 2
    hr = hf_ref[:half] + hb_ref[:half]
    hi = hf_ref[half:] - hb_ref[half:]
    for bi in range(bt):
        xs = jnp.dot(g1_ref[...], x_ref[bi], precision=HIGHEST, preferred_element_type=F32)
        xr, xi = xs[:half], xs[half:]
        ys = jnp.concatenate([xr * hr - xi * hi, xr * hi + xi * hr], axis=0)
        o_ref[bi] = jnp.dot(g2_ref[...], ys, precision=HIGHEST, preferred_element_type=F32)


def _bmm_gate_body(g_ref, x_ref, v_ref, gate_ref, bias_ref, o_ref, *, bt):
    for bi in range(bt):
        y = jnp.dot(g_ref[...], x_ref[bi], precision=HIGHEST, preferred_element_type=F32)
        o_ref[bi] = gate_ref[bi] * (y + v_ref[bi] * bias_ref[...])


def _bmm(g, x, xsel, *, bt, tc, name):
    _, bsz, s, k, c = x.shape
    sg, r, _ = g.shape
    gmap = (lambda si, bi, ci: (si, 0, 0)) if sg > 1 else (lambda si, bi, ci: (0, 0, 0))
    return pl.pallas_call(
        functools.partial(_bmm_body, bt=bt),
        grid=(s, bsz // bt, c // tc),
        in_specs=[pl.BlockSpec((None, r, k), gmap),
                  pl.BlockSpec((None, bt, None, k, tc), lambda si, bi, ci: (xsel, bi, si, 0, ci))],
        out_specs=pl.BlockSpec((bt, None, r, tc), lambda si, bi, ci: (bi, si, 0, ci)),
        out_shape=jax.ShapeDtypeStruct((bsz, s, r, c), F32),
        compiler_params=_cp(("parallel", "parallel", "parallel"), 48),
        name=name,
    )(g, x)


def _bmm_spec(g1, g2, x, hspec, order, *, bt, name):
    bsz, s, r, c = x.shape
    slab = lambda si, bi: (si, 0, 0)
    return pl.pallas_call(
        functools.partial(_bmm_spec_body, bt=bt),
        grid=(s, bsz // bt),
        in_specs=[pl.BlockSpec((None, r, r), slab), pl.BlockSpec((None, r, r), slab),
                  pl.BlockSpec((bt, None, r, c), lambda si, bi: (bi, si, 0, 0)),
                  pl.BlockSpec((None, None, r, c), lambda si, bi: (2 * order, si, 0, 0)),
                  pl.BlockSpec((None, None, r, c), lambda si, bi: (2 * order + 1, si, 0, 0))],
        out_specs=pl.BlockSpec((bt, None, r, c), lambda si, bi: (bi, si, 0, 0)),
        out_shape=jax.ShapeDtypeStruct((bsz, s, r, c), F32),
        compiler_params=_cp(("parallel", "parallel"), 48),
        name=name,
    )(g1, g2, x, hspec, hspec)


def _bmm_gate(g, x, v, vsel, gates, gsel, bias, *, bt, tc, name):
    bsz, _, k, c = x.shape
    r = g.shape[0]
    return pl.pallas_call(
        functools.partial(_bmm_gate_body, bt=bt),
        grid=(bsz // bt, c // tc),
        in_specs=[pl.BlockSpec((r, k), lambda bi, ci: (0, 0)),
                  pl.BlockSpec((bt, None, k, tc), lambda bi, ci: (bi, 0, 0, ci)),
                  pl.BlockSpec((None, bt, r, tc), lambda bi, ci: (vsel, bi, 0, ci)),
                  pl.BlockSpec((None, bt, r, tc), lambda bi, ci: (gsel, bi, 0, ci)),
                  pl.BlockSpec((1, tc), lambda bi, ci: (0, ci))],
        out_specs=pl.BlockSpec((bt, r, tc), lambda bi, ci: (bi, 0, ci)),
        out_shape=jax.ShapeDtypeStruct((bsz, r, c), F32),
        compiler_params=_cp(("parallel", "parallel"), 48),
        name=name,
    )(g, x, v, gates, bias)


def _fft_plan(length):
    n = 2 * length
    n1 = 256 if n >= 32768 else 64
    return n1, n // n1


def hyena_long_convs(xs, filt, f_bias):
    _, bsz, length, d = xs.shape
    n1, n2s = _fft_plan(length)
    m1, g_a, g_c, g_ci, g_ai = _dft_tables(n1, n2s)
    half = n1 // 2
    c = n2s * d
    bt = 1 if length >= 8192 else min(8, bsz)
    tc = 4096

    def forward_a(x5, sel, nb, name):
        return _bmm(g_a[None], x5, sel, bt=min(bt, nb), tc=tc, name=name).reshape(nb, m1, 2 * n2s, d)

    fa = forward_a(filt.reshape(1, 4, 1, half, c), 0, 4, "filt_dft_a")
    hspec = _bmm(g_c, fa[None], 0, bt=1, tc=d, name="filt_dft_c")
    bias_t = jnp.tile(f_bias, (1, n2s))
    xs5 = xs.reshape(3, bsz, 1, half, c)
    xs4 = xs.reshape(3, bsz, half, c)
    z5, zsel = xs5, 2
    z4 = xs4
    out = None
    for order in range(HYENA_ORDER):
        sa = forward_a(z5, zsel, bsz, "conv_dft_a")
        sb = _bmm_spec(g_c, g_ci, sa, hspec, order, bt=bt, name="conv_spec")
        out = _bmm_gate(g_ai, sb.reshape(bsz, 1, 2 * m1, c), z4, zsel, xs4, order,
                        bias_t[order:order + 1], bt=bt, tc=tc, name="conv_idft_gate")
        z5, zsel = out.reshape(1, bsz, 1, half, c), 0
        z4 = out[None]
    return out.reshape(bsz, length, d)


def _prefix_tables(tm):
    r = jnp.arange(tm, dtype=jnp.int32)
    upper = jnp.where(r[:, None] < r[None, :], 2.0, jnp.where(r[:, None] == r[None, :], 1.0, 0.0))
    return upper.astype(BF16), upper.T.astype(BF16)


def _trunk(x, p, wts):
    bsz, length, d = x.shape
    t = bsz * length
    xf = x.reshape(t, d)
    zeros_d = jnp.zeros((1, d), F32)
    ones_d = jnp.ones((1, d), F32)
    uw, lw = _prefix_tables(MOE_TM)
    for i in range(DEPTH):
        j, kind = i // N_MIXERS, i % N_MIXERS
        if kind == 0:
            lam_init = 0.8 - 0.6 * math.exp(-0.3 * i)
            qkv = linear(xf, wts["attn_w_qkv"][j], jnp.zeros((1, 3 * d), F32), out_dtype=BF16, name="attn_qkv")
            a = diff_attention(qkv.reshape(bsz, length, 3 * d), wts["attn_lam"][j], wts["attn_subln_g"][j],
                               lam_init=lam_init).reshape(t, d)
            w_o, b_o = wts["attn_w_o"][j], zeros_d
        elif kind == 1:
            hglu = linear(xf, wts["conv_w_pw1"][j], wts["conv_b_pw1"][j], out_dtype=F32, glu=True, name="conv_pw1")
            a = dwconv(hglu.reshape(bsz, length, d), wts["conv_w_dw"][j], wts["conv_b_dw"][j],
                       wts["conv_ln_g"][j], wts["conv_ln_b"][j], norm_act=True, out_dtype=BF16).reshape(t, d)
            w_o, b_o = wts["conv_w_pw2"][j], wts["conv_b_pw2"][j]
        else:
            u = linear(xf, wts["hy_w_in"][j], wts["hy_b_in"][j], out_dtype=F32, name="hyena_in")
            xs = dwconv(u.reshape(bsz, length, 3 * d), wts["hy_w_short"][j], wts["hy_b_short"][j],
                        ones_d, zeros_d, norm_act=False, out_dtype=F32)
            filt = hyena_filters(length, wts["hy_f_w1"][j], wts["hy_f_b1"][j], wts["hy_f_w2"][j], wts["hy_f_b2"][j],
                                 wts["hy_f_w3"][j], wts["hy_f_b3"][j], wts["hy_f_freq"][j], wts["hy_f_wout"][j])
            a = hyena_long_convs(xs, filt, wts["hy_f_bias"][j]).reshape(t, d)
            w_o, b_o = wts["hy_w_out"][j], wts["hy_b_out"][j]
        x1, comb = proj_ln_route(a, w_o, b_o, xf, wts["ln1_g"][i], wts["ln1_b"][i],
                                 wts["route_w"][i], wts["route_b"][i])
        xf = moe_ple(x1, comb, uw, lw, wts["moe_w_gate"][i], wts["moe_w_up"][i], wts["moe_w_down"][i],
                     wts["ln2_g"][i], wts["ln2_b"][i], p[i].reshape(t, PLE_DIM),
                     wts["ple_w_up"][i], wts["ple_w_gate"][i])
    return xf.reshape(bsz, length, d)


def kernel(x_prompt, x_sample, p_prompt, p_sample, attn_w_qkv, attn_w_o, attn_lam_q1, attn_lam_k1, attn_lam_q2, attn_lam_k2, attn_subln_g, conv_w_pw1, conv_b_pw1, conv_w_dw, conv_b_dw, conv_ln_g, conv_ln_b, conv_w_pw2, conv_b_pw2, hy_w_in, hy_b_in, hy_w_short, hy_b_short, hy_f_w1, hy_f_b1, hy_f_w2, hy_f_b2, hy_f_w3, hy_f_b3, hy_f_freq, hy_f_wout, hy_f_bias, hy_w_out, hy_b_out, ln1_g, ln1_b, ln2_g, ln2_b, moe_w_group, moe_b_group, moe_w_expert, moe_b_expert, moe_w_gate, moe_w_up, moe_w_down, ple_w_up, ple_w_gate):
    d = D_MODEL
    row = lambda a: a[:, None, :]
    q_scale = jnp.concatenate([jnp.full((d,), HEAD_DIM ** -0.5 * LOG2E, F32), jnp.ones((2 * d,), F32)])
    route_w = jnp.concatenate([jnp.swapaxes(moe_w_group, 1, 2), jnp.swapaxes(moe_w_expert, 1, 2),
                               jnp.zeros((DEPTH, ROUTE_ROWS - N_GROUPS - N_EXPERTS, d), F32)], axis=1)
    route_b = jnp.concatenate([moe_b_group, moe_b_expert,
                               jnp.zeros((DEPTH, ROUTE_ROWS - N_GROUPS - N_EXPERTS), F32)], axis=1)[:, :, None]
    wts = {
        "attn_w_qkv": (attn_w_qkv * q_scale).astype(BF16),
        "attn_w_o": attn_w_o.astype(BF16),
        "attn_lam": jnp.stack([attn_lam_q1, attn_lam_k1, attn_lam_q2, attn_lam_k2], axis=1),
        "attn_subln_g": row(attn_subln_g),
        "conv_w_pw1": conv_w_pw1.astype(BF16), "conv_b_pw1": row(conv_b_pw1),
        "conv_w_dw": conv_w_dw, "conv_b_dw": row(conv_b_dw),
        "conv_ln_g": row(conv_ln_g), "conv_ln_b": row(conv_ln_b),
        "conv_w_pw2": conv_w_pw2.astype(BF16), "conv_b_pw2": row(conv_b_pw2),
        "hy_w_in": hy_w_in.astype(BF16), "hy_b_in": row(hy_b_in),
        "hy_w_short": hy_w_short, "hy_b_short": row(hy_b_short),
        "hy_f_w1": hy_f_w1, "hy_f_b1": row(hy_f_b1), "hy_f_w2": hy_f_w2, "hy_f_b2": row(hy_f_b2),
        "hy_f_w3": hy_f_w3, "hy_f_b3": row(hy_f_b3), "hy_f_freq": row(hy_f_freq), "hy_f_wout": hy_f_wout,
        "hy_f_bias": hy_f_bias,
        "hy_w_out": hy_w_out.astype(BF16), "hy_b_out": row(hy_b_out),
        "ln1_g": row(ln1_g), "ln1_b": row(ln1_b), "ln2_g": row(ln2_g), "ln2_b": row(ln2_b),
        "route_w": route_w, "route_b": route_b,
        "moe_w_gate": moe_w_gate.astype(BF16), "moe_w_up": moe_w_up.astype(BF16),
        "moe_w_down": moe_w_down.astype(BF16),
        "ple_w_up": ple_w_up.astype(BF16), "ple_w_gate": ple_w_gate.astype(BF16),
    }
    y_prompt = _trunk(x_prompt, p_prompt, wts)
    y_sample = _trunk(x_sample, p_sample, wts)
    return (y_prompt, y_sample)
```

```python
import functools
import math

import jax
import jax.numpy as jnp
from jax import lax
from jax.experimental import pallas as pl
from jax.experimental.pallas import tpu as pltpu

F32 = jnp.float32
BF16 = jnp.bfloat16
HIGHEST = lax.Precision.HIGHEST

D_MODEL = 1024
DEPTH = 4
N_MIXERS = 3
N_HEADS = 8
HEAD_DIM = 64
CONV_WIDTH = 31
HYENA_ORDER = 2
SHORT_WIDTH = 3
POS_BANDS = 16
FILTER_WIDTH = 64
MAX_DECAY = math.log(1e-2) / 0.3
MIN_DECAY = math.log(1e-2) / 1.5
N_GROUPS = 4
EXPERTS_PER_GROUP = 4
N_EXPERTS = N_GROUPS * EXPERTS_PER_GROUP
EXPERT_FF = 512
PLE_DIM = 256
ALPHA = (2 * DEPTH) ** 0.25
LN_EPS = 1e-5
LOG2E = 1.4426950408889634

HALO = 16
CONV_ROWS = 16
ROUTE_ROWS = 24
MOE_TM = 1024
MOE_CAP = 128
ATT_TQ = 512
ATT_TK = 1024
ATT_CW = 256
ATT_FILL = 512
NT_DIMS = (((1,), (1,)), ((), ()))


def _cp(sem, vmem_mb):
    return pltpu.CompilerParams(dimension_semantics=sem, vmem_limit_bytes=vmem_mb << 20)


def _layer_norm(z, g, b):
    mu = jnp.mean(z, axis=-1, keepdims=True)
    zc = z - mu
    var = jnp.mean(zc * zc, axis=-1, keepdims=True)
    return zc * lax.rsqrt(var + LN_EPS) * g + b


def _sigmoid(x):
    return 1.0 / (1.0 + jnp.exp(-x))


def _linear_body(x_ref, w_ref, b_ref, o_ref, *, nc, glu):
    x = x_ref[...].astype(BF16)
    n_out = o_ref.shape[-1]
    for n0 in range(0, n_out, nc):
        a = jnp.dot(x, w_ref[:, n0:n0 + nc], preferred_element_type=F32) + b_ref[:, n0:n0 + nc]
        if glu:
            g = (jnp.dot(x, w_ref[:, n_out + n0:n_out + n0 + nc], preferred_element_type=F32)
                 + b_ref[:, n_out + n0:n_out + n0 + nc])
            a = a * _sigmoid(g)
        o_ref[:, n0:n0 + nc] = a.astype(o_ref.dtype)


def linear(x, w, b, *, out_dtype, glu=False, tm=512, nc=512, name="linear"):
    t, k = x.shape
    n = w.shape[1]
    n_out = n // 2 if glu else n
    return pl.pallas_call(
        functools.partial(_linear_body, nc=nc, glu=glu),
        grid=(t // tm,),
        in_specs=[pl.BlockSpec((tm, k), lambda i: (i, 0)),
                  pl.BlockSpec((k, n), lambda i: (0, 0)),
                  pl.BlockSpec((1, n), lambda i: (0, 0))],
        out_specs=pl.BlockSpec((tm, n_out), lambda i: (i, 0)),
        out_shape=jax.ShapeDtypeStruct((t, n_out), out_dtype),
        compiler_params=_cp(("parallel",), 48),
        name=name,
    )(x, w, b)


def _split3(x):
    hi = x.astype(BF16).astype(F32)
    mid = (x - hi).astype(BF16).astype(F32)
    lo = (x - hi - mid).astype(BF16).astype(F32)
    return [hi, mid, lo]


def _lane_table(lane, values, first):
    out = jnp.zeros(lane.shape, F32)
    for n, val in enumerate(values):
        out = jnp.where(lane == first + n, val, out)
    return out


def _attn_body(lam_ref, g_ref, q_ref, k_ref, v_ref, o_ref, kp_scr, vt_scr, d0_scr, qq_scr, s0_scr, s1_scr,
               m_scr, l_scr, acc_scr, *, tq, tk, lam_init):
    h = pl.program_id(1)
    qi = pl.program_id(2)
    length = k_ref.shape[0]
    nk = length // tk
    n_other = nk - 1
    hd2 = 2 * HEAD_DIM
    slope2 = jnp.exp2(-(jnp.full((1, 1), h, jnp.int32) + 1).astype(F32)) * LOG2E
    c1 = _split3(slope2)
    c128 = [128.0 * c for c in c1]

    @pl.when(qi == 0)
    def _():
        d0_scr[...] = (lax.broadcasted_iota(jnp.int32, (tk, tq), 1)
                       - lax.broadcasted_iota(jnp.int32, (tk, tq), 0)).astype(F32)
        lane = lax.broadcasted_iota(jnp.int32, (ATT_FILL, hd2), 1)
        consts = _lane_table(lane, [-c for c in c128] + [-c for c in c1], 0)
        eye = jnp.where(lax.broadcasted_iota(jnp.int32, (hd2, hd2), 0)
                        == lax.broadcasted_iota(jnp.int32, (hd2, hd2), 1), 1.0, 0.0).astype(BF16)

        def fill(ci, carry):
            r0 = pl.multiple_of(ci * ATT_FILL, ATT_FILL)
            pos = r0 + lax.broadcasted_iota(jnp.int32, (ATT_FILL, hd2), 0)
            hi_digit = jnp.right_shift(pos, 7).astype(F32)
            lo_digit = jnp.bitwise_and(pos, 127).astype(F32)
            aug = jnp.where(lane < 6, consts, jnp.where(lane < 9, hi_digit, jnp.where(lane < 12, lo_digit, 0.0)))
            kp_scr[pl.ds(r0, ATT_FILL), 0:hd2] = k_ref[pl.ds(r0, ATT_FILL), :]
            kp_scr[pl.ds(r0, ATT_FILL), hd2:2 * hd2] = aug.astype(BF16)
            vt_scr[:, pl.ds(r0, ATT_FILL)] = lax.dot_general(
                eye, v_ref[pl.ds(r0, ATT_FILL), :], NT_DIMS, preferred_element_type=F32).astype(BF16)
            return carry

        lax.fori_loop(0, length // ATT_FILL, fill, 0)

    q = q_ref[...]
    lane = lax.broadcasted_iota(jnp.int32, (tq, hd2), 1)
    zero = jnp.zeros_like(q)
    tpos = qi * tq + lax.broadcasted_iota(jnp.int32, (tq, hd2), 0)
    hi_digit = jnp.right_shift(tpos, 7).astype(F32)
    lo_digit = jnp.bitwise_and(tpos, 127).astype(F32)
    consts = _lane_table(lane, c128 + c1, 6)
    augq = jnp.where(lane < 3, hi_digit, jnp.where(lane < 6, lo_digit, consts))
    for var, aug in enumerate((augq.astype(BF16), (-augq).astype(BF16))):
        qq_scr[var, 0:tq, 0:hd2] = jnp.where(lane < HEAD_DIM, q, zero)
        qq_scr[var, tq:2 * tq, 0:hd2] = jnp.where(lane >= HEAD_DIM, q, zero)
        qq_scr[var, 0:tq, hd2:2 * hd2] = aug
        qq_scr[var, tq:2 * tq, hd2:2 * hd2] = aug

    m_scr[...] = jnp.full(m_scr.shape, -jnp.inf, F32)
    l_scr[...] = jnp.zeros(l_scr.shape, F32)
    acc_scr[...] = jnp.zeros(acc_scr.shape, F32)

    kd = (qi * tq) // tk

    def other_block(n):
        return jnp.where(n < n_other, jnp.where(n < kd, n, n + 1), kd)

    def scores(blk, s_scr):
        ks = pl.multiple_of(blk * tk, tk)
        var = (blk > kd).astype(jnp.int32)
        s_scr[...] = lax.dot_general(kp_scr[pl.ds(ks, tk), :], qq_scr[var], NT_DIMS, preferred_element_type=F32)

    def softmax_pv(blk, s_scr, diag):
        ks = pl.multiple_of(blk * tk, tk)
        cdiag = (qi * tq - blk * tk).astype(F32)
        vt = vt_scr[:, pl.ds(ks, tk)]
        m_all = m_scr[...]
        l_all = l_scr[...]
        m_out, l_out = [], []
        for cg in range(2 * tq // ATT_CW):
            c0 = cg * ATT_CW
            s = s_scr[:, c0:c0 + ATT_CW]
            if diag:
                dc = c0 % tq
                s = s + jnp.minimum(d0_scr[:, dc:dc + ATT_CW] + cdiag, 0.0) * (2.0 * slope2)
            m_old = m_all[:, c0:c0 + ATT_CW]
            m_new = jnp.maximum(m_old, jnp.max(s, axis=0, keepdims=True))
            p = jnp.exp2(s - m_new)
            alpha = jnp.exp2(m_old - m_new)
            l_out.append(alpha * l_all[:, c0:c0 + ATT_CW] + jnp.sum(p, axis=0, keepdims=True))
            m_out.append(m_new)
            pv = jnp.dot(vt, p.astype(BF16), preferred_element_type=F32)
            acc_scr[:, c0:c0 + ATT_CW] = alpha * acc_scr[:, c0:c0 + ATT_CW] + pv
        m_scr[...] = jnp.concatenate(m_out, axis=1)
        l_scr[...] = jnp.concatenate(l_out, axis=1)

    scores(other_block(0), s0_scr)

    def pair(n2, carry):
        n = 2 * n2
        scores(other_block(n + 1), s1_scr)
        softmax_pv(other_block(n), s0_scr, False)
        scores(other_block(n + 2), s0_scr)
        softmax_pv(other_block(n + 1), s1_scr, False)
        return carry

    lax.fori_loop(0, n_other // 2, pair, 0)
    if n_other % 2:
        scores(kd, s1_scr)
        softmax_pv(other_block(n_other - 1), s0_scr, False)
        softmax_pv(kd, s1_scr, True)
    else:
        softmax_pv(kd, s0_scr, True)

    lam = lam_ref[...]
    lam_full = (jnp.exp(jnp.sum(lam[0:1] * lam[1:2], axis=-1, keepdims=True))
                - jnp.exp(jnp.sum(lam[2:3] * lam[3:4], axis=-1, keepdims=True)) + lam_init)
    ot = acc_scr[...] / l_scr[...]
    ot = ot[:, :tq] - lam_full * ot[:, tq:]
    ms = jnp.mean(ot * ot, axis=0, keepdims=True)
    ot = ot * lax.rsqrt(ms + LN_EPS) * g_ref[...] * (1.0 - lam_init)
    o_ref[...] = jnp.transpose(ot).astype(o_ref.dtype)


def diff_attention(qkv, lam, subln_g, *, lam_init, tq=ATT_TQ, tk=ATT_TK):
    b, l, _ = qkv.shape
    assert l % tk == 0 and l // tk >= 2 and tk % tq == 0 and l <= 128 * 128
    hd2 = 2 * HEAD_DIM
    return pl.pallas_call(
        functools.partial(_attn_body, tq=tq, tk=tk, lam_init=lam_init),
        grid=(b, N_HEADS, l // tq),
        in_specs=[pl.BlockSpec((4, HEAD_DIM), lambda bi, h, qi: (0, 0)),
                  pl.BlockSpec((hd2, 1), lambda bi, h, qi: (0, 0)),
                  pl.BlockSpec((None, tq, hd2), lambda bi, h, qi: (bi, qi, h)),
                  pl.BlockSpec((None, l, hd2), lambda bi, h, qi: (bi, 0, N_HEADS + h)),
                  pl.BlockSpec((None, l, hd2), lambda bi, h, qi: (bi, 0, 2 * N_HEADS + h))],
        out_specs=pl.BlockSpec((None, tq, hd2), lambda bi, h, qi: (bi, qi, h)),
        out_shape=jax.ShapeDtypeStruct((b, l, D_MODEL), BF16),
        scratch_shapes=[pltpu.VMEM((l, 2 * hd2), BF16), pltpu.VMEM((hd2, l), BF16), pltpu.VMEM((tk, tq), F32),
                        pltpu.VMEM((2, 2 * tq, 2 * hd2), BF16),
                        pltpu.VMEM((tk, 2 * tq), F32), pltpu.VMEM((tk, 2 * tq), F32),
                        pltpu.VMEM((1, 2 * tq), F32), pltpu.VMEM((1, 2 * tq), F32),
                        pltpu.VMEM((hd2, 2 * tq), F32)],
        compiler_params=_cp(("parallel", "parallel", "arbitrary"), 56),
        name="diff_attention",
    )(lam, subln_g, qkv, qkv, qkv)


def _route(x1, wr, br, comb_ref):
    lt = lax.dot_general(wr, x1, NT_DIMS, precision=HIGHEST, preferred_element_type=F32) + br
    gl = [lt[g:g + 1] for g in range(N_GROUPS)]
    gmax = jnp.maximum(jnp.maximum(gl[0], gl[1]), jnp.maximum(gl[2], gl[3]))
    gidx = jnp.where(gl[0] == gmax, 0, jnp.where(gl[1] == gmax, 1, jnp.where(gl[2] == gmax, 2, 3)))
    gw = 1.0 / (jnp.exp(gl[0] - gmax) + jnp.exp(gl[1] - gmax) + jnp.exp(gl[2] - gmax) + jnp.exp(gl[3] - gmax))
    el = []
    for j in range(EXPERTS_PER_GROUP):
        acc = jnp.zeros_like(gmax)
        for g in range(N_GROUPS):
            r = N_GROUPS + g * EXPERTS_PER_GROUP + j
            acc = jnp.where(gidx == g, lt[r:r + 1], acc)
        el.append(acc)
    v1 = jnp.maximum(jnp.maximum(el[0], el[1]), jnp.maximum(el[2], el[3]))
    i1 = jnp.where(el[0] == v1, 0, jnp.where(el[1] == v1, 1, jnp.where(el[2] == v1, 2, 3)))
    neg = jnp.full_like(v1, -jnp.inf)
    el2 = [jnp.where(i1 == j, neg, el[j]) for j in range(EXPERTS_PER_GROUP)]
    v2 = jnp.maximum(jnp.maximum(el2[0], el2[1]), jnp.maximum(el2[2], el2[3]))
    i2 = jnp.where(el2[0] == v2, 0, jnp.where(el2[1] == v2, 1, jnp.where(el2[2] == v2, 2, 3)))
    e21 = jnp.exp(v2 - v1)
    w1 = gw / (1.0 + e21)
    w2 = gw * e21 / (1.0 + e21)
    zero = jnp.zeros_like(v1)
    for g in range(N_GROUPS):
        for j in range(EXPERTS_PER_GROUP):
            wj = jnp.where(i1 == j, w1, jnp.where(i2 == j, w2, zero))
            r = g * EXPERTS_PER_GROUP + j
            comb_ref[r:r + 1, :] = jnp.where(gidx == g, wj, zero)


def _proj_ln_route_body(a_ref, w_ref, b_ref, x_ref, g_ref, beta_ref, wr_ref, br_ref, x1_ref, comb_ref):
    h = jnp.dot(a_ref[...].astype(BF16), w_ref[...], preferred_element_type=F32) + b_ref[...]
    x1 = _layer_norm(ALPHA * x_ref[...] + h, g_ref[...], beta_ref[...])
    x1_ref[...] = x1
    _route(x1, wr_ref[...], br_ref[...], comb_ref)


def proj_ln_route(a, w, b, x, ln_g, ln_b, wr, br, *, tm=512):
    t, k = a.shape
    d = D_MODEL
    const = lambda i: (0, 0)
    return pl.pallas_call(
        _proj_ln_route_body,
        grid=(t // tm,),
        in_specs=[pl.BlockSpec((tm, k), lambda i: (i, 0)),
                  pl.BlockSpec((k, d), const), pl.BlockSpec((1, d), const),
                  pl.BlockSpec((tm, d), lambda i: (i, 0)),
                  pl.BlockSpec((1, d), const), pl.BlockSpec((1, d), const),
                  pl.BlockSpec((ROUTE_ROWS, d), const), pl.BlockSpec((ROUTE_ROWS, 1), const)],
        out_specs=[pl.BlockSpec((tm, d), lambda i: (i, 0)),
                   pl.BlockSpec((N_EXPERTS, tm), lambda i: (0, i))],
        out_shape=[jax.ShapeDtypeStruct((t, d), F32), jax.ShapeDtypeStruct((N_EXPERTS, t), F32)],
        compiler_params=_cp(("parallel",), 48),
        name="proj_ln_route",
    )(a, w, b, x, ln_g, ln_b, wr, br)


def _moe_body(x1_ref, comb_ref, uw_ref, lw_ref, wg_ref, wu_ref, wd_ref, g2_ref, b2_ref,
              p_ref, up_ref, gate_ref, o_ref, xb_scr, y_scr, vrow_scr, vcol_scr, *, tm, cap):
    e = pl.program_id(1)

    @pl.when(e == 0)
    def _():
        xb_scr[...] = x1_ref[...].astype(BF16)
        y_scr[...] = jnp.zeros(y_scr.shape, F32)
        member = jnp.where(comb_ref[...] > 0.0, 1.0, 0.0).astype(BF16)
        vrow_scr[...] = jnp.dot(member, uw_ref[...], preferred_element_type=F32)
        vcol_scr[...] = lax.dot_general(lw_ref[...], member, NT_DIMS, preferred_element_type=F32)

    vrow = vrow_scr[pl.ds(e, 1), :]
    sel = lax.broadcasted_iota(jnp.int32, (tm, N_EXPERTS), 1) == e
    vcol = jnp.sum(jnp.where(sel, vcol_scr[...], 0.0), axis=1, keepdims=True)
    w = comb_ref[pl.ds(e, 1), :]
    w_hi = w.astype(BF16)
    r1 = w - w_hi.astype(F32)
    w_mid = r1.astype(BF16)
    w_lo = (r1 - w_mid.astype(F32)).astype(BF16)
    prow = lax.broadcasted_iota(jnp.int32, (8, tm), 0)
    w3 = jnp.where(prow == 0, w_hi.astype(F32),
                   jnp.where(prow == 1, w_mid.astype(F32),
                             jnp.where(prow == 2, w_lo.astype(F32), 0.0))).astype(BF16)
    count = (jnp.max(vrow) + 1.0) * 0.5
    n_chunk = (count.astype(jnp.int32) + (cap - 1)) // cap

    def chunk(c, carry):
        base = (c * cap).astype(F32)
        tgt_r = 2.0 * (base + lax.broadcasted_iota(jnp.int32, (cap, 1), 0).astype(F32)) + 1.0
        tgt_c = 2.0 * (base + lax.broadcasted_iota(jnp.int32, (1, cap), 1).astype(F32)) + 1.0
        gather = jnp.where(vrow == tgt_r, 1.0, 0.0).astype(BF16)
        scatter = jnp.where(vcol == tgt_c, 1.0, 0.0).astype(BF16)
        xg = jnp.dot(gather, xb_scr[...], preferred_element_type=F32).astype(BF16)
        ws = lax.dot_general(gather, w3, NT_DIMS, preferred_element_type=F32)
        wslot = ws[:, 0:1] + ws[:, 1:2] + ws[:, 2:3]
        hg = jnp.dot(xg, wg_ref[...], preferred_element_type=F32)
        hu = jnp.dot(xg, wu_ref[...], preferred_element_type=F32)
        hid = hg * _sigmoid(hg) * hu * wslot
        o = jnp.dot(hid.astype(BF16), wd_ref[...], preferred_element_type=F32)
        y_scr[...] += jnp.dot(scatter, o.astype(BF16), preferred_element_type=F32)
        return carry

    lax.fori_loop(0, n_chunk, chunk, 0)

    @pl.when(e == N_EXPERTS - 1)
    def _():
        x2 = _layer_norm(ALPHA * x1_ref[...] + y_scr[...], g2_ref[...], b2_ref[...])
        up = jnp.dot(p_ref[...].astype(BF16), up_ref[...], preferred_element_type=F32)
        gt = jnp.dot(x2.astype(BF16), gate_ref[...], preferred_element_type=F32)
        o_ref[...] = x2 + up * _sigmoid(gt)


def moe_ple(x1, comb, uw, lw, wg, wu, wd, ln_g, ln_b, p, up, gate, *, tm=MOE_TM, cap=MOE_CAP):
    t, d = x1.shape
    ff = EXPERT_FF
    const = lambda i, e: (0, 0)
    return pl.pallas_call(
        functools.partial(_moe_body, tm=tm, cap=cap),
        grid=(t // tm, N_EXPERTS),
        in_specs=[pl.BlockSpec((tm, d), lambda i, e: (i, 0)),
                  pl.BlockSpec((N_EXPERTS, tm), lambda i, e: (0, i)),
                  pl.BlockSpec((tm, tm), const), pl.BlockSpec((tm, tm), const),
                  pl.BlockSpec((None, d, ff), lambda i, e: (e, 0, 0)),
                  pl.BlockSpec((None, d, ff), lambda i, e: (e, 0, 0)),
                  pl.BlockSpec((None, ff, d), lambda i, e: (e, 0, 0)),
                  pl.BlockSpec((1, d), const), pl.BlockSpec((1, d), const),
                  pl.BlockSpec((tm, PLE_DIM), lambda i, e: (i, 0)),
                  pl.BlockSpec((PLE_DIM, d), const), pl.BlockSpec((d, d), const)],
        out_specs=pl.BlockSpec((tm, d), lambda i, e: (i, 0)),
        out_shape=jax.ShapeDtypeStruct((t, d), F32),
        scratch_shapes=[pltpu.VMEM((tm, d), BF16), pltpu.VMEM((tm, d), F32),
                        pltpu.VMEM((N_EXPERTS, tm), F32), pltpu.VMEM((tm, N_EXPERTS), F32)],
        compiler_params=_cp(("parallel", "arbitrary"), 56),
        name="moe_ple",
    )(x1, comb, uw, lw, wg, wu, wd, ln_g, ln_b, p, up, gate)


def _dwconv_body(prev_ref, cur_ref, next_ref, w_ref, b_ref, g_ref, beta_ref, o_ref, buf_scr, *, width, tm, norm_act):
    i = pl.program_id(1)
    n = pl.num_programs(1)
    pad = width // 2
    halo = jnp.zeros(prev_ref.shape, F32)
    buf_scr[0:HALO, :] = jnp.where(i > 0, prev_ref[...], halo)
    buf_scr[HALO:HALO + tm, :] = cur_ref[...]
    buf_scr[HALO + tm:2 * HALO + tm, :] = jnp.where(i < n - 1, next_ref[...], halo)

    def rows(r, carry):
        r0 = pl.multiple_of(r * CONV_ROWS, CONV_ROWS)
        win = buf_scr[pl.ds(r0, CONV_ROWS + 2 * HALO), :]
        acc = jnp.zeros((CONV_ROWS, win.shape[1]), F32) + b_ref[...]
        for k in range(width):
            off = HALO - pad + k
            acc = acc + win[off:off + CONV_ROWS, :] * w_ref[k:k + 1, :]
        if norm_act:
            acc = _layer_norm(acc, g_ref[...], beta_ref[...])
            acc = acc * _sigmoid(acc)
        o_ref[pl.ds(r0, CONV_ROWS), :] = acc.astype(o_ref.dtype)
        return carry

    lax.fori_loop(0, tm // CONV_ROWS, rows, 0)


def dwconv(x, w, b, ln_g, ln_b, *, norm_act, out_dtype, tm=256):
    bsz, l, c = x.shape
    width = w.shape[0]
    tc = D_MODEL
    nb = tm // HALO
    last = l // HALO - 1
    return pl.pallas_call(
        functools.partial(_dwconv_body, width=width, tm=tm, norm_act=norm_act),
        grid=(bsz, l // tm, c // tc),
        in_specs=[pl.BlockSpec((None, HALO, tc), lambda bi, i, j: (bi, jnp.maximum(i * nb - 1, 0), j)),
                  pl.BlockSpec((None, tm, tc), lambda bi, i, j: (bi, i, j)),
                  pl.BlockSpec((None, HALO, tc), lambda bi, i, j: (bi, jnp.minimum((i + 1) * nb, last), j)),
                  pl.BlockSpec((width, tc), lambda bi, i, j: (0, j)),
                  pl.BlockSpec((1, tc), lambda bi, i, j: (0, j)),
                  pl.BlockSpec((1, tc), lambda bi, i, j: (0, 0)),
                  pl.BlockSpec((1, tc), lambda bi, i, j: (0, 0))],
        out_specs=pl.BlockSpec((None, None, tm, tc), lambda bi, i, j: (j, bi, i, 0)),
        out_shape=jax.ShapeDtypeStruct((c // tc, bsz, l, tc), out_dtype),
        scratch_shapes=[pltpu.VMEM((tm + 2 * HALO, tc), F32)],
        compiler_params=_cp(("parallel", "parallel", "parallel"), 32),
        name="dwconv%d" % width,
    )(x, x, x, w, b, ln_g, ln_b)


def _filter_body(w1t_ref, w1c_ref, w1s_ref, b1_ref, w2_ref, b2_ref, w3_ref, b3_ref, freq_ref, wout_ref,
                 o_ref, *, tl, length):
    i = pl.program_id(0)
    pos = (i * tl + lax.broadcasted_iota(jnp.int32, (tl, 1), 0)).astype(F32)
    t = pos / float(length - 1)
    ang = (2.0 * math.pi / length) * pos
    band = lax.broadcasted_iota(jnp.int32, (1, POS_BANDS), 1).astype(F32)
    bands = 1e-4 + band * ((POS_BANDS - 1 - 1e-4) / (POS_BANDS - 1))
    fw = bands * ang
    freq = freq_ref[...]
    dot = functools.partial(jnp.dot, precision=HIGHEST, preferred_element_type=F32)
    pre = t * w1t_ref[...] + dot(jnp.cos(fw), w1c_ref[...]) - dot(jnp.sin(fw), w1s_ref[...]) + b1_ref[...]
    hcur = jnp.sin(freq * pre)
    hcur = jnp.sin(freq * (dot(hcur, w2_ref[...]) + b2_ref[...]))
    hcur = jnp.sin(freq * (dot(hcur, w3_ref[...]) + b3_ref[...]))
    ch = lax.broadcasted_iota(jnp.int32, (1, D_MODEL), 1).astype(F32)
    deltas = jnp.abs(MIN_DECAY + ch * ((MAX_DECAY - MIN_DECAY) / (D_MODEL - 1)))
    decay = jnp.exp(-t * deltas)
    for j in range(2 * HYENA_ORDER):
        o_ref[j] = dot(hcur, wout_ref[:, j * D_MODEL:(j + 1) * D_MODEL]) * decay


def hyena_filters(length, w1, b1, w2, b2, w3, b3, freq, wout, *, tl=256):
    fwid = FILTER_WIDTH
    const = lambda i: (0, 0)
    nf = 2 * HYENA_ORDER
    return pl.pallas_call(
        functools.partial(_filter_body, tl=tl, length=length),
        grid=(length // tl,),
        in_specs=[pl.BlockSpec((1, fwid), const), pl.BlockSpec((POS_BANDS, fwid), const),
                  pl.BlockSpec((POS_BANDS, fwid), const), pl.BlockSpec((1, fwid), const),
                  pl.BlockSpec((fwid, fwid), const), pl.BlockSpec((1, fwid), const),
                  pl.BlockSpec((fwid, fwid), const), pl.BlockSpec((1, fwid), const),
                  pl.BlockSpec((1, fwid), const), pl.BlockSpec((fwid, nf * D_MODEL), const)],
        out_specs=pl.BlockSpec((nf, tl, D_MODEL), lambda i: (0, i, 0)),
        out_shape=jax.ShapeDtypeStruct((nf, length, D_MODEL), F32),
        compiler_params=_cp(("parallel",), 32),
        name="hyena_filters",
    )(w1[0:1], w1[1:1 + POS_BANDS], w1[1 + POS_BANDS:], b1, w2, b2, w3, b3, freq, wout)


def _dft_tables(n1, n2s):
    n = n1 * n2s
    m1 = (n1 // 2 + 1 + 3) // 4 * 4
    k1 = jnp.arange(m1, dtype=jnp.int32)
    valid = (k1 <= n1 // 2)
    a = jnp.arange(n1 // 2, dtype=jnp.int32)
    ang_a = (2.0 * math.pi / n1) * ((k1[:, None] * a[None, :]) % n1).astype(F32)
    ca = jnp.where(valid[:, None], jnp.cos(ang_a), 0.0)
    sa = jnp.where(valid[:, None], jnp.sin(ang_a), 0.0)
    g_a = jnp.stack([ca, -sa], axis=1).reshape(2 * m1, n1 // 2)
    weight = jnp.where(valid, jnp.where((k1 == 0) | (k1 == n1 // 2), 1.0, 2.0), 0.0)
    g_ai = jnp.stack([ca.T, -sa.T], axis=2).reshape(n1 // 2, 2 * m1)
    b = jnp.arange(n2s, dtype=jnp.int32)
    k2 = jnp.arange(n2s, dtype=jnp.int32)
    idx = (b[None, None, :] * (k2[None, :, None] * n1 + k1[:, None, None])) % n
    ang_c = (2.0 * math.pi / n) * idx.astype(F32)
    cc, sc = jnp.cos(ang_c), jnp.sin(ang_c)
    g_c = jnp.concatenate([jnp.concatenate([cc, sc], axis=2),
                           jnp.concatenate([-sc, cc], axis=2)], axis=1)
    cct = jnp.swapaxes(cc, 1, 2) * (weight / n)[:, None, None]
    sct = jnp.swapaxes(sc, 1, 2) * (weight / n)[:, None, None]
    g_ci = jnp.concatenate([jnp.concatenate([cct, -sct], axis=2),
                            jnp.concatenate([sct, cct], axis=2)], axis=1)
    return m1, g_a, g_c, g_ci, g_ai


def _bmm_body(g_ref, x_ref, o_ref, *, bt):
    for bi in range(bt):
        o_ref[bi] = jnp.dot(g_ref[...], x_ref[bi], precision=HIGHEST, preferred_element_type=F32)


def _bmm_spec_body(g1_ref, g2_ref, x_ref, hf_ref, hb_ref, o_ref, *, bt):
    half = g1_ref.shape[0] // 2
    hr = hf_ref[:half] + hb_ref[:half]
    hi = hf_ref[half:] - hb_ref[half:]
    for bi in range(bt):
        xs = jnp.dot(g1_ref[...], x_ref[bi], precision=HIGHEST, preferred_element_type=F32)
        xr, xi = xs[:half], xs[half:]
        ys = jnp.concatenate([xr * hr - xi * hi, xr * hi + xi * hr], axis=0)
        o_ref[bi] = jnp.dot(g2_ref[...], ys, precision=HIGHEST, preferred_element_type=F32)


def _bmm_gate_body(g_ref, x_ref, v_ref, gate_ref, bias_ref, o_ref, *, bt):
    for bi in range(bt):
        y = jnp.dot(g_ref[...], x_ref[bi], precision=HIGHEST, preferred_element_type=F32)
        o_ref[bi] = gate_ref[bi] * (y + v_ref[bi] * bias_ref[...])


def _bmm(g, x, xsel, *, bt, tc, name):
    _, bsz, s, k, c = x.shape
    sg, r, _ = g.shape
    gmap = (lambda si, bi, ci: (si, 0, 0)) if sg > 1 else (lambda si, bi, ci: (0, 0, 0))
    return pl.pallas_call(
        functools.partial(_bmm_body, bt=bt),
        grid=(s, bsz // bt, c // tc),
        in_specs=[pl.BlockSpec((None, r, k), gmap),
                  pl.BlockSpec((None, bt, None, k, tc), lambda si, bi, ci: (xsel, bi, si, 0, ci))],
        out_specs=pl.BlockSpec((bt, None, r, tc), lambda si, bi, ci: (bi, si, 0, ci)),
        out_shape=jax.ShapeDtypeStruct((bsz, s, r, c), F32),
        compiler_params=_cp(("parallel", "parallel", "parallel"), 48),
        name=name,
    )(g, x)


def _bmm_spec(g1, g2, x, hspec, order, *, bt, name):
    bsz, s, r, c = x.shape
    slab = lambda si, bi: (si, 0, 0)
    return pl.pallas_call(
        functools.partial(_bmm_spec_body, bt=bt),
        grid=(s, bsz // bt),
        in_specs=[pl.BlockSpec((None, r, r), slab), pl.BlockSpec((None, r, r), slab),
                  pl.BlockSpec((bt, None, r, c), lambda si, bi: (bi, si, 0, 0)),
                  pl.BlockSpec((None, None, r, c), lambda si, bi: (2 * order, si, 0, 0)),
                  pl.BlockSpec((None, None, r, c), lambda si, bi: (2 * order + 1, si, 0, 0))],
        out_specs=pl.BlockSpec((bt, None, r, c), lambda si, bi: (bi, si, 0, 0)),
        out_shape=jax.ShapeDtypeStruct((bsz, s, r, c), F32),
        compiler_params=_cp(("parallel", "parallel"), 48),
        name=name,
    )(g1, g2, x, hspec, hspec)


def _bmm_gate(g, x, v, vsel, gates, gsel, bias, *, bt, tc, name):
    bsz, _, k, c = x.shape
    r = g.shape[0]
    return pl.pallas_call(
        functools.partial(_bmm_gate_body, bt=bt),
        grid=(bsz // bt, c // tc),
        in_specs=[pl.BlockSpec((r, k), lambda bi, ci: (0, 0)),
                  pl.BlockSpec((bt, None, k, tc), lambda bi, ci: (bi, 0, 0, ci)),
                  pl.BlockSpec((None, bt, r, tc), lambda bi, ci: (vsel, bi, 0, ci)),
                  pl.BlockSpec((None, bt, r, tc), lambda bi, ci: (gsel, bi, 0, ci)),
                  pl.BlockSpec((1, tc), lambda bi, ci: (0, ci))],
        out_specs=pl.BlockSpec((bt, r, tc), lambda bi, ci: (bi, 0, ci)),
        out_shape=jax.ShapeDtypeStruct((bsz, r, c), F32),
        compiler_params=_cp(("parallel", "parallel"), 48),
        name=name,
    )(g, x, v, gates, bias)


def _fft_plan(length):
    n = 2 * length
    n1 = 256 if n >= 32768 else 64
    return n1, n // n1


def hyena_long_convs(xs, filt, f_bias):
    _, bsz, length, d = xs.shape
    n1, n2s = _fft_plan(length)
    m1, g_a, g_c, g_ci, g_ai = _dft_tables(n1, n2s)
    half = n1 // 2
    c = n2s * d
    bt = 1 if length >= 8192 else min(8, bsz)
    tc = 4096

    def forward_a(x5, sel, nb, name):
        return _bmm(g_a[None], x5, sel, bt=min(bt, nb), tc=tc, name=name).reshape(nb, m1, 2 * n2s, d)

    fa = forward_a(filt.reshape(1, 4, 1, half, c), 0, 4, "filt_dft_a")
    hspec = _bmm(g_c, fa[None], 0, bt=1, tc=d, name="filt_dft_c")
    bias_t = jnp.tile(f_bias, (1, n2s))
    xs5 = xs.reshape(3, bsz, 1, half, c)
    xs4 = xs.reshape(3, bsz, half, c)
    z5, zsel = xs5, 2
    z4 = xs4
    out = None
    for order in range(HYENA_ORDER):
        sa = forward_a(z5, zsel, bsz, "conv_dft_a")
        sb = _bmm_spec(g_c, g_ci, sa, hspec, order, bt=bt, name="conv_spec")
        out = _bmm_gate(g_ai, sb.reshape(bsz, 1, 2 * m1, c), z4, zsel, xs4, order,
                        bias_t[order:order + 1], bt=bt, tc=tc, name="conv_idft_gate")
        z5, zsel = out.reshape(1, bsz, 1, half, c), 0
        z4 = out[None]
    return out.reshape(bsz, length, d)


def _prefix_tables(tm):
    r = jnp.arange(tm, dtype=jnp.int32)
    upper = jnp.where(r[:, None] < r[None, :], 2.0, jnp.where(r[:, None] == r[None, :], 1.0, 0.0))
    return upper.astype(BF16), upper.T.astype(BF16)


def _trunk(x, p, wts):
    bsz, length, d = x.shape
    t = bsz * length
    xf = x.reshape(t, d)
    zeros_d = jnp.zeros((1, d), F32)
    ones_d = jnp.ones((1, d), F32)
    uw, lw = _prefix_tables(MOE_TM)
    for i in range(DEPTH):
        j, kind = i // N_MIXERS, i % N_MIXERS
        if kind == 0:
            lam_init = 0.8 - 0.6 * math.exp(-0.3 * i)
            qkv = linear(xf, wts["attn_w_qkv"][j], jnp.zeros((1, 3 * d), F32), out_dtype=BF16, name="attn_qkv")
            a = diff_attention(qkv.reshape(bsz, length, 3 * d), wts["attn_lam"][j], wts["attn_subln_g"][j],
                               lam_init=lam_init).reshape(t, d)
            w_o, b_o = wts["attn_w_o"][j], zeros_d
        elif kind == 1:
            hglu = linear(xf, wts["conv_w_pw1"][j], wts["conv_b_pw1"][j], out_dtype=F32, glu=True, name="conv_pw1")
            a = dwconv(hglu.reshape(bsz, length, d), wts["conv_w_dw"][j], wts["conv_b_dw"][j],
                       wts["conv_ln_g"][j], wts["conv_ln_b"][j], norm_act=True, out_dtype=BF16).reshape(t, d)
            w_o, b_o = wts["conv_w_pw2"][j], wts["conv_b_pw2"][j]
        else:
            u = linear(xf, wts["hy_w_in"][j], wts["hy_b_in"][j], out_dtype=F32, name="hyena_in")
            xs = dwconv(u.reshape(bsz, length, 3 * d), wts["hy_w_short"][j], wts["hy_b_short"][j],
                        ones_d, zeros_d, norm_act=False, out_dtype=F32)
            filt = hyena_filters(length, wts["hy_f_w1"][j], wts["hy_f_b1"][j], wts["hy_f_w2"][j], wts["hy_f_b2"][j],
                                 wts["hy_f_w3"][j], wts["hy_f_b3"][j], wts["hy_f_freq"][j], wts["hy_f_wout"][j])
            a = hyena_long_convs(xs, filt, wts["hy_f_bias"][j]).reshape(t, d)
            w_o, b_o = wts["hy_w_out"][j], wts["hy_b_out"][j]
        x1, comb = proj_ln_route(a, w_o, b_o, xf, wts["ln1_g"][i], wts["ln1_b"][i],
                                 wts["route_w"][i], wts["route_b"][i])
        xf = moe_ple(x1, comb, uw, lw, wts["moe_w_gate"][i], wts["moe_w_up"][i], wts["moe_w_down"][i],
                     wts["ln2_g"][i], wts["ln2_b"][i], p[i].reshape(t, PLE_DIM),
                     wts["ple_w_up"][i], wts["ple_w_gate"][i])
    return xf.reshape(bsz, length, d)


def kernel(x_prompt, x_sample, p_prompt, p_sample, attn_w_qkv, attn_w_o, attn_lam_q1, attn_lam_k1, attn_lam_q2, attn_lam_k2, attn_subln_g, conv_w_pw1, conv_b_pw1, conv_w_dw, conv_b_dw, conv_ln_g, conv_ln_b, conv_w_pw2, conv_b_pw2, hy_w_in, hy_b_in, hy_w_short, hy_b_short, hy_f_w1, hy_f_b1, hy_f_w2, hy_f_b2, hy_f_w3, hy_f_b3, hy_f_freq, hy_f_wout, hy_f_bias, hy_w_out, hy_b_out, ln1_g, ln1_b, ln2_g, ln2_b, moe_w_group, moe_b_group, moe_w_expert, moe_b_expert, moe_w_gate, moe_w_up, moe_w_down, ple_w_up, ple_w_gate):
    d = D_MODEL
    row = lambda a: a[:, None, :]
    q_scale = jnp.concatenate([jnp.full((d,), HEAD_DIM ** -0.5 * LOG2E, F32), jnp.ones((2 * d,), F32)])
    route_w = jnp.concatenate([jnp.swapaxes(moe_w_group, 1, 2), jnp.swapaxes(moe_w_expert, 1, 2),
                               jnp.zeros((DEPTH, ROUTE_ROWS - N_GROUPS - N_EXPERTS, d), F32)], axis=1)
    route_b = jnp.concatenate([moe_b_group, moe_b_expert,
                               jnp.zeros((DEPTH, ROUTE_ROWS - N_GROUPS - N_EXPERTS), F32)], axis=1)[:, :, None]
    wts = {
        "attn_w_qkv": (attn_w_qkv * q_scale).astype(BF16),
        "attn_w_o": attn_w_o.astype(BF16),
        "attn_lam": jnp.stack([attn_lam_q1, attn_lam_k1, attn_lam_q2, attn_lam_k2], axis=1),
        "attn_subln_g": attn_subln_g[:, :, None],
        "conv_w_pw1": conv_w_pw1.astype(BF16), "conv_b_pw1": row(conv_b_pw1),
        "conv_w_dw": conv_w_dw, "conv_b_dw": row(conv_b_dw),
        "conv_ln_g": row(conv_ln_g), "conv_ln_b": row(conv_ln_b),
        "conv_w_pw2": conv_w_pw2.astype(BF16), "conv_b_pw2": row(conv_b_pw2),
        "hy_w_in": hy_w_in.astype(BF16), "hy_b_in": row(hy_b_in),
        "hy_w_short": hy_w_short, "hy_b_short": row(hy_b_short),
        "hy_f_w1": hy_f_w1, "hy_f_b1": row(hy_f_b1), "hy_f_w2": hy_f_w2, "hy_f_b2": row(hy_f_b2),
        "hy_f_w3": hy_f_w3, "hy_f_b3": row(hy_f_b3), "hy_f_freq": row(hy_f_freq), "hy_f_wout": hy_f_wout,
        "hy_f_bias": hy_f_bias,
        "hy_w_out": hy_w_out.astype(BF16), "hy_b_out": row(hy_b_out),
        "ln1_g": row(ln1_g), "ln1_b": row(ln1_b), "ln2_g": row(ln2_g), "ln2_b": row(ln2_b),
        "route_w": route_w, "route_b": route_b,
        "moe_w_gate": moe_w_gate.astype(BF16), "moe_w_up": moe_w_up.astype(BF16),
        "moe_w_down": moe_w_down.astype(BF16),
        "ple_w_up": ple_w_up.astype(BF16), "ple_w_gate": ple_w_gate.astype(BF16),
    }
    y_prompt = _trunk(x_prompt, p_prompt, wts)
    y_sample = _trunk(x_sample, p_sample, wts)
    return (y_prompt, y_sample)
```

```python
import functools
import math

import jax
import jax.numpy as jnp
from jax import lax
from jax.experimental import pallas as pl
from jax.experimental.pallas import tpu as pltpu

F32 = jnp.float32
BF16 = jnp.bfloat16
HIGHEST = lax.Precision.HIGHEST

D_MODEL = 1024
DEPTH = 4
N_MIXERS = 3
N_HEADS = 8
HEAD_DIM = 64
CONV_WIDTH = 31
HYENA_ORDER = 2
SHORT_WIDTH = 3
POS_BANDS = 16
FILTER_WIDTH = 64
MAX_DECAY = math.log(1e-2) / 0.3
MIN_DECAY = math.log(1e-2) / 1.5
N_GROUPS = 4
EXPERTS_PER_GROUP = 4
N_EXPERTS = N_GROUPS * EXPERTS_PER_GROUP
EXPERT_FF = 512
PLE_DIM = 256
ALPHA = (2 * DEPTH) ** 0.25
LN_EPS = 1e-5
LOG2E = 1.4426950408889634

HALO = 16
CONV_ROWS = 16
ROUTE_ROWS = 24
MOE_TM = 1024
MOE_CAP = 192
ATT_TQ = 512
ATT_TK = 1024
ATT_CW = 256
ATT_RCH = 64
ATT_FILL = 512
NT_DIMS = (((1,), (1,)), ((), ()))


def _cp(sem, vmem_mb):
    return pltpu.CompilerParams(dimension_semantics=sem, vmem_limit_bytes=vmem_mb << 20)


def _layer_norm(z, g, b):
    mu = jnp.mean(z, axis=-1, keepdims=True)
    zc = z - mu
    var = jnp.mean(zc * zc, axis=-1, keepdims=True)
    return zc * lax.rsqrt(var + LN_EPS) * g + b


def _sigmoid(x):
    return 1.0 / (1.0 + jnp.exp(-x))


def _linear_body(x_ref, w_ref, b_ref, o_ref, *, nc, glu):
    x = x_ref[...].astype(BF16)
    n_out = o_ref.shape[-1]
    for n0 in range(0, n_out, nc):
        a = jnp.dot(x, w_ref[:, n0:n0 + nc], preferred_element_type=F32) + b_ref[:, n0:n0 + nc]
        if glu:
            g = (jnp.dot(x, w_ref[:, n_out + n0:n_out + n0 + nc], preferred_element_type=F32)
                 + b_ref[:, n_out + n0:n_out + n0 + nc])
            a = a * _sigmoid(g)
        o_ref[:, n0:n0 + nc] = a.astype(o_ref.dtype)


def linear(x, w, b, *, out_dtype, glu=False, tm=512, nc=512, name="linear"):
    t, k = x.shape
    n = w.shape[1]
    n_out = n // 2 if glu else n
    return pl.pallas_call(
        functools.partial(_linear_body, nc=nc, glu=glu),
        grid=(t // tm,),
        in_specs=[pl.BlockSpec((tm, k), lambda i: (i, 0)),
                  pl.BlockSpec((k, n), lambda i: (0, 0)),
                  pl.BlockSpec((1, n), lambda i: (0, 0))],
        out_specs=pl.BlockSpec((tm, n_out), lambda i: (i, 0)),
        out_shape=jax.ShapeDtypeStruct((t, n_out), out_dtype),
        compiler_params=_cp(("parallel",), 48),
        name=name,
    )(x, w, b)


def _split3(x):
    hi = x.astype(BF16).astype(F32)
    mid = (x - hi).astype(BF16).astype(F32)
    lo = (x - hi - mid).astype(BF16).astype(F32)
    return [hi, mid, lo]


def _lane_table(lane, values, first):
    out = jnp.zeros(lane.shape, F32)
    for n, val in enumerate(values):
        out = jnp.where(lane == first + n, val, out)
    return out


def _attn_body(lam_ref, g_ref, q_ref, k_ref, v_ref, o_ref, kp_scr, vt_scr, d0_scr, qq_scr, s0_scr, s1_scr, p_scr,
               m_scr, l_scr, acc_scr, *, tq, tk, lam_init):
    h = pl.program_id(1)
    qi = pl.program_id(2)
    length = k_ref.shape[0]
    nk = length // tk
    n_other = nk - 1
    hd2 = 2 * HEAD_DIM
    slope2 = jnp.exp2(-(jnp.full((1, 1), h, jnp.int32) + 1).astype(F32)) * LOG2E
    c1 = _split3(slope2)
    c128 = [128.0 * c for c in c1]

    @pl.when(qi == 0)
    def _():
        d0_scr[...] = (lax.broadcasted_iota(jnp.int32, (tk, tq), 1)
                       - lax.broadcasted_iota(jnp.int32, (tk, tq), 0)).astype(F32)
        lane = lax.broadcasted_iota(jnp.int32, (ATT_FILL, hd2), 1)
        consts = _lane_table(lane, [-c for c in c128] + [-c for c in c1], 0)
        eye = jnp.where(lax.broadcasted_iota(jnp.int32, (hd2, hd2), 0)
                        == lax.broadcasted_iota(jnp.int32, (hd2, hd2), 1), 1.0, 0.0).astype(BF16)

        def fill(ci, carry):
            r0 = pl.multiple_of(ci * ATT_FILL, ATT_FILL)
            pos = r0 + lax.broadcasted_iota(jnp.int32, (ATT_FILL, hd2), 0)
            hi_digit = jnp.right_shift(pos, 7).astype(F32)
            lo_digit = jnp.bitwise_and(pos, 127).astype(F32)
            aug = jnp.where(lane < 6, consts, jnp.where(lane < 9, hi_digit, jnp.where(lane < 12, lo_digit, 0.0)))
            kp_scr[pl.ds(r0, ATT_FILL), 0:hd2] = k_ref[pl.ds(r0, ATT_FILL), :]
            kp_scr[pl.ds(r0, ATT_FILL), hd2:2 * hd2] = aug.astype(BF16)
            vt_scr[:, pl.ds(r0, ATT_FILL)] = lax.dot_general(
                eye, v_ref[pl.ds(r0, ATT_FILL), :], NT_DIMS, preferred_element_type=F32).astype(BF16)
            return carry

        lax.fori_loop(0, length // ATT_FILL, fill, 0)

    q = q_ref[...]
    lane = lax.broadcasted_iota(jnp.int32, (tq, hd2), 1)
    zero = jnp.zeros_like(q)
    tpos = qi * tq + lax.broadcasted_iota(jnp.int32, (tq, hd2), 0)
    hi_digit = jnp.right_shift(tpos, 7).astype(F32)
    lo_digit = jnp.bitwise_and(tpos, 127).astype(F32)
    consts = _lane_table(lane, c128 + c1, 6)
    augq = jnp.where(lane < 3, hi_digit, jnp.where(lane < 6, lo_digit, consts))
    for var, aug in enumerate((augq.astype(BF16), (-augq).astype(BF16))):
        qq_scr[var, 0:tq, 0:hd2] = jnp.where(lane < HEAD_DIM, q, zero)
        qq_scr[var, tq:2 * tq, 0:hd2] = jnp.where(lane >= HEAD_DIM, q, zero)
        qq_scr[var, 0:tq, hd2:2 * hd2] = aug
        qq_scr[var, tq:2 * tq, hd2:2 * hd2] = aug

    m_scr[...] = jnp.full(m_scr.shape, -jnp.inf, F32)
    l_scr[...] = jnp.zeros(l_scr.shape, F32)
    acc_scr[...] = jnp.zeros(acc_scr.shape, F32)

    kd = (qi * tq) // tk

    def other_block(n):
        return jnp.where(n < n_other, jnp.where(n < kd, n, n + 1), kd)

    def scores(blk, s_scr):
        ks = pl.multiple_of(blk * tk, tk)
        var = (blk > kd).astype(jnp.int32)
        s_scr[...] = lax.dot_general(kp_scr[pl.ds(ks, tk), :], qq_scr[var], NT_DIMS, preferred_element_type=F32)

    def softmax_pv(blk, s_scr, diag):
        ks = pl.multiple_of(blk * tk, tk)
        cdiag = (qi * tq - blk * tk).astype(F32)
        vt = vt_scr[:, pl.ds(ks, tk)]
        m_all = m_scr[...]
        l_all = l_scr[...]
        m_out, l_out = [], []
        n_chunk = tk // ATT_RCH
        for cg in range(2 * tq // ATT_CW):
            cols = slice(cg * ATT_CW, (cg + 1) * ATT_CW)
            dc = (cg * ATT_CW) % tq
            run = None
            for r in range(n_chunk):
                rows = slice(r * ATT_RCH, (r + 1) * ATT_RCH)
                s = s_scr[rows, cols]
                if diag:
                    s = s + jnp.minimum(d0_scr[rows, dc:dc + ATT_CW] + cdiag, 0.0) * (2.0 * slope2)
                    s_scr[rows, cols] = s
                run = s if run is None else jnp.maximum(run, s)
            m_old = m_all[:, cols]
            m_new = jnp.maximum(m_old, jnp.max(run, axis=0, keepdims=True))
            alpha = jnp.exp2(m_old - m_new)
            tot = None
            for r in range(n_chunk):
                rows = slice(r * ATT_RCH, (r + 1) * ATT_RCH)
                p = jnp.exp2(s_scr[rows, cols] - m_new)
                p_scr[rows, cols] = p.astype(BF16)
                tot = p if tot is None else tot + p
            l_out.append(alpha * l_all[:, cols] + jnp.sum(tot, axis=0, keepdims=True))
            m_out.append(m_new)
            pv = jnp.dot(vt, p_scr[:, cols], preferred_element_type=F32)
            acc_scr[:, cols] = alpha * acc_scr[:, cols] + pv
        m_scr[...] = jnp.concatenate(m_out, axis=1)
        l_scr[...] = jnp.concatenate(l_out, axis=1)

    scores(other_block(0), s0_scr)

    def pair(n2, carry):
        n = 2 * n2
        scores(other_block(n + 1), s1_scr)
        softmax_pv(other_block(n), s0_scr, False)
        scores(other_block(n + 2), s0_scr)
        softmax_pv(other_block(n + 1), s1_scr, False)
        return carry

    lax.fori_loop(0, n_other // 2, pair, 0)
    if n_other % 2:
        scores(kd, s1_scr)
        softmax_pv(other_block(n_other - 1), s0_scr, False)
        softmax_pv(kd, s1_scr, True)
    else:
        softmax_pv(kd, s0_scr, True)

    lam = lam_ref[...]
    lam_full = (jnp.exp(jnp.sum(lam[0:1] * lam[1:2], axis=-1, keepdims=True))
                - jnp.exp(jnp.sum(lam[2:3] * lam[3:4], axis=-1, keepdims=True)) + lam_init)
    ot = acc_scr[...] / l_scr[...]
    ot = ot[:, :tq] - lam_full * ot[:, tq:]
    ms = jnp.mean(ot * ot, axis=0, keepdims=True)
    ot = ot * lax.rsqrt(ms + LN_EPS) * g_ref[...] * (1.0 - lam_init)
    o_ref[...] = jnp.transpose(ot).astype(o_ref.dtype)


def diff_attention(qkv, lam, subln_g, *, lam_init, tq=ATT_TQ, tk=ATT_TK):
    b, l, _ = qkv.shape
    assert l % tk == 0 and l // tk >= 2 and tk % tq == 0 and l <= 128 * 128
    hd2 = 2 * HEAD_DIM
    return pl.pallas_call(
        functools.partial(_attn_body, tq=tq, tk=tk, lam_init=lam_init),
        grid=(b, N_HEADS, l // tq),
        in_specs=[pl.BlockSpec((4, HEAD_DIM), lambda bi, h, qi: (0, 0)),
                  pl.BlockSpec((hd2, 1), lambda bi, h, qi: (0, 0)),
                  pl.BlockSpec((None, tq, hd2), lambda bi, h, qi: (bi, qi, h)),
                  pl.BlockSpec((None, l, hd2), lambda bi, h, qi: (bi, 0, N_HEADS + h)),
                  pl.BlockSpec((None, l, hd2), lambda bi, h, qi: (bi, 0, 2 * N_HEADS + h))],
        out_specs=pl.BlockSpec((None, tq, hd2), lambda bi, h, qi: (bi, qi, h)),
        out_shape=jax.ShapeDtypeStruct((b, l, D_MODEL), BF16),
        scratch_shapes=[pltpu.VMEM((l, 2 * hd2), BF16), pltpu.VMEM((hd2, l), BF16), pltpu.VMEM((tk, tq), F32),
                        pltpu.VMEM((2, 2 * tq, 2 * hd2), BF16),
                        pltpu.VMEM((tk, 2 * tq), F32), pltpu.VMEM((tk, 2 * tq), F32),
                        pltpu.VMEM((tk, 2 * tq), BF16),
                        pltpu.VMEM((1, 2 * tq), F32), pltpu.VMEM((1, 2 * tq), F32),
                        pltpu.VMEM((hd2, 2 * tq), F32)],
        compiler_params=_cp(("parallel", "parallel", "arbitrary"), 56),
        name="diff_attention",
    )(lam, subln_g, qkv, qkv, qkv)


def _route(x1, wr, br, comb_ref):
    lt = lax.dot_general(wr, x1, NT_DIMS, precision=HIGHEST, preferred_element_type=F32) + br
    gl = [lt[g:g + 1] for g in range(N_GROUPS)]
    gmax = jnp.maximum(jnp.maximum(gl[0], gl[1]), jnp.maximum(gl[2], gl[3]))
    gidx = jnp.where(gl[0] == gmax, 0, jnp.where(gl[1] == gmax, 1, jnp.where(gl[2] == gmax, 2, 3)))
    gw = 1.0 / (jnp.exp(gl[0] - gmax) + jnp.exp(gl[1] - gmax) + jnp.exp(gl[2] - gmax) + jnp.exp(gl[3] - gmax))
    el = []
    for j in range(EXPERTS_PER_GROUP):
        acc = jnp.zeros_like(gmax)
        for g in range(N_GROUPS):
            r = N_GROUPS + g * EXPERTS_PER_GROUP + j
            acc = jnp.where(gidx == g, lt[r:r + 1], acc)
        el.append(acc)
    v1 = jnp.maximum(jnp.maximum(el[0], el[1]), jnp.maximum(el[2], el[3]))
    i1 = jnp.where(el[0] == v1, 0, jnp.where(el[1] == v1, 1, jnp.where(el[2] == v1, 2, 3)))
    neg = jnp.full_like(v1, -jnp.inf)
    el2 = [jnp.where(i1 == j, neg, el[j]) for j in range(EXPERTS_PER_GROUP)]
    v2 = jnp.maximum(jnp.maximum(el2[0], el2[1]), jnp.maximum(el2[2], el2[3]))
    i2 = jnp.where(el2[0] == v2, 0, jnp.where(el2[1] == v2, 1, jnp.where(el2[2] == v2, 2, 3)))
    e21 = jnp.exp(v2 - v1)
    w1 = gw / (1.0 + e21)
    w2 = gw * e21 / (1.0 + e21)
    zero = jnp.zeros_like(v1)
    for g in range(N_GROUPS):
        for j in range(EXPERTS_PER_GROUP):
            wj = jnp.where(i1 == j, w1, jnp.where(i2 == j, w2, zero))
            r = g * EXPERTS_PER_GROUP + j
            comb_ref[r:r + 1, :] = jnp.where(gidx == g, wj, zero)


def _proj_ln_route_body(a_ref, w_ref, b_ref, x_ref, g_ref, beta_ref, wr_ref, br_ref, x1_ref, comb_ref):
    h = jnp.dot(a_ref[...].astype(BF16), w_ref[...], preferred_element_type=F32) + b_ref[...]
    x1 = _layer_norm(ALPHA * x_ref[...] + h, g_ref[...], beta_ref[...])
    x1_ref[...] = x1
    _route(x1, wr_ref[...], br_ref[...], comb_ref)


def proj_ln_route(a, w, b, x, ln_g, ln_b, wr, br, *, tm=512):
    t, k = a.shape
    d = D_MODEL
    const = lambda i: (0, 0)
    return pl.pallas_call(
        _proj_ln_route_body,
        grid=(t // tm,),
        in_specs=[pl.BlockSpec((tm, k), lambda i: (i, 0)),
                  pl.BlockSpec((k, d), const), pl.BlockSpec((1, d), const),
                  pl.BlockSpec((tm, d), lambda i: (i, 0)),
                  pl.BlockSpec((1, d), const), pl.BlockSpec((1, d), const),
                  pl.BlockSpec((ROUTE_ROWS, d), const), pl.BlockSpec((ROUTE_ROWS, 1), const)],
        out_specs=[pl.BlockSpec((tm, d), lambda i: (i, 0)),
                   pl.BlockSpec((N_EXPERTS, tm), lambda i: (0, i))],
        out_shape=[jax.ShapeDtypeStruct((t, d), F32), jax.ShapeDtypeStruct((N_EXPERTS, t), F32)],
        compiler_params=_cp(("parallel",), 48),
        name="proj_ln_route",
    )(a, w, b, x, ln_g, ln_b, wr, br)


def _moe_body(x1_ref, comb_ref, uw_ref, lw_ref, wg_ref, wu_ref, wd_ref, g2_ref, b2_ref,
              p_ref, up_ref, gate_ref, o_ref, xb_scr, y_scr, vrow_scr, vcol_scr, *, tm, cap):
    e = pl.program_id(1)

    @pl.when(e == 0)
    def _():
        xb_scr[...] = x1_ref[...].astype(BF16)
        y_scr[...] = jnp.zeros(y_scr.shape, F32)
        member = jnp.where(comb_ref[...] > 0.0, 1.0, 0.0).astype(BF16)
        vrow_scr[...] = jnp.dot(member, uw_ref[...], preferred_element_type=F32)
        vcol_scr[...] = lax.dot_general(lw_ref[...], member, NT_DIMS, preferred_element_type=F32)

    vrow = vrow_scr[pl.ds(e, 1), :]
    sel = lax.broadcasted_iota(jnp.int32, (tm, N_EXPERTS), 1) == e
    vcol = jnp.sum(jnp.where(sel, vcol_scr[...], 0.0), axis=1, keepdims=True)
    w = comb_ref[pl.ds(e, 1), :]
    w_hi = w.astype(BF16)
    r1 = w - w_hi.astype(F32)
    w_mid = r1.astype(BF16)
    w_lo = (r1 - w_mid.astype(F32)).astype(BF16)
    prow = lax.broadcasted_iota(jnp.int32, (8, tm), 0)
    w3 = jnp.where(prow == 0, w_hi.astype(F32),
                   jnp.where(prow == 1, w_mid.astype(F32),
                             jnp.where(prow == 2, w_lo.astype(F32), 0.0))).astype(BF16)
    count = (jnp.max(vrow) + 1.0) * 0.5
    n_chunk = (count.astype(jnp.int32) + (cap - 1)) // cap

    def chunk(c, carry):
        base = (c * cap).astype(F32)
        tgt_r = 2.0 * (base + lax.broadcasted_iota(jnp.int32, (cap, 1), 0).astype(F32)) + 1.0
        tgt_c = 2.0 * (base + lax.broadcasted_iota(jnp.int32, (1, cap), 1).astype(F32)) + 1.0
        gather = jnp.where(vrow == tgt_r, 1.0, 0.0).astype(BF16)
        scatter = jnp.where(vcol == tgt_c, 1.0, 0.0).astype(BF16)
        xg = jnp.dot(gather, xb_scr[...], preferred_element_type=F32).astype(BF16)
        ws = lax.dot_general(gather, w3, NT_DIMS, preferred_element_type=F32)
        wslot = ws[:, 0:1] + ws[:, 1:2] + ws[:, 2:3]
        hg = jnp.dot(xg, wg_ref[...], preferred_element_type=F32)
        hu = jnp.dot(xg, wu_ref[...], preferred_element_type=F32)
        hid = hg * _sigmoid(hg) * hu * wslot
        o = jnp.dot(hid.astype(BF16), wd_ref[...], preferred_element_type=F32)
        y_scr[...] += jnp.dot(scatter, o.astype(BF16), preferred_element_type=F32)
        return carry

    lax.fori_loop(0, n_chunk, chunk, 0)

    @pl.when(e == N_EXPERTS - 1)
    def _():
        x2 = _layer_norm(ALPHA * x1_ref[...] + y_scr[...], g2_ref[...], b2_ref[...])
        up = jnp.dot(p_ref[...].astype(BF16), up_ref[...], preferred_element_type=F32)
        gt = jnp.dot(x2.astype(BF16), gate_ref[...], preferred_element_type=F32)
        o_ref[...] = x2 + up * _sigmoid(gt)


def moe_ple(x1, comb, uw, lw, wg, wu, wd, ln_g, ln_b, p, up, gate, *, tm=MOE_TM, cap=MOE_CAP):
    t, d = x1.shape
    ff = EXPERT_FF
    const = lambda i, e: (0, 0)
    return pl.pallas_call(
        functools.partial(_moe_body, tm=tm, cap=cap),
        grid=(t // tm, N_EXPERTS),
        in_specs=[pl.BlockSpec((tm, d), lambda i, e: (i, 0)),
                  pl.BlockSpec((N_EXPERTS, tm), lambda i, e: (0, i)),
                  pl.BlockSpec((tm, tm), const), pl.BlockSpec((tm, tm), const),
                  pl.BlockSpec((None, d, ff), lambda i, e: (e, 0, 0)),
                  pl.BlockSpec((None, d, ff), lambda i, e: (e, 0, 0)),
                  pl.BlockSpec((None, ff, d), lambda i, e: (e, 0, 0)),
                  pl.BlockSpec((1, d), const), pl.BlockSpec((1, d), const),
                  pl.BlockSpec((tm, PLE_DIM), lambda i, e: (i, 0)),
                  pl.BlockSpec((PLE_DIM, d), const), pl.BlockSpec((d, d), const)],
        out_specs=pl.BlockSpec((tm, d), lambda i, e: (i, 0)),
        out_shape=jax.ShapeDtypeStruct((t, d), F32),
        scratch_shapes=[pltpu.VMEM((tm, d), BF16), pltpu.VMEM((tm, d), F32),
                        pltpu.VMEM((N_EXPERTS, tm), F32), pltpu.VMEM((tm, N_EXPERTS), F32)],
        compiler_params=_cp(("parallel", "arbitrary"), 56),
        name="moe_ple",
    )(x1, comb, uw, lw, wg, wu, wd, ln_g, ln_b, p, up, gate)


def _dwconv_body(prev_ref, cur_ref, next_ref, w_ref, b_ref, g_ref, beta_ref, o_ref, buf_scr, sh_scr,
                 *, width, tm, norm_act):
    i = pl.program_id(1)
    n = pl.num_programs(1)
    pad = width // 2
    halo = jnp.zeros(prev_ref.shape, F32)
    buf_scr[0:HALO, :] = jnp.where(i > 0, prev_ref[...], halo)
    buf_scr[HALO:HALO + tm, :] = cur_ref[...]
    buf_scr[HALO + tm:2 * HALO + tm, :] = jnp.where(i < n - 1, next_ref[...], halo)
    offs = [HALO - pad + k for k in range(width)]
    n_sh = tm + 2 * HALO - 8
    for sft in sorted({off % 8 for off in offs}):
        sh_scr[sft, 0:n_sh, :] = buf_scr[sft:sft + n_sh, :]

    def rows(r, carry):
        r0 = pl.multiple_of(r * CONV_ROWS, CONV_ROWS)
        acc = jnp.zeros((CONV_ROWS, o_ref.shape[1]), F32) + b_ref[...]
        for k, off in enumerate(offs):
            start = pl.multiple_of(r0 + (off // 8) * 8, 8)
            acc = acc + sh_scr[off % 8, pl.ds(start, CONV_ROWS), :] * w_ref[k:k + 1, :]
        if norm_act:
            acc = _layer_norm(acc, g_ref[...], beta_ref[...])
            acc = acc * _sigmoid(acc)
        o_ref[pl.ds(r0, CONV_ROWS), :] = acc.astype(o_ref.dtype)
        return carry

    lax.fori_loop(0, tm // CONV_ROWS, rows, 0)


def dwconv(x, w, b, ln_g, ln_b, *, norm_act, out_dtype, tm=256):
    bsz, l, c = x.shape
    width = w.shape[0]
    tc = D_MODEL
    nb = tm // HALO
    last = l // HALO - 1
    return pl.pallas_call(
        functools.partial(_dwconv_body, width=width, tm=tm, norm_act=norm_act),
        grid=(bsz, l // tm, c // tc),
        in_specs=[pl.BlockSpec((None, HALO, tc), lambda bi, i, j: (bi, jnp.maximum(i * nb - 1, 0), j)),
                  pl.BlockSpec((None, tm, tc), lambda bi, i, j: (bi, i, j)),
                  pl.BlockSpec((None, HALO, tc), lambda bi, i, j: (bi, jnp.minimum((i + 1) * nb, last), j)),
                  pl.BlockSpec((width, tc), lambda bi, i, j: (0, j)),
                  pl.BlockSpec((1, tc), lambda bi, i, j: (0, j)),
                  pl.BlockSpec((1, tc), lambda bi, i, j: (0, 0)),
                  pl.BlockSpec((1, tc), lambda bi, i, j: (0, 0))],
        out_specs=pl.BlockSpec((None, None, tm, tc), lambda bi, i, j: (j, bi, i, 0)),
        out_shape=jax.ShapeDtypeStruct((c // tc, bsz, l, tc), out_dtype),
        scratch_shapes=[pltpu.VMEM((tm + 2 * HALO, tc), F32), pltpu.VMEM((8, tm + 2 * HALO, tc), F32)],
        compiler_params=_cp(("parallel", "parallel", "parallel"), 32),
        name="dwconv%d" % width,
    )(x, x, x, w, b, ln_g, ln_b)


def _filter_body(w1t_ref, w1c_ref, w1s_ref, b1_ref, w2_ref, b2_ref, w3_ref, b3_ref, freq_ref, wout_ref,
                 o_ref, *, tl, length):
    i = pl.program_id(0)
    pos = (i * tl + lax.broadcasted_iota(jnp.int32, (tl, 1), 0)).astype(F32)
    t = pos / float(length - 1)
    ang = (2.0 * math.pi / length) * pos
    band = lax.broadcasted_iota(jnp.int32, (1, POS_BANDS), 1).astype(F32)
    bands = 1e-4 + band * ((POS_BANDS - 1 - 1e-4) / (POS_BANDS - 1))
    fw = bands * ang
    freq = freq_ref[...]
    dot = functools.partial(jnp.dot, precision=HIGHEST, preferred_element_type=F32)
    pre = t * w1t_ref[...] + dot(jnp.cos(fw), w1c_ref[...]) - dot(jnp.sin(fw), w1s_ref[...]) + b1_ref[...]
    hcur = jnp.sin(freq * pre)
    hcur = jnp.sin(freq * (dot(hcur, w2_ref[...]) + b2_ref[...]))
    hcur = jnp.sin(freq * (dot(hcur, w3_ref[...]) + b3_ref[...]))
    ch = lax.broadcasted_iota(jnp.int32, (1, D_MODEL), 1).astype(F32)
    deltas = jnp.abs(MIN_DECAY + ch * ((MAX_DECAY - MIN_DECAY) / (D_MODEL - 1)))
    decay = jnp.exp(-t * deltas)
    for j in range(2 * HYENA_ORDER):
        o_ref[j] = dot(hcur, wout_ref[:, j * D_MODEL:(j + 1) * D_MODEL]) * decay


def hyena_filters(length, w1, b1, w2, b2, w3, b3, freq, wout, *, tl=256):
    fwid = FILTER_WIDTH
    const = lambda i: (0, 0)
    nf = 2 * HYENA_ORDER
    return pl.pallas_call(
        functools.partial(_filter_body, tl=tl, length=length),
        grid=(length // tl,),
        in_specs=[pl.BlockSpec((1, fwid), const), pl.BlockSpec((POS_BANDS, fwid), const),
                  pl.BlockSpec((POS_BANDS, fwid), const), pl.BlockSpec((1, fwid), const),
                  pl.BlockSpec((fwid, fwid), const), pl.BlockSpec((1, fwid), const),
                  pl.BlockSpec((fwid, fwid), const), pl.BlockSpec((1, fwid), const),
                  pl.BlockSpec((1, fwid), const), pl.BlockSpec((fwid, nf * D_MODEL), const)],
        out_specs=pl.BlockSpec((nf, tl, D_MODEL), lambda i: (0, i, 0)),
        out_shape=jax.ShapeDtypeStruct((nf, length, D_MODEL), F32),
        compiler_params=_cp(("parallel",), 32),
        name="hyena_filters",
    )(w1[0:1], w1[1:1 + POS_BANDS], w1[1 + POS_BANDS:], b1, w2, b2, w3, b3, freq, wout)


def _dft_tables(n1, n2s):
    n = n1 * n2s
    m1 = (n1 // 2 + 1 + 7) // 8 * 8
    k1 = jnp.arange(m1, dtype=jnp.int32)
    valid = (k1 <= n1 // 2)
    a = jnp.arange(n1 // 2, dtype=jnp.int32)
    ang_a = (2.0 * math.pi / n1) * ((k1[:, None] * a[None, :]) % n1).astype(F32)
    ca = jnp.where(valid[:, None], jnp.cos(ang_a), 0.0)
    sa = jnp.where(valid[:, None], jnp.sin(ang_a), 0.0)
    g_a = jnp.concatenate([ca, -sa], axis=0)
    g_ai = jnp.concatenate([ca.T, -sa.T], axis=1)
    weight = jnp.where(valid, jnp.where((k1 == 0) | (k1 == n1 // 2), 1.0, 2.0), 0.0)
    b = jnp.arange(n2s, dtype=jnp.int32)
    k2 = jnp.arange(n2s, dtype=jnp.int32)
    idx = (b[None, None, :] * (k2[None, :, None] * n1 + k1[:, None, None])) % n
    ang_c = (2.0 * math.pi / n) * idx.astype(F32)
    cc, sc = jnp.cos(ang_c), jnp.sin(ang_c)
    g_c = jnp.concatenate([jnp.concatenate([cc, sc], axis=2),
                           jnp.concatenate([-sc, cc], axis=2)], axis=1)
    cct = jnp.swapaxes(cc, 1, 2) * (weight / n)[:, None, None]
    sct = jnp.swapaxes(sc, 1, 2) * (weight / n)[:, None, None]
    g_ci = jnp.concatenate([jnp.concatenate([cct, -sct], axis=2),
                            jnp.concatenate([sct, cct], axis=2)], axis=1)
    return m1, g_a.astype(BF16), g_c.astype(BF16), g_ci.astype(BF16), g_ai.astype(BF16)


def _dft_dot(g_ref, x):
    return jnp.dot(g_ref[...], x.astype(BF16), preferred_element_type=F32)


def _dft_a_body(g_ref, x_ref, o_ref, *, nb, m1):
    for bb in range(nb):
        res = _dft_dot(g_ref, x_ref[:, bb, :])
        o_ref[:, 0, bb, :] = res[:m1]
        o_ref[:, 1, bb, :] = res[m1:]


def _dft_slab_body(*refs, bt, spec):
    if spec:
        g1_ref, g2_ref, x_ref, hf_ref, hb_ref, o_ref = refs
        half = g1_ref.shape[0] // 2
        hr = hf_ref[:half] + hb_ref[:half]
        hi = hf_ref[half:] - hb_ref[half:]
    else:
        g1_ref, x_ref, o_ref = refs
    for bi in range(bt):
        xs = _dft_dot(g1_ref, x_ref[bi])
        if spec:
            xr, xi = xs[:half], xs[half:]
            ys = jnp.concatenate([xr * hr - xi * hi, xr * hi + xi * hr], axis=0)
            xs = _dft_dot(g2_ref, ys)
        o_ref[bi] = xs


def _idft_gate_body(g_ref, x_ref, v_ref, gate_ref, bias_ref, o_ref, *, nb):
    for bb in range(nb):
        spec = jnp.concatenate([x_ref[:, 0, bb, :], x_ref[:, 1, bb, :]], axis=0)
        y = _dft_dot(g_ref, spec)
        o_ref[:, bb, :] = gate_ref[:, bb, :] * (y + v_ref[:, bb, :] * bias_ref[...])


def _dft_a(g, x, xsel, *, m1, nb, dt):
    _, bsz, half, n2s, d = x.shape
    return pl.pallas_call(
        functools.partial(_dft_a_body, nb=nb, m1=m1),
        grid=(bsz, n2s // nb, d // dt),
        in_specs=[pl.BlockSpec((2 * m1, half), lambda bi, bj, di: (0, 0)),
                  pl.BlockSpec((None, None, half, nb, dt), lambda bi, bj, di: (xsel, bi, 0, bj, di))],
        out_specs=pl.BlockSpec((None, m1, 2, nb, dt), lambda bi, bj, di: (bi, 0, 0, bj, di)),
        out_shape=jax.ShapeDtypeStruct((bsz, m1, 2, n2s, d), F32),
        compiler_params=_cp(("parallel", "parallel", "parallel"), 48),
        name="dft_a",
    )(g, x)


def _dft_slab(g1, g2, x, hspec, order, *, bt):
    bsz, s, r, d = x.shape
    spec = g2 is not None
    gspec = pl.BlockSpec((None, r, r), lambda si, bi: (si, 0, 0))
    xspec = pl.BlockSpec((bt, None, r, d), lambda si, bi: (bi, si, 0, 0))
    if spec:
        in_specs = [gspec, gspec, xspec,
                    pl.BlockSpec((None, None, r, d), lambda si, bi: (2 * order, si, 0, 0)),
                    pl.BlockSpec((None, None, r, d), lambda si, bi: (2 * order + 1, si, 0, 0))]
        args = (g1, g2, x, hspec, hspec)
    else:
        in_specs, args = [gspec, xspec], (g1, x)
    return pl.pallas_call(
        functools.partial(_dft_slab_body, bt=bt, spec=spec),
        grid=(s, bsz // bt),
        in_specs=in_specs,
        out_specs=xspec,
        out_shape=jax.ShapeDtypeStruct((bsz, s, r, d), F32),
        compiler_params=_cp(("parallel", "parallel"), 48),
        name="dft_spec" if spec else "dft_c",
    )(*args)


def _idft_gate(g, x, v, vsel, gates, gsel, bias, *, nb, dt):
    bsz, m1, _, n2s, d = x.shape
    half = g.shape[0]
    tspec = lambda sel: pl.BlockSpec((None, None, half, nb, dt), lambda bi, bj, di: (sel, bi, 0, bj, di))
    return pl.pallas_call(
        functools.partial(_idft_gate_body, nb=nb),
        grid=(bsz, n2s // nb, d // dt),
        in_specs=[pl.BlockSpec((half, 2 * m1), lambda bi, bj, di: (0, 0)),
                  pl.BlockSpec((None, m1, 2, nb, dt), lambda bi, bj, di: (bi, 0, 0, bj, di)),
                  tspec(vsel), tspec(gsel),
                  pl.BlockSpec((1, dt), lambda bi, bj, di: (0, di))],
        out_specs=pl.BlockSpec((None, half, nb, dt), lambda bi, bj, di: (bi, 0, bj, di)),
        out_shape=jax.ShapeDtypeStruct((bsz, half, n2s, d), F32),
        compiler_params=_cp(("parallel", "parallel", "parallel"), 48),
        name="idft_gate",
    )(g, x, v, gates, bias)


def _fft_plan(length):
    n = 2 * length
    n1 = 256 if n >= 32768 else 64
    return n1, n // n1


def hyena_long_convs(xs, filt, f_bias):
    _, bsz, length, d = xs.shape
    n1, n2s = _fft_plan(length)
    m1, g_a, g_c, g_ci, g_ai = _dft_tables(n1, n2s)
    half = n1 // 2
    big = length >= 8192
    bt = 1 if big else min(8, bsz)
    nb = 8 if big else 16
    dt = d // 2 if big else d
    fa = _dft_a(g_a, filt.reshape(1, 4, half, n2s, d), 0, m1=m1, nb=nb, dt=dt)
    hspec = _dft_slab(g_c, None, fa.reshape(4, m1, 2 * n2s, d), None, 0, bt=1)
    xs6 = xs.reshape(3, bsz, half, n2s, d)
    z, zsel = xs6, 2
    out = None
    for order in range(HYENA_ORDER):
        sa = _dft_a(g_a, z, zsel, m1=m1, nb=nb, dt=dt).reshape(bsz, m1, 2 * n2s, d)
        sb = _dft_slab(g_c, g_ci, sa, hspec, order, bt=bt).reshape(bsz, m1, 2, n2s, d)
        out = _idft_gate(g_ai, sb, z, zsel, xs6, order, f_bias[order:order + 1], nb=nb, dt=dt)
        z, zsel = out[None], 0
    return out.reshape(bsz, length, d)


def _prefix_tables(tm):
    r = jnp.arange(tm, dtype=jnp.int32)
    upper = jnp.where(r[:, None] < r[None, :], 2.0, jnp.where(r[:, None] == r[None, :], 1.0, 0.0))
    return upper.astype(BF16), upper.T.astype(BF16)


def _trunk(x, p, wts):
    bsz, length, d = x.shape
    t = bsz * length
    xf = x.reshape(t, d)
    zeros_d = jnp.zeros((1, d), F32)
    ones_d = jnp.ones((1, d), F32)
    uw, lw = _prefix_tables(MOE_TM)
    for i in range(DEPTH):
        j, kind = i // N_MIXERS, i % N_MIXERS
        if kind == 0:
            lam_init = 0.8 - 0.6 * math.exp(-0.3 * i)
            qkv = linear(xf, wts["attn_w_qkv"][j], jnp.zeros((1, 3 * d), F32), out_dtype=BF16, name="attn_qkv")
            a = diff_attention(qkv.reshape(bsz, length, 3 * d), wts["attn_lam"][j], wts["attn_subln_g"][j],
                               lam_init=lam_init).reshape(t, d)
            w_o, b_o = wts["attn_w_o"][j], zeros_d
        elif kind == 1:
            hglu = linear(xf, wts["conv_w_pw1"][j], wts["conv_b_pw1"][j], out_dtype=F32, glu=True, name="conv_pw1")
            a = dwconv(hglu.reshape(bsz, length, d), wts["conv_w_dw"][j], wts["conv_b_dw"][j],
                       wts["conv_ln_g"][j], wts["conv_ln_b"][j], norm_act=True, out_dtype=BF16).reshape(t, d)
            w_o, b_o = wts["conv_w_pw2"][j], wts["conv_b_pw2"][j]
        else:
            u = linear(xf, wts["hy_w_in"][j], wts["hy_b_in"][j], out_dtype=F32, name="hyena_in")
            xs = dwconv(u.reshape(bsz, length, 3 * d), wts["hy_w_short"][j], wts["hy_b_short"][j],
                        ones_d, zeros_d, norm_act=False, out_dtype=F32)
            filt = hyena_filters(length, wts["hy_f_w1"][j], wts["hy_f_b1"][j], wts["hy_f_w2"][j], wts["hy_f_b2"][j],
                                 wts["hy_f_w3"][j], wts["hy_f_b3"][j], wts["hy_f_freq"][j], wts["hy_f_wout"][j])
            a = hyena_long_convs(xs, filt, wts["hy_f_bias"][j]).reshape(t, d)
            w_o, b_o = wts["hy_w_out"][j], wts["hy_b_out"][j]
        x1, comb = proj_ln_route(a, w_o, b_o, xf, wts["ln1_g"][i], wts["ln1_b"][i],
                                 wts["route_w"][i], wts["route_b"][i])
        xf = moe_ple(x1, comb, uw, lw, wts["moe_w_gate"][i], wts["moe_w_up"][i], wts["moe_w_down"][i],
                     wts["ln2_g"][i], wts["ln2_b"][i], p[i].reshape(t, PLE_DIM),
                     wts["ple_w_up"][i], wts["ple_w_gate"][i])
    return xf.reshape(bsz, length, d)


def kernel(x_prompt, x_sample, p_prompt, p_sample, attn_w_qkv, attn_w_o, attn_lam_q1, attn_lam_k1, attn_lam_q2, attn_lam_k2, attn_subln_g, conv_w_pw1, conv_b_pw1, conv_w_dw, conv_b_dw, conv_ln_g, conv_ln_b, conv_w_pw2, conv_b_pw2, hy_w_in, hy_b_in, hy_w_short, hy_b_short, hy_f_w1, hy_f_b1, hy_f_w2, hy_f_b2, hy_f_w3, hy_f_b3, hy_f_freq, hy_f_wout, hy_f_bias, hy_w_out, hy_b_out, ln1_g, ln1_b, ln2_g, ln2_b, moe_w_group, moe_b_group, moe_w_expert, moe_b_expert, moe_w_gate, moe_w_up, moe_w_down, ple_w_up, ple_w_gate):
    d = D_MODEL
    row = lambda a: a[:, None, :]
    q_scale = jnp.concatenate([jnp.full((d,), HEAD_DIM ** -0.5 * LOG2E, F32), jnp.ones((2 * d,), F32)])
    route_w = jnp.concatenate([jnp.swapaxes(moe_w_group, 1, 2), jnp.swapaxes(moe_w_expert, 1, 2),
                               jnp.zeros((DEPTH, ROUTE_ROWS - N_GROUPS - N_EXPERTS, d), F32)], axis=1)
    route_b = jnp.concatenate([moe_b_group, moe_b_expert,
                               jnp.zeros((DEPTH, ROUTE_ROWS - N_GROUPS - N_EXPERTS), F32)], axis=1)[:, :, None]
    wts = {
        "attn_w_qkv": (attn_w_qkv * q_scale).astype(BF16),
        "attn_w_o": attn_w_o.astype(BF16),
        "attn_lam": jnp.stack([attn_lam_q1, attn_lam_k1, attn_lam_q2, attn_lam_k2], axis=1),
        "attn_subln_g": attn_subln_g[:, :, None],
        "conv_w_pw1": conv_w_pw1.astype(BF16), "conv_b_pw1": row(conv_b_pw1),
        "conv_w_dw": conv_w_dw, "conv_b_dw": row(conv_b_dw),
        "conv_ln_g": row(conv_ln_g), "conv_ln_b": row(conv_ln_b),
        "conv_w_pw2": conv_w_pw2.astype(BF16), "conv_b_pw2": row(conv_b_pw2),
        "hy_w_in": hy_w_in.astype(BF16), "hy_b_in": row(hy_b_in),
        "hy_w_short": hy_w_short, "hy_b_short": row(hy_b_short),
        "hy_f_w1": hy_f_w1, "hy_f_b1": row(hy_f_b1), "hy_f_w2": hy_f_w2, "hy_f_b2": row(hy_f_b2),
        "hy_f_w3": hy_f_w3, "hy_f_b3": row(hy_f_b3), "hy_f_freq": row(hy_f_freq), "hy_f_wout": hy_f_wout,
        "hy_f_bias": hy_f_bias,
        "hy_w_out": hy_w_out.astype(BF16), "hy_b_out": row(hy_b_out),
        "ln1_g": row(ln1_g), "ln1_b": row(ln1_b), "ln2_g": row(ln2_g), "ln2_b": row(ln2_b),
        "route_w": route_w, "route_b": route_b,
        "moe_w_gate": moe_w_gate.astype(BF16), "moe_w_up": moe_w_up.astype(BF16),
        "moe_w_down": moe_w_down.astype(BF16),
        "ple_w_up": ple_w_up.astype(BF16), "ple_w_gate": ple_w_gate.astype(BF16),
    }
    y_prompt = _trunk(x_prompt, p_prompt, wts)
    y_sample = _trunk(x_sample, p_sample, wts)
    return (y_prompt, y_sample)
```

```python
import functools
import math

import jax
import jax.numpy as jnp
from jax import lax
from jax.experimental import pallas as pl
from jax.experimental.pallas import tpu as pltpu

F32 = jnp.float32
BF16 = jnp.bfloat16
HIGHEST = lax.Precision.HIGHEST

D_MODEL = 1024
DEPTH = 4
N_MIXERS = 3
N_HEADS = 8
HEAD_DIM = 64
CONV_WIDTH = 31
HYENA_ORDER = 2
SHORT_WIDTH = 3
POS_BANDS = 16
FILTER_WIDTH = 64
MAX_DECAY = math.log(1e-2) / 0.3
MIN_DECAY = math.log(1e-2) / 1.5
N_GROUPS = 4
EXPERTS_PER_GROUP = 4
N_EXPERTS = N_GROUPS * EXPERTS_PER_GROUP
EXPERT_FF = 512
PLE_DIM = 256
ALPHA = (2 * DEPTH) ** 0.25
LN_EPS = 1e-5
LOG2E = 1.4426950408889634

HALO = 16
CONV_ROWS = 16
ROUTE_ROWS = 24
MOE_TM = 1024
MOE_ST = 256
MOE_CAPS = 64
ATT_TQ = 512
ATT_TK = 1024
ATT_CW = 256
ATT_FILL = 512
NT_DIMS = (((1,), (1,)), ((), ()))


def _cp(sem, vmem_mb):
    return pltpu.CompilerParams(dimension_semantics=sem, vmem_limit_bytes=vmem_mb << 20)


def _layer_norm(z, g, b):
    mu = jnp.mean(z, axis=-1, keepdims=True)
    zc = z - mu
    var = jnp.mean(zc * zc, axis=-1, keepdims=True)
    return zc * lax.rsqrt(var + LN_EPS) * g + b


def _sigmoid(x):
    return 1.0 / (1.0 + jnp.exp(-x))


def _linear_body(x_ref, w_ref, b_ref, o_ref, *, nc, glu):
    x = x_ref[...].astype(BF16)
    n_out = o_ref.shape[-1]
    for n0 in range(0, n_out, nc):
        a = jnp.dot(x, w_ref[:, n0:n0 + nc], preferred_element_type=F32) + b_ref[:, n0:n0 + nc]
        if glu:
            g = (jnp.dot(x, w_ref[:, n_out + n0:n_out + n0 + nc], preferred_element_type=F32)
                 + b_ref[:, n_out + n0:n_out + n0 + nc])
            a = a * _sigmoid(g)
        o_ref[:, n0:n0 + nc] = a.astype(o_ref.dtype)


def linear(x, w, b, *, out_dtype, glu=False, tm=512, nc=512, name="linear"):
    t, k = x.shape
    n = w.shape[1]
    n_out = n // 2 if glu else n
    return pl.pallas_call(
        functools.partial(_linear_body, nc=nc, glu=glu),
        grid=(t // tm,),
        in_specs=[pl.BlockSpec((tm, k), lambda i: (i, 0)),
                  pl.BlockSpec((k, n), lambda i: (0, 0)),
                  pl.BlockSpec((1, n), lambda i: (0, 0))],
        out_specs=pl.BlockSpec((tm, n_out), lambda i: (i, 0)),
        out_shape=jax.ShapeDtypeStruct((t, n_out), out_dtype),
        compiler_params=_cp(("parallel",), 48),
        name=name,
    )(x, w, b)


def _split3(x):
    hi = x.astype(BF16).astype(F32)
    mid = (x - hi).astype(BF16).astype(F32)
    lo = (x - hi - mid).astype(BF16).astype(F32)
    return [hi, mid, lo]


def _lane_table(lane, values, first):
    out = jnp.zeros(lane.shape, F32)
    for n, val in enumerate(values):
        out = jnp.where(lane == first + n, val, out)
    return out


def _attn_body(lam_ref, g_ref, q_ref, k_ref, v_ref, o_ref, kp_scr, vt_scr, d0_scr, qq_scr, s0_scr, s1_scr,
               m_scr, l_scr, acc_scr, *, tq, tk, lam_init):
    h = pl.program_id(1)
    qi = pl.program_id(2)
    length = k_ref.shape[0]
    nk = length // tk
    n_other = nk - 1
    hd2 = 2 * HEAD_DIM
    slope2 = jnp.exp2(-(jnp.full((1, 1), h, jnp.int32) + 1).astype(F32)) * LOG2E
    c1 = _split3(slope2)
    c128 = [128.0 * c for c in c1]

    @pl.when(qi == 0)
    def _():
        d0_scr[...] = (lax.broadcasted_iota(jnp.int32, (tk, tq), 1)
                       - lax.broadcasted_iota(jnp.int32, (tk, tq), 0)).astype(F32)
        lane = lax.broadcasted_iota(jnp.int32, (ATT_FILL, hd2), 1)
        consts = _lane_table(lane, [-c for c in c128] + [-c for c in c1], 0)
        eye = jnp.where(lax.broadcasted_iota(jnp.int32, (hd2, hd2), 0)
                        == lax.broadcasted_iota(jnp.int32, (hd2, hd2), 1), 1.0, 0.0).astype(BF16)

        def fill(ci, carry):
            r0 = pl.multiple_of(ci * ATT_FILL, ATT_FILL)
            pos = r0 + lax.broadcasted_iota(jnp.int32, (ATT_FILL, hd2), 0)
            hi_digit = jnp.right_shift(pos, 7).astype(F32)
            lo_digit = jnp.bitwise_and(pos, 127).astype(F32)
            aug = jnp.where(lane < 6, consts, jnp.where(lane < 9, hi_digit, jnp.where(lane < 12, lo_digit, 0.0)))
            kp_scr[pl.ds(r0, ATT_FILL), 0:hd2] = k_ref[pl.ds(r0, ATT_FILL), :]
            kp_scr[pl.ds(r0, ATT_FILL), hd2:2 * hd2] = aug.astype(BF16)
            vt_scr[:, pl.ds(r0, ATT_FILL)] = lax.dot_general(
                eye, v_ref[pl.ds(r0, ATT_FILL), :], NT_DIMS, preferred_element_type=F32).astype(BF16)
            return carry

        lax.fori_loop(0, length // ATT_FILL, fill, 0)

    q = q_ref[...]
    lane = lax.broadcasted_iota(jnp.int32, (tq, hd2), 1)
    zero = jnp.zeros_like(q)
    tpos = qi * tq + lax.broadcasted_iota(jnp.int32, (tq, hd2), 0)
    hi_digit = jnp.right_shift(tpos, 7).astype(F32)
    lo_digit = jnp.bitwise_and(tpos, 127).astype(F32)
    consts = _lane_table(lane, c128 + c1, 6)
    augq = jnp.where(lane < 3, hi_digit, jnp.where(lane < 6, lo_digit, consts))
    for var, aug in enumerate((augq.astype(BF16), (-augq).astype(BF16))):
        qq_scr[var, 0:tq, 0:hd2] = jnp.where(lane < HEAD_DIM, q, zero)
        qq_scr[var, tq:2 * tq, 0:hd2] = jnp.where(lane >= HEAD_DIM, q, zero)
        qq_scr[var, 0:tq, hd2:2 * hd2] = aug
        qq_scr[var, tq:2 * tq, hd2:2 * hd2] = aug

    m_scr[...] = jnp.full(m_scr.shape, -jnp.inf, F32)
    l_scr[...] = jnp.zeros(l_scr.shape, F32)
    acc_scr[...] = jnp.zeros(acc_scr.shape, F32)

    kd = (qi * tq) // tk

    def other_block(n):
        return jnp.where(n < n_other, jnp.where(n < kd, n, n + 1), kd)

    def scores(blk, s_scr):
        ks = pl.multiple_of(blk * tk, tk)
        var = (blk > kd).astype(jnp.int32)
        s_scr[...] = lax.dot_general(kp_scr[pl.ds(ks, tk), :], qq_scr[var], NT_DIMS, preferred_element_type=F32)

    def softmax_pv(blk, s_scr, diag):
        ks = pl.multiple_of(blk * tk, tk)
        cdiag = (qi * tq - blk * tk).astype(F32)
        vt = vt_scr[:, pl.ds(ks, tk)]
        m_all = m_scr[...]
        l_all = l_scr[...]
        m_out, l_out = [], []
        for cg in range(2 * tq // ATT_CW):
            cols = slice(cg * ATT_CW, (cg + 1) * ATT_CW)
            s = s_scr[:, cols]
            if diag:
                dc = (cg * ATT_CW) % tq
                s = s + jnp.minimum(d0_scr[:, dc:dc + ATT_CW] + cdiag, 0.0) * (2.0 * slope2)
            m_old = m_all[:, cols]
            m_new = jnp.maximum(m_old, jnp.max(s, axis=0, keepdims=True))
            p = jnp.exp2(s - m_new)
            alpha = jnp.exp2(m_old - m_new)
            l_out.append(alpha * l_all[:, cols] + jnp.sum(p, axis=0, keepdims=True))
            m_out.append(m_new)
            pv = jnp.dot(vt, p.astype(BF16), preferred_element_type=F32)
            acc_scr[:, cols] = alpha * acc_scr[:, cols] + pv
        m_scr[...] = jnp.concatenate(m_out, axis=1)
        l_scr[...] = jnp.concatenate(l_out, axis=1)

    scores(other_block(0), s0_scr)

    def pair(n2, carry):
        n = 2 * n2
        scores(other_block(n + 1), s1_scr)
        softmax_pv(other_block(n), s0_scr, False)
        scores(other_block(n + 2), s0_scr)
        softmax_pv(other_block(n + 1), s1_scr, False)
        return carry

    lax.fori_loop(0, n_other // 2, pair, 0)
    if n_other % 2:
        scores(kd, s1_scr)
        softmax_pv(other_block(n_other - 1), s0_scr, False)
        softmax_pv(kd, s1_scr, True)
    else:
        softmax_pv(kd, s0_scr, True)

    lam = lam_ref[...]
    lam_full = (jnp.exp(jnp.sum(lam[0:1] * lam[1:2], axis=-1, keepdims=True))
                - jnp.exp(jnp.sum(lam[2:3] * lam[3:4], axis=-1, keepdims=True)) + lam_init)
    ot = acc_scr[...] / l_scr[...]
    ot = ot[:, :tq] - lam_full * ot[:, tq:]
    ms = jnp.mean(ot * ot, axis=0, keepdims=True)
    ot = ot * lax.rsqrt(ms + LN_EPS) * g_ref[...] * (1.0 - lam_init)
    o_ref[...] = jnp.transpose(ot).astype(o_ref.dtype)


def diff_attention(qkv, lam, subln_g, *, lam_init, tq=ATT_TQ, tk=ATT_TK):
    b, l, _ = qkv.shape
    assert l % tk == 0 and l // tk >= 2 and tk % tq == 0 and l <= 128 * 128
    hd2 = 2 * HEAD_DIM
    return pl.pallas_call(
        functools.partial(_attn_body, tq=tq, tk=tk, lam_init=lam_init),
        grid=(b, N_HEADS, l // tq),
        in_specs=[pl.BlockSpec((4, HEAD_DIM), lambda bi, h, qi: (0, 0)),
                  pl.BlockSpec((hd2, 1), lambda bi, h, qi: (0, 0)),
                  pl.BlockSpec((None, tq, hd2), lambda bi, h, qi: (bi, qi, h)),
                  pl.BlockSpec((None, l, hd2), lambda bi, h, qi: (bi, 0, N_HEADS + h)),
                  pl.BlockSpec((None, l, hd2), lambda bi, h, qi: (bi, 0, 2 * N_HEADS + h))],
        out_specs=pl.BlockSpec((None, tq, hd2), lambda bi, h, qi: (bi, qi, h)),
        out_shape=jax.ShapeDtypeStruct((b, l, D_MODEL), BF16),
        scratch_shapes=[pltpu.VMEM((l, 2 * hd2), BF16), pltpu.VMEM((hd2, l), BF16), pltpu.VMEM((tk, tq), F32),
                        pltpu.VMEM((2, 2 * tq, 2 * hd2), BF16),
                        pltpu.VMEM((tk, 2 * tq), F32), pltpu.VMEM((tk, 2 * tq), F32),
                        pltpu.VMEM((1, 2 * tq), F32), pltpu.VMEM((1, 2 * tq), F32),
                        pltpu.VMEM((hd2, 2 * tq), F32)],
        compiler_params=_cp(("parallel", "parallel", "arbitrary"), 56),
        name="diff_attention",
    )(lam, subln_g, qkv, qkv, qkv)


def _route(x1, wr, br, comb_ref):
    lt = lax.dot_general(wr, x1, NT_DIMS, precision=HIGHEST, preferred_element_type=F32) + br
    gl = [lt[g:g + 1] for g in range(N_GROUPS)]
    gmax = jnp.maximum(jnp.maximum(gl[0], gl[1]), jnp.maximum(gl[2], gl[3]))
    gidx = jnp.where(gl[0] == gmax, 0, jnp.where(gl[1] == gmax, 1, jnp.where(gl[2] == gmax, 2, 3)))
    gw = 1.0 / (jnp.exp(gl[0] - gmax) + jnp.exp(gl[1] - gmax) + jnp.exp(gl[2] - gmax) + jnp.exp(gl[3] - gmax))
    el = []
    for j in range(EXPERTS_PER_GROUP):
        acc = jnp.zeros_like(gmax)
        for g in range(N_GROUPS):
            r = N_GROUPS + g * EXPERTS_PER_GROUP + j
            acc = jnp.where(gidx == g, lt[r:r + 1], acc)
        el.append(acc)
    v1 = jnp.maximum(jnp.maximum(el[0], el[1]), jnp.maximum(el[2], el[3]))
    i1 = jnp.where(el[0] == v1, 0, jnp.where(el[1] == v1, 1, jnp.where(el[2] == v1, 2, 3)))
    neg = jnp.full_like(v1, -jnp.inf)
    el2 = [jnp.where(i1 == j, neg, el[j]) for j in range(EXPERTS_PER_GROUP)]
    v2 = jnp.maximum(jnp.maximum(el2[0], el2[1]), jnp.maximum(el2[2], el2[3]))
    i2 = jnp.where(el2[0] == v2, 0, jnp.where(el2[1] == v2, 1, jnp.where(el2[2] == v2, 2, 3)))
    e21 = jnp.exp(v2 - v1)
    w1 = gw / (1.0 + e21)
    w2 = gw * e21 / (1.0 + e21)
    zero = jnp.zeros_like(v1)
    for g in range(N_GROUPS):
        for j in range(EXPERTS_PER_GROUP):
            wj = jnp.where(i1 == j, w1, jnp.where(i2 == j, w2, zero))
            r = g * EXPERTS_PER_GROUP + j
            comb_ref[r:r + 1, :] = jnp.where(gidx == g, wj, zero)


def _proj_ln_route_body(a_ref, w_ref, b_ref, x_ref, g_ref, beta_ref, wr_ref, br_ref, x1_ref, comb_ref):
    h = jnp.dot(a_ref[...].astype(BF16), w_ref[...], preferred_element_type=F32) + b_ref[...]
    x1 = _layer_norm(ALPHA * x_ref[...] + h, g_ref[...], beta_ref[...])
    x1_ref[...] = x1
    _route(x1, wr_ref[...], br_ref[...], comb_ref)


def proj_ln_route(a, w, b, x, ln_g, ln_b, wr, br, *, tm=512):
    t, k = a.shape
    d = D_MODEL
    const = lambda i: (0, 0)
    return pl.pallas_call(
        _proj_ln_route_body,
        grid=(t // tm,),
        in_specs=[pl.BlockSpec((tm, k), lambda i: (i, 0)),
                  pl.BlockSpec((k, d), const), pl.BlockSpec((1, d), const),
                  pl.BlockSpec((tm, d), lambda i: (i, 0)),
                  pl.BlockSpec((1, d), const), pl.BlockSpec((1, d), const),
                  pl.BlockSpec((ROUTE_ROWS, d), const), pl.BlockSpec((ROUTE_ROWS, 1), const)],
        out_specs=[pl.BlockSpec((tm, d), lambda i: (i, 0)),
                   pl.BlockSpec((N_EXPERTS, tm), lambda i: (0, i))],
        out_shape=[jax.ShapeDtypeStruct((t, d), F32), jax.ShapeDtypeStruct((N_EXPERTS, t), F32)],
        compiler_params=_cp(("parallel",), 48),
        name="proj_ln_route",
    )(a, w, b, x, ln_g, ln_b, wr, br)


def _moe_body(x1_ref, comb_ref, uw_ref, lw_ref, wg_ref, wu_ref, wd_ref, g2_ref, b2_ref, p_ref, up_ref, gate_ref,
              o_ref, y_scr, vrow_scr, vcol_scr, xg_scr, og_scr, wsl_scr, fast_scr, *, tm, st, caps):
    e = pl.program_id(1)
    n_sub = tm // st

    def sub_rows(sub):
        return slice(sub * st, (sub + 1) * st)

    @pl.when(e == 0)
    def _():
        member = jnp.where(comb_ref[...] > 0.0, 1.0, 0.0).astype(BF16)
        for sub in range(n_sub):
            ms = member[:, sub_rows(sub)]
            vrow_scr[:, sub_rows(sub)] = jnp.dot(ms, uw_ref[...], preferred_element_type=F32)
            vcol_scr[sub_rows(sub), :] = lax.dot_general(lw_ref[...], ms, NT_DIMS, preferred_element_type=F32)
        most = ((jnp.max(vrow_scr[...]) + 1.0) * 0.5).astype(jnp.int32)
        fast_scr[0] = (most <= caps).astype(jnp.int32)

        @pl.when(most <= caps)
        def _():
            tgt = 2.0 * lax.broadcasted_iota(jnp.int32, (caps, 1), 0).astype(F32) + 1.0
            for sub in range(n_sub):
                vr = vrow_scr[:, sub_rows(sub)]
                cw = comb_ref[:, sub_rows(sub)]
                blocks = []
                for ee in range(N_EXPERTS):
                    hit = vr[ee:ee + 1, :] == tgt
                    blocks.append(jnp.where(hit, 1.0, 0.0).astype(BF16))
                    wsl_scr[ee, sub * caps:(sub + 1) * caps, :] = jnp.sum(
                        jnp.where(hit, cw[ee:ee + 1, :], 0.0), axis=1, keepdims=True)
                gather = jnp.concatenate(blocks, axis=0)
                xg = jnp.dot(gather, x1_ref[sub_rows(sub), :].astype(BF16),
                             preferred_element_type=F32).astype(BF16)
                for ee in range(N_EXPERTS):
                    xg_scr[ee, sub * caps:(sub + 1) * caps, :] = xg[ee * caps:(ee + 1) * caps]

        @pl.when(most > caps)
        def _():
            y_scr[...] = jnp.zeros(y_scr.shape, F32)

    fast = fast_scr[0] == 1

    def expert_ffn(xg, wslot):
        hg = jnp.dot(xg, wg_ref[...], preferred_element_type=F32)
        hu = jnp.dot(xg, wu_ref[...], preferred_element_type=F32)
        hid = hg * _sigmoid(hg) * hu * wslot
        return jnp.dot(hid.astype(BF16), wd_ref[...], preferred_element_type=F32).astype(BF16)

    @pl.when(fast)
    def _():
        ob = expert_ffn(xg_scr[e], wsl_scr[e])
        r0 = pl.multiple_of(e * caps, caps)
        for sub in range(n_sub):
            og_scr[sub, pl.ds(r0, caps), :] = ob[sub * caps:(sub + 1) * caps]

    @pl.when(jnp.logical_not(fast))
    def _():
        sel = lax.broadcasted_iota(jnp.int32, (st, N_EXPERTS), 1) == e
        prow = lax.broadcasted_iota(jnp.int32, (8, st), 0)
        for sub in range(n_sub):
            vrow = vrow_scr[pl.ds(e, 1), sub_rows(sub)]
            vcol = jnp.sum(jnp.where(sel, vcol_scr[sub_rows(sub), :], 0.0), axis=1, keepdims=True)
            w_hi, w_mid, w_lo = _split3(comb_ref[pl.ds(e, 1), sub_rows(sub)])
            w3 = jnp.where(prow == 0, w_hi, jnp.where(prow == 1, w_mid, jnp.where(prow == 2, w_lo, 0.0))
                           ).astype(BF16)
            count = ((jnp.max(vrow) + 1.0) * 0.5).astype(jnp.int32)
            xsub = x1_ref[sub_rows(sub), :].astype(BF16)

            def chunk(c, carry):
                base = (c * caps).astype(F32)
                tgt_r = 2.0 * (base + lax.broadcasted_iota(jnp.int32, (caps, 1), 0).astype(F32)) + 1.0
                tgt_c = 2.0 * (base + lax.broadcasted_iota(jnp.int32, (1, caps), 1).astype(F32)) + 1.0
                gather = jnp.where(vrow == tgt_r, 1.0, 0.0).astype(BF16)
                scatter = jnp.where(vcol == tgt_c, 1.0, 0.0).astype(BF16)
                xg = jnp.dot(gather, xsub, preferred_element_type=F32).astype(BF16)
                ws = lax.dot_general(gather, w3, NT_DIMS, preferred_element_type=F32)
                ob = expert_ffn(xg, ws[:, 0:1] + ws[:, 1:2] + ws[:, 2:3])
                y_scr[sub_rows(sub), :] += jnp.dot(scatter, ob, preferred_element_type=F32)
                return carry

            lax.fori_loop(0, (count + (caps - 1)) // caps, chunk, 0)

    @pl.when(e == N_EXPERTS - 1)
    def _():
        @pl.when(fast)
        def _():
            lane = lax.broadcasted_iota(jnp.int32, (st, 2 * caps), 1)
            tgt = 2.0 * jnp.where(lane < caps, lane, lane - caps).astype(F32) + 1.0
            for sub in range(n_sub):
                vc = vcol_scr[sub_rows(sub), :]
                pieces = []
                for pr in range(N_EXPERTS // 2):
                    val = jnp.where(lane < caps, vc[:, 2 * pr:2 * pr + 1], vc[:, 2 * pr + 1:2 * pr + 2])
                    pieces.append(jnp.where(val == tgt, 1.0, 0.0).astype(BF16))
                scatter = jnp.concatenate(pieces, axis=1)
                y_scr[sub_rows(sub), :] = jnp.dot(scatter, og_scr[sub], preferred_element_type=F32)

        x2 = _layer_norm(ALPHA * x1_ref[...] + y_scr[...], g2_ref[...], b2_ref[...])
        up = jnp.dot(p_ref[...].astype(BF16), up_ref[...], preferred_element_type=F32)
        gt = jnp.dot(x2.astype(BF16), gate_ref[...], preferred_element_type=F32)
        o_ref[...] = x2 + up * _sigmoid(gt)


def moe_ple(x1, comb, uw, lw, wg, wu, wd, ln_g, ln_b, p, up, gate, *, tm=MOE_TM, st=MOE_ST, caps=MOE_CAPS):
    t, d = x1.shape
    ff = EXPERT_FF
    n_sub = tm // st
    assert 2 * caps == 128 and st % 128 == 0
    const = lambda i, e: (0, 0)
    once = pl.Buffered(1)
    return pl.pallas_call(
        functools.partial(_moe_body, tm=tm, st=st, caps=caps),
        grid=(t // tm, N_EXPERTS),
        in_specs=[pl.BlockSpec((tm, d), lambda i, e: (i, 0), pipeline_mode=once),
                  pl.BlockSpec((N_EXPERTS, tm), lambda i, e: (0, i)),
                  pl.BlockSpec((st, st), const), pl.BlockSpec((st, st), const),
                  pl.BlockSpec((None, d, ff), lambda i, e: (e, 0, 0)),
                  pl.BlockSpec((None, d, ff), lambda i, e: (e, 0, 0)),
                  pl.BlockSpec((None, ff, d), lambda i, e: (e, 0, 0)),
                  pl.BlockSpec((1, d), const), pl.BlockSpec((1, d), const),
                  pl.BlockSpec((tm, PLE_DIM), lambda i, e: (i, 0), pipeline_mode=once),
                  pl.BlockSpec((PLE_DIM, d), const, pipeline_mode=once),
                  pl.BlockSpec((d, d), const, pipeline_mode=once)],
        out_specs=pl.BlockSpec((tm, d), lambda i, e: (i, 0)),
        out_shape=jax.ShapeDtypeStruct((t, d), F32),
        scratch_shapes=[pltpu.VMEM((tm, d), F32),
                        pltpu.VMEM((N_EXPERTS, tm), F32), pltpu.VMEM((tm, N_EXPERTS), F32),
                        pltpu.VMEM((N_EXPERTS, n_sub * caps, d), BF16),
                        pltpu.VMEM((n_sub, N_EXPERTS * caps, d), BF16),
                        pltpu.VMEM((N_EXPERTS, n_sub * caps, 1), F32),
                        pltpu.SMEM((1,), jnp.int32)],
        compiler_params=_cp(("parallel", "arbitrary"), 58),
        name="moe_ple",
    )(x1, comb, uw, lw, wg, wu, wd, ln_g, ln_b, p, up, gate)


def _dwconv_body(prev_ref, cur_ref, next_ref, w_ref, b_ref, g_ref, beta_ref, o_ref, buf_scr, sh_scr,
                 *, width, tm, norm_act):
    i = pl.program_id(1)
    n = pl.num_programs(1)
    pad = width // 2
    halo = jnp.zeros(prev_ref.shape, F32)
    buf_scr[0:HALO, :] = jnp.where(i > 0, prev_ref[...], halo)
    buf_scr[HALO:HALO + tm, :] = cur_ref[...]
    buf_scr[HALO + tm:2 * HALO + tm, :] = jnp.where(i < n - 1, next_ref[...], halo)
    offs = [HALO - pad + k for k in range(width)]
    n_sh = tm + 2 * HALO - 8
    for sft in sorted({off % 8 for off in offs}):
        sh_scr[sft, 0:n_sh, :] = buf_scr[sft:sft + n_sh, :]

    def rows(r, carry):
        r0 = pl.multiple_of(r * CONV_ROWS, CONV_ROWS)
        acc = jnp.zeros((CONV_ROWS, o_ref.shape[1]), F32) + b_ref[...]
        for k, off in enumerate(offs):
            start = pl.multiple_of(r0 + (off // 8) * 8, 8)
            acc = acc + sh_scr[off % 8, pl.ds(start, CONV_ROWS), :] * w_ref[k:k + 1, :]
        if norm_act:
            acc = _layer_norm(acc, g_ref[...], beta_ref[...])
            acc = acc * _sigmoid(acc)
        o_ref[pl.ds(r0, CONV_ROWS), :] = acc.astype(o_ref.dtype)
        return carry

    lax.fori_loop(0, tm // CONV_ROWS, rows, 0)


def dwconv(x, w, b, ln_g, ln_b, *, norm_act, out_dtype, tm=256):
    bsz, l, c = x.shape
    width = w.shape[0]
    tc = D_MODEL
    nb = tm // HALO
    last = l // HALO - 1
    return pl.pallas_call(
        functools.partial(_dwconv_body, width=width, tm=tm, norm_act=norm_act),
        grid=(bsz, l // tm, c // tc),
        in_specs=[pl.BlockSpec((None, HALO, tc), lambda bi, i, j: (bi, jnp.maximum(i * nb - 1, 0), j)),
                  pl.BlockSpec((None, tm, tc), lambda bi, i, j: (bi, i, j)),
                  pl.BlockSpec((None, HALO, tc), lambda bi, i, j: (bi, jnp.minimum((i + 1) * nb, last), j)),
                  pl.BlockSpec((width, tc), lambda bi, i, j: (0, j)),
                  pl.BlockSpec((1, tc), lambda bi, i, j: (0, j)),
                  pl.BlockSpec((1, tc), lambda bi, i, j: (0, 0)),
                  pl.BlockSpec((1, tc), lambda bi, i, j: (0, 0))],
        out_specs=pl.BlockSpec((None, None, tm, tc), lambda bi, i, j: (j, bi, i, 0)),
        out_shape=jax.ShapeDtypeStruct((c // tc, bsz, l, tc), out_dtype),
        scratch_shapes=[pltpu.VMEM((tm + 2 * HALO, tc), F32), pltpu.VMEM((8, tm + 2 * HALO, tc), F32)],
        compiler_params=_cp(("parallel", "parallel", "parallel"), 32),
        name="dwconv%d" % width,
    )(x, x, x, w, b, ln_g, ln_b)


def _filter_body(w1t_ref, w1c_ref, w1s_ref, b1_ref, w2_ref, b2_ref, w3_ref, b3_ref, freq_ref, wout_ref,
                 o_ref, *, tl, length):
    i = pl.program_id(0)
    pos = (i * tl + lax.broadcasted_iota(jnp.int32, (tl, 1), 0)).astype(F32)
    t = pos / float(length - 1)
    ang = (2.0 * math.pi / length) * pos
    band = lax.broadcasted_iota(jnp.int32, (1, POS_BANDS), 1).astype(F32)
    bands = 1e-4 + band * ((POS_BANDS - 1 - 1e-4) / (POS_BANDS - 1))
    fw = bands * ang
    freq = freq_ref[...]
    dot = functools.partial(jnp.dot, precision=HIGHEST, preferred_element_type=F32)
    pre = t * w1t_ref[...] + dot(jnp.cos(fw), w1c_ref[...]) - dot(jnp.sin(fw), w1s_ref[...]) + b1_ref[...]
    hcur = jnp.sin(freq * pre)
    hcur = jnp.sin(freq * (dot(hcur, w2_ref[...]) + b2_ref[...]))
    hcur = jnp.sin(freq * (dot(hcur, w3_ref[...]) + b3_ref[...]))
    ch = lax.broadcasted_iota(jnp.int32, (1, D_MODEL), 1).astype(F32)
    deltas = jnp.abs(MIN_DECAY + ch * ((MAX_DECAY - MIN_DECAY) / (D_MODEL - 1)))
    decay = jnp.exp(-t * deltas)
    for j in range(2 * HYENA_ORDER):
        o_ref[j] = dot(hcur, wout_ref[:, j * D_MODEL:(j + 1) * D_MODEL]) * decay


def hyena_filters(length, w1, b1, w2, b2, w3, b3, freq, wout, *, tl=256):
    fwid = FILTER_WIDTH
    const = lambda i: (0, 0)
    nf = 2 * HYENA_ORDER
    return pl.pallas_call(
        functools.partial(_filter_body, tl=tl, length=length),
        grid=(length // tl,),
        in_specs=[pl.BlockSpec((1, fwid), const), pl.BlockSpec((POS_BANDS, fwid), const),
                  pl.BlockSpec((POS_BANDS, fwid), const), pl.BlockSpec((1, fwid), const),
                  pl.BlockSpec((fwid, fwid), const), pl.BlockSpec((1, fwid), const),
                  pl.BlockSpec((fwid, fwid), const), pl.BlockSpec((1, fwid), const),
                  pl.BlockSpec((1, fwid), const), pl.BlockSpec((fwid, nf * D_MODEL), const)],
        out_specs=pl.BlockSpec((nf, tl, D_MODEL), lambda i: (0, i, 0)),
        out_shape=jax.ShapeDtypeStruct((nf, length, D_MODEL), F32),
        compiler_params=_cp(("parallel",), 32),
        name="hyena_filters",
    )(w1[0:1], w1[1:1 + POS_BANDS], w1[1 + POS_BANDS:], b1, w2, b2, w3, b3, freq, wout)


def _dft_tables(n1, n2s):
    n = n1 * n2s
    m1 = (n1 // 2 + 1 + 7) // 8 * 8
    k1 = jnp.arange(m1, dtype=jnp.int32)
    valid = (k1 <= n1 // 2)
    a = jnp.arange(n1 // 2, dtype=jnp.int32)
    ang_a = (2.0 * math.pi / n1) * ((k1[:, None] * a[None, :]) % n1).astype(F32)
    ca = jnp.where(valid[:, None], jnp.cos(ang_a), 0.0)
    sa = jnp.where(valid[:, None], jnp.sin(ang_a), 0.0)
    g_a = jnp.concatenate([ca, -sa], axis=0)
    g_ai = jnp.concatenate([ca.T, -sa.T], axis=1)
    weight = jnp.where(valid, jnp.where((k1 == 0) | (k1 == n1 // 2), 1.0, 2.0), 0.0)
    b = jnp.arange(n2s, dtype=jnp.int32)
    k2 = jnp.arange(n2s, dtype=jnp.int32)
    idx = (b[None, None, :] * (k2[None, :, None] * n1 + k1[:, None, None])) % n
    ang_c = (2.0 * math.pi / n) * idx.astype(F32)
    cc, sc = jnp.cos(ang_c), jnp.sin(ang_c)
    g_c = jnp.concatenate([jnp.concatenate([cc, sc], axis=2),
                           jnp.concatenate([-sc, cc], axis=2)], axis=1)
    cct = jnp.swapaxes(cc, 1, 2) * (weight / n)[:, None, None]
    sct = jnp.swapaxes(sc, 1, 2) * (weight / n)[:, None, None]
    g_ci = jnp.concatenate([jnp.concatenate([cct, -sct], axis=2),
                            jnp.concatenate([sct, cct], axis=2)], axis=1)
    return m1, g_a.astype(BF16), g_c.astype(BF16), g_ci.astype(BF16), g_ai.astype(BF16)


def _dft_dot(g_ref, x):
    return jnp.dot(g_ref[...], x.astype(BF16), preferred_element_type=F32)


def _dft_a_body(g_ref, x_ref, o_ref, *, nb, m1):
    for bb in range(nb):
        res = _dft_dot(g_ref, x_ref[:, bb, :])
        o_ref[:, 0, bb, :] = res[:m1]
        o_ref[:, 1, bb, :] = res[m1:]


def _dft_slab_body(*refs, bt, spec):
    if spec:
        g1_ref, g2_ref, x_ref, hf_ref, hb_ref, o_ref = refs
        half = g1_ref.shape[0] // 2
        hr = hf_ref[:half] + hb_ref[:half]
        hi = hf_ref[half:] - hb_ref[half:]
    else:
        g1_ref, x_ref, o_ref = refs
    for bi in range(bt):
        xs = _dft_dot(g1_ref, x_ref[bi])
        if spec:
            xr, xi = xs[:half], xs[half:]
            ys = jnp.concatenate([xr * hr - xi * hi, xr * hi + xi * hr], axis=0)
            xs = _dft_dot(g2_ref, ys)
        o_ref[bi] = xs


def _idft_gate_body(g_ref, x_ref, v_ref, gate_ref, bias_ref, o_ref, *, nb):
    for bb in range(nb):
        spec = jnp.concatenate([x_ref[:, 0, bb, :], x_ref[:, 1, bb, :]], axis=0)
        y = _dft_dot(g_ref, spec)
        o_ref[:, bb, :] = gate_ref[:, bb, :] * (y + v_ref[:, bb, :] * bias_ref[...])


def _dft_a(g, x, xsel, *, m1, nb, dt):
    _, bsz, half, n2s, d = x.shape
    return pl.pallas_call(
        functools.partial(_dft_a_body, nb=nb, m1=m1),
        grid=(bsz, n2s // nb, d // dt),
        in_specs=[pl.BlockSpec((2 * m1, half), lambda bi, bj, di: (0, 0)),
                  pl.BlockSpec((None, None, half, nb, dt), lambda bi, bj, di: (xsel, bi, 0, bj, di))],
        out_specs=pl.BlockSpec((None, m1, 2, nb, dt), lambda bi, bj, di: (bi, 0, 0, bj, di)),
        out_shape=jax.ShapeDtypeStruct((bsz, m1, 2, n2s, d), F32),
        compiler_params=_cp(("parallel", "parallel", "parallel"), 48),
        name="dft_a",
    )(g, x)


def _dft_slab(g1, g2, x, hspec, order, *, bt):
    bsz, s, r, d = x.shape
    spec = g2 is not None
    gspec = pl.BlockSpec((None, r, r), lambda si, bi: (si, 0, 0))
    xspec = pl.BlockSpec((bt, None, r, d), lambda si, bi: (bi, si, 0, 0))
    if spec:
        in_specs = [gspec, gspec, xspec,
                    pl.BlockSpec((None, None, r, d), lambda si, bi: (2 * order, si, 0, 0)),
                    pl.BlockSpec((None, None, r, d), lambda si, bi: (2 * order + 1, si, 0, 0))]
        args = (g1, g2, x, hspec, hspec)
    else:
        in_specs, args = [gspec, xspec], (g1, x)
    return pl.pallas_call(
        functools.partial(_dft_slab_body, bt=bt, spec=spec),
        grid=(s, bsz // bt),
        in_specs=in_specs,
        out_specs=xspec,
        out_shape=jax.ShapeDtypeStruct((bsz, s, r, d), F32),
        compiler_params=_cp(("parallel", "parallel"), 48),
        name="dft_spec" if spec else "dft_c",
    )(*args)


def _idft_gate(g, x, v, vsel, gates, gsel, bias, *, nb, dt):
    bsz, m1, _, n2s, d = x.shape
    half = g.shape[0]
    tspec = lambda sel: pl.BlockSpec((None, None, half, nb, dt), lambda bi, bj, di: (sel, bi, 0, bj, di))
    return pl.pallas_call(
        functools.partial(_idft_gate_body, nb=nb),
        grid=(bsz, n2s // nb, d // dt),
        in_specs=[pl.BlockSpec((half, 2 * m1), lambda bi, bj, di: (0, 0)),
                  pl.BlockSpec((None, m1, 2, nb, dt), lambda bi, bj, di: (bi, 0, 0, bj, di)),
                  tspec(vsel), tspec(gsel),
                  pl.BlockSpec((1, dt), lambda bi, bj, di: (0, di))],
        out_specs=pl.BlockSpec((None, half, nb, dt), lambda bi, bj, di: (bi, 0, bj, di)),
        out_shape=jax.ShapeDtypeStruct((bsz, half, n2s, d), F32),
        compiler_params=_cp(("parallel", "parallel", "parallel"), 48),
        name="idft_gate",
    )(g, x, v, gates, bias)


def _fft_plan(length):
    n = 2 * length
    n1 = 256 if n >= 32768 else 64
    return n1, n // n1


def hyena_long_convs(xs, filt, f_bias):
    _, bsz, length, d = xs.shape
    n1, n2s = _fft_plan(length)
    m1, g_a, g_c, g_ci, g_ai = _dft_tables(n1, n2s)
    half = n1 // 2
    big = length >= 8192
    bt = 1 if big else min(8, bsz)
    nb = 8 if big else 16
    dt = d // 2 if big else d
    fa = _dft_a(g_a, filt.reshape(1, 4, half, n2s, d), 0, m1=m1, nb=nb, dt=dt)
    hspec = _dft_slab(g_c, None, fa.reshape(4, m1, 2 * n2s, d), None, 0, bt=1)
    xs6 = xs.reshape(3, bsz, half, n2s, d)
    z, zsel = xs6, 2
    out = None
    for order in range(HYENA_ORDER):
        sa = _dft_a(g_a, z, zsel, m1=m1, nb=nb, dt=dt).reshape(bsz, m1, 2 * n2s, d)
        sb = _dft_slab(g_c, g_ci, sa, hspec, order, bt=bt).reshape(bsz, m1, 2, n2s, d)
        out = _idft_gate(g_ai, sb, z, zsel, xs6, order, f_bias[order:order + 1], nb=nb, dt=dt)
        z, zsel = out[None], 0
    return out.reshape(bsz, length, d)


def _prefix_tables(tm):
    r = jnp.arange(tm, dtype=jnp.int32)
    upper = jnp.where(r[:, None] < r[None, :], 2.0, jnp.where(r[:, None] == r[None, :], 1.0, 0.0))
    return upper.astype(BF16), upper.T.astype(BF16)


def _trunk(x, p, wts):
    bsz, length, d = x.shape
    t = bsz * length
    xf = x.reshape(t, d)
    zeros_d = jnp.zeros((1, d), F32)
    ones_d = jnp.ones((1, d), F32)
    uw, lw = _prefix_tables(MOE_ST)
    for i in range(DEPTH):
        j, kind = i // N_MIXERS, i % N_MIXERS
        if kind == 0:
            lam_init = 0.8 - 0.6 * math.exp(-0.3 * i)
            qkv = linear(xf, wts["attn_w_qkv"][j], jnp.zeros((1, 3 * d), F32), out_dtype=BF16, name="attn_qkv")
            a = diff_attention(qkv.reshape(bsz, length, 3 * d), wts["attn_lam"][j], wts["attn_subln_g"][j],
                               lam_init=lam_init).reshape(t, d)
            w_o, b_o = wts["attn_w_o"][j], zeros_d
        elif kind == 1:
            hglu = linear(xf, wts["conv_w_pw1"][j], wts["conv_b_pw1"][j], out_dtype=F32, glu=True, name="conv_pw1")
            a = dwconv(hglu.reshape(bsz, length, d), wts["conv_w_dw"][j], wts["conv_b_dw"][j],
                       wts["conv_ln_g"][j], wts["conv_ln_b"][j], norm_act=True, out_dtype=BF16).reshape(t, d)
            w_o, b_o = wts["conv_w_pw2"][j], wts["conv_b_pw2"][j]
        else:
            u = linear(xf, wts["hy_w_in"][j], wts["hy_b_in"][j], out_dtype=F32, name="hyena_in")
            xs = dwconv(u.reshape(bsz, length, 3 * d), wts["hy_w_short"][j], wts["hy_b_short"][j],
                        ones_d, zeros_d, norm_act=False, out_dtype=F32)
            filt = hyena_filters(length, wts["hy_f_w1"][j], wts["hy_f_b1"][j], wts["hy_f_w2"][j], wts["hy_f_b2"][j],
                                 wts["hy_f_w3"][j], wts["hy_f_b3"][j], wts["hy_f_freq"][j], wts["hy_f_wout"][j])
            a = hyena_long_convs(xs, filt, wts["hy_f_bias"][j]).reshape(t, d)
            w_o, b_o = wts["hy_w_out"][j], wts["hy_b_out"][j]
        x1, comb = proj_ln_route(a, w_o, b_o, xf, wts["ln1_g"][i], wts["ln1_b"][i],
                                 wts["route_w"][i], wts["route_b"][i])
        xf = moe_ple(x1, comb, uw, lw, wts["moe_w_gate"][i], wts["moe_w_up"][i], wts["moe_w_down"][i],
                     wts["ln2_g"][i], wts["ln2_b"][i], p[i].reshape(t, PLE_DIM),
                     wts["ple_w_up"][i], wts["ple_w_gate"][i])
    return xf.reshape(bsz, length, d)


def kernel(x_prompt, x_sample, p_prompt, p_sample, attn_w_qkv, attn_w_o, attn_lam_q1, attn_lam_k1, attn_lam_q2, attn_lam_k2, attn_subln_g, conv_w_pw1, conv_b_pw1, conv_w_dw, conv_b_dw, conv_ln_g, conv_ln_b, conv_w_pw2, conv_b_pw2, hy_w_in, hy_b_in, hy_w_short, hy_b_short, hy_f_w1, hy_f_b1, hy_f_w2, hy_f_b2, hy_f_w3, hy_f_b3, hy_f_freq, hy_f_wout, hy_f_bias, hy_w_out, hy_b_out, ln1_g, ln1_b, ln2_g, ln2_b, moe_w_group, moe_b_group, moe_w_expert, moe_b_expert, moe_w_gate, moe_w_up, moe_w_down, ple_w_up, ple_w_gate):
    d = D_MODEL
    row = lambda a: a[:, None, :]
    q_scale = jnp.concatenate([jnp.full((d,), HEAD_DIM ** -0.5 * LOG2E, F32), jnp.ones((2 * d,), F32)])
    route_w = jnp.concatenate([jnp.swapaxes(moe_w_group, 1, 2), jnp.swapaxes(moe_w_expert, 1, 2),
                               jnp.zeros((DEPTH, ROUTE_ROWS - N_GROUPS - N_EXPERTS, d), F32)], axis=1)
    route_b = jnp.concatenate([moe_b_group, moe_b_expert,
                               jnp.zeros((DEPTH, ROUTE_ROWS - N_GROUPS - N_EXPERTS), F32)], axis=1)[:, :, None]
    wts = {
        "attn_w_qkv": (attn_w_qkv * q_scale).astype(BF16),
        "attn_w_o": attn_w_o.astype(BF16),
        "attn_lam": jnp.stack([attn_lam_q1, attn_lam_k1, attn_lam_q2, attn_lam_k2], axis=1),
        "attn_subln_g": attn_subln_g[:, :, None],
        "conv_w_pw1": conv_w_pw1.astype(BF16), "conv_b_pw1": row(conv_b_pw1),
        "conv_w_dw": conv_w_dw, "conv_b_dw": row(conv_b_dw),
        "conv_ln_g": row(conv_ln_g), "conv_ln_b": row(conv_ln_b),
        "conv_w_pw2": conv_w_pw2.astype(BF16), "conv_b_pw2": row(conv_b_pw2),
        "hy_w_in": hy_w_in.astype(BF16), "hy_b_in": row(hy_b_in),
        "hy_w_short": hy_w_short, "hy_b_short": row(hy_b_short),
        "hy_f_w1": hy_f_w1, "hy_f_b1": row(hy_f_b1), "hy_f_w2": hy_f_w2, "hy_f_b2": row(hy_f_b2),
        "hy_f_w3": hy_f_w3, "hy_f_b3": row(hy_f_b3), "hy_f_freq": row(hy_f_freq), "hy_f_wout": hy_f_wout,
        "hy_f_bias": hy_f_bias,
        "hy_w_out": hy_w_out.astype(BF16), "hy_b_out": row(hy_b_out),
        "ln1_g": row(ln1_g), "ln1_b": row(ln1_b), "ln2_g": row(ln2_g), "ln2_b": row(ln2_b),
        "route_w": route_w, "route_b": route_b,
        "moe_w_gate": moe_w_gate.astype(BF16), "moe_w_up": moe_w_up.astype(BF16),
        "moe_w_down": moe_w_down.astype(BF16),
        "ple_w_up": ple_w_up.astype(BF16), "ple_w_gate": ple_w_gate.astype(BF16),
    }
    y_prompt = _trunk(x_prompt, p_prompt, wts)
    y_sample = _trunk(x_sample, p_sample, wts)
    return (y_prompt, y_sample)
```

```python
import functools
import math

import jax
import jax.numpy as jnp
from jax import lax
from jax.experimental import pallas as pl
from jax.experimental.pallas import tpu as pltpu

F32 = jnp.float32
BF16 = jnp.bfloat16
HIGHEST = lax.Precision.HIGHEST

D_MODEL = 1024
DEPTH = 4
N_MIXERS = 3
N_HEADS = 8
HEAD_DIM = 64
CONV_WIDTH = 31
HYENA_ORDER = 2
SHORT_WIDTH = 3
POS_BANDS = 16
FILTER_WIDTH = 64
MAX_DECAY = math.log(1e-2) / 0.3
MIN_DECAY = math.log(1e-2) / 1.5
N_GROUPS = 4
EXPERTS_PER_GROUP = 4
N_EXPERTS = N_GROUPS * EXPERTS_PER_GROUP
EXPERT_FF = 512
PLE_DIM = 256
ALPHA = (2 * DEPTH) ** 0.25
LN_EPS = 1e-5
LOG2E = 1.4426950408889634

HALO = 16
CONV_ROWS = 16
ROUTE_ROWS = 24
MOE_TM = 1024
MOE_ST = 256
MOE_CAPS = 64
ATT_TQ = 512
ATT_TK = 1024
ATT_CW = 256
ATT_FILL = 512
NT_DIMS = (((1,), (1,)), ((), ()))


def _cp(sem, vmem_mb):
    return pltpu.CompilerParams(dimension_semantics=sem, vmem_limit_bytes=vmem_mb << 20)


def _layer_norm(z, g, b):
    mu = jnp.mean(z, axis=-1, keepdims=True)
    zc = z - mu
    var = jnp.mean(zc * zc, axis=-1, keepdims=True)
    return zc * lax.rsqrt(var + LN_EPS) * g + b


def _sigmoid(x):
    return 1.0 / (1.0 + jnp.exp(-x))


def _linear_body(x_ref, w_ref, b_ref, o_ref, *, nc, glu):
    x = x_ref[...].astype(BF16)
    n_out = o_ref.shape[-1]
    for n0 in range(0, n_out, nc):
        a = jnp.dot(x, w_ref[:, n0:n0 + nc], preferred_element_type=F32) + b_ref[:, n0:n0 + nc]
        if glu:
            g = (jnp.dot(x, w_ref[:, n_out + n0:n_out + n0 + nc], preferred_element_type=F32)
                 + b_ref[:, n_out + n0:n_out + n0 + nc])
            a = a * _sigmoid(g)
        o_ref[:, n0:n0 + nc] = a.astype(o_ref.dtype)


def linear(x, w, b, *, out_dtype, glu=False, tm=512, nc=512, name="linear"):
    t, k = x.shape
    n = w.shape[1]
    n_out = n // 2 if glu else n
    return pl.pallas_call(
        functools.partial(_linear_body, nc=nc, glu=glu),
        grid=(t // tm,),
        in_specs=[pl.BlockSpec((tm, k), lambda i: (i, 0)),
                  pl.BlockSpec((k, n), lambda i: (0, 0)),
                  pl.BlockSpec((1, n), lambda i: (0, 0))],
        out_specs=pl.BlockSpec((tm, n_out), lambda i: (i, 0)),
        out_shape=jax.ShapeDtypeStruct((t, n_out), out_dtype),
        compiler_params=_cp(("parallel",), 48),
        name=name,
    )(x, w, b)


def _split3(x):
    hi = x.astype(BF16).astype(F32)
    mid = (x - hi).astype(BF16).astype(F32)
    lo = (x - hi - mid).astype(BF16).astype(F32)
    return [hi, mid, lo]


def _lane_table(lane, values, first):
    out = jnp.zeros(lane.shape, F32)
    for n, val in enumerate(values):
        out = jnp.where(lane == first + n, val, out)
    return out


def _attn_body(lam_ref, g_ref, q_ref, k_ref, v_ref, o_ref, kp_scr, vt_scr, d0_scr, qq_scr, s0_scr, s1_scr,
               m_scr, acc_scr, *, tq, tk, lam_init):
    h = pl.program_id(1)
    qi = pl.program_id(2)
    length = k_ref.shape[0]
    nk = length // tk
    n_other = nk - 1
    hd2 = 2 * HEAD_DIM
    slope2 = jnp.exp2(-(jnp.full((1, 1), h, jnp.int32) + 1).astype(F32)) * LOG2E
    c1 = _split3(slope2)
    c128 = [128.0 * c for c in c1]

    @pl.when(qi == 0)
    def _():
        d0_scr[...] = (lax.broadcasted_iota(jnp.int32, (tk, tq), 1)
                       - lax.broadcasted_iota(jnp.int32, (tk, tq), 0)).astype(F32)
        lane = lax.broadcasted_iota(jnp.int32, (ATT_FILL, hd2), 1)
        consts = _lane_table(lane, [-c for c in c128] + [-c for c in c1], 0)
        eye = jnp.where(lax.broadcasted_iota(jnp.int32, (hd2, hd2), 0)
                        == lax.broadcasted_iota(jnp.int32, (hd2, hd2), 1), 1.0, 0.0).astype(BF16)

        def fill(ci, carry):
            r0 = pl.multiple_of(ci * ATT_FILL, ATT_FILL)
            pos = r0 + lax.broadcasted_iota(jnp.int32, (ATT_FILL, hd2), 0)
            hi_digit = jnp.right_shift(pos, 7).astype(F32)
            lo_digit = jnp.bitwise_and(pos, 127).astype(F32)
            aug = jnp.where(lane < 6, consts, jnp.where(lane < 9, hi_digit, jnp.where(lane < 12, lo_digit, 0.0)))
            kp_scr[pl.ds(r0, ATT_FILL), 0:hd2] = k_ref[pl.ds(r0, ATT_FILL), :]
            kp_scr[pl.ds(r0, ATT_FILL), hd2:2 * hd2] = aug.astype(BF16)
            vt_scr[0:hd2, pl.ds(r0, ATT_FILL)] = lax.dot_general(
                eye, v_ref[pl.ds(r0, ATT_FILL), :], NT_DIMS, preferred_element_type=F32).astype(BF16)
            vt_scr[hd2:hd2 + 16, pl.ds(r0, ATT_FILL)] = jnp.where(
                lax.broadcasted_iota(jnp.int32, (16, ATT_FILL), 0) == 0, 1.0, 0.0).astype(BF16)
            return carry

        lax.fori_loop(0, length // ATT_FILL, fill, 0)

    q = q_ref[...]
    lane = lax.broadcasted_iota(jnp.int32, (tq, hd2), 1)
    zero = jnp.zeros_like(q)
    tpos = qi * tq + lax.broadcasted_iota(jnp.int32, (tq, hd2), 0)
    hi_digit = jnp.right_shift(tpos, 7).astype(F32)
    lo_digit = jnp.bitwise_and(tpos, 127).astype(F32)
    consts = _lane_table(lane, c128 + c1, 6)
    augq = jnp.where(lane < 3, hi_digit, jnp.where(lane < 6, lo_digit, consts))
    for var, aug in enumerate((augq.astype(BF16), (-augq).astype(BF16))):
        qq_scr[var, 0:tq, 0:hd2] = jnp.where(lane < HEAD_DIM, q, zero)
        qq_scr[var, tq:2 * tq, 0:hd2] = jnp.where(lane >= HEAD_DIM, q, zero)
        qq_scr[var, 0:tq, hd2:2 * hd2] = aug
        qq_scr[var, tq:2 * tq, hd2:2 * hd2] = aug

    m_scr[...] = jnp.full(m_scr.shape, -jnp.inf, F32)
    acc_scr[...] = jnp.zeros(acc_scr.shape, F32)

    kd = (qi * tq) // tk

    def other_block(n):
        return jnp.where(n < n_other, jnp.where(n < kd, n, n + 1), kd)

    def scores(blk, s_scr):
        ks = pl.multiple_of(blk * tk, tk)
        var = (blk > kd).astype(jnp.int32)
        s_scr[...] = lax.dot_general(kp_scr[pl.ds(ks, tk), :], qq_scr[var], NT_DIMS, preferred_element_type=F32)

    def softmax_pv(blk, s_scr, diag):
        ks = pl.multiple_of(blk * tk, tk)
        cdiag = (qi * tq - blk * tk).astype(F32)
        vt = vt_scr[:, pl.ds(ks, tk)]
        m_all = m_scr[...]
        m_out = []
        for cg in range(2 * tq // ATT_CW):
            cols = slice(cg * ATT_CW, (cg + 1) * ATT_CW)
            s = s_scr[:, cols]
            if diag:
                dc = (cg * ATT_CW) % tq
                s = s + jnp.minimum(d0_scr[:, dc:dc + ATT_CW] + cdiag, 0.0) * (2.0 * slope2)
            m_old = m_all[:, cols]
            m_new = jnp.maximum(m_old, jnp.max(s, axis=0, keepdims=True))
            p = jnp.exp2((s - m_new).astype(BF16))
            alpha = jnp.exp2(m_old - m_new)
            m_out.append(m_new)
            pv = jnp.dot(vt, p, preferred_element_type=F32)
            acc_scr[:, cols] = alpha * acc_scr[:, cols] + pv
        m_scr[...] = jnp.concatenate(m_out, axis=1)

    scores(other_block(0), s0_scr)

    def pair(n2, carry):
        n = 2 * n2
        scores(other_block(n + 1), s1_scr)
        softmax_pv(other_block(n), s0_scr, False)
        scores(other_block(n + 2), s0_scr)
        softmax_pv(other_block(n + 1), s1_scr, False)
        return carry

    lax.fori_loop(0, n_other // 2, pair, 0)
    if n_other % 2:
        scores(kd, s1_scr)
        softmax_pv(other_block(n_other - 1), s0_scr, False)
        softmax_pv(kd, s1_scr, True)
    else:
        softmax_pv(kd, s0_scr, True)

    lam = lam_ref[...]
    lam_full = (jnp.exp(jnp.sum(lam[0:1] * lam[1:2], axis=-1, keepdims=True))
                - jnp.exp(jnp.sum(lam[2:3] * lam[3:4], axis=-1, keepdims=True)) + lam_init)
    ot = acc_scr[0:hd2, :] / acc_scr[hd2:hd2 + 1, :]
    ot = ot[:, :tq] - lam_full * ot[:, tq:]
    ms = jnp.mean(ot * ot, axis=0, keepdims=True)
    ot = ot * lax.rsqrt(ms + LN_EPS) * g_ref[...] * (1.0 - lam_init)
    o_ref[...] = jnp.transpose(ot).astype(o_ref.dtype)


def diff_attention(qkv, lam, subln_g, *, lam_init, tq=ATT_TQ, tk=ATT_TK):
    b, l, _ = qkv.shape
    assert l % tk == 0 and l // tk >= 2 and tk % tq == 0 and l <= 128 * 128
    hd2 = 2 * HEAD_DIM
    return pl.pallas_call(
        functools.partial(_attn_body, tq=tq, tk=tk, lam_init=lam_init),
        grid=(b, N_HEADS, l // tq),
        in_specs=[pl.BlockSpec((4, HEAD_DIM), lambda bi, h, qi: (0, 0)),
                  pl.BlockSpec((hd2, 1), lambda bi, h, qi: (0, 0)),
                  pl.BlockSpec((None, tq, hd2), lambda bi, h, qi: (bi, qi, h)),
                  pl.BlockSpec((None, l, hd2), lambda bi, h, qi: (bi, 0, N_HEADS + h)),
                  pl.BlockSpec((None, l, hd2), lambda bi, h, qi: (bi, 0, 2 * N_HEADS + h))],
        out_specs=pl.BlockSpec((None, tq, hd2), lambda bi, h, qi: (bi, qi, h)),
        out_shape=jax.ShapeDtypeStruct((b, l, D_MODEL), BF16),
        scratch_shapes=[pltpu.VMEM((l, 2 * hd2), BF16), pltpu.VMEM((hd2 + 16, l), BF16), pltpu.VMEM((tk, tq), F32),
                        pltpu.VMEM((2, 2 * tq, 2 * hd2), BF16),
                        pltpu.VMEM((tk, 2 * tq), F32), pltpu.VMEM((tk, 2 * tq), F32),
                        pltpu.VMEM((1, 2 * tq), F32),
                        pltpu.VMEM((hd2 + 16, 2 * tq), F32)],
        compiler_params=_cp(("parallel", "parallel", "arbitrary"), 56),
        name="diff_attention",
    )(lam, subln_g, qkv, qkv, qkv)


def _route(x1, wr, br, comb_ref):
    lt = lax.dot_general(wr, x1, NT_DIMS, precision=HIGHEST, preferred_element_type=F32) + br
    gl = [lt[g:g + 1] for g in range(N_GROUPS)]
    gmax = jnp.maximum(jnp.maximum(gl[0], gl[1]), jnp.maximum(gl[2], gl[3]))
    gidx = jnp.where(gl[0] == gmax, 0, jnp.where(gl[1] == gmax, 1, jnp.where(gl[2] == gmax, 2, 3)))
    gw = 1.0 / (jnp.exp(gl[0] - gmax) + jnp.exp(gl[1] - gmax) + jnp.exp(gl[2] - gmax) + jnp.exp(gl[3] - gmax))
    el = []
    for j in range(EXPERTS_PER_GROUP):
        acc = jnp.zeros_like(gmax)
        for g in range(N_GROUPS):
            r = N_GROUPS + g * EXPERTS_PER_GROUP + j
            acc = jnp.where(gidx == g, lt[r:r + 1], acc)
        el.append(acc)
    v1 = jnp.maximum(jnp.maximum(el[0], el[1]), jnp.maximum(el[2], el[3]))
    i1 = jnp.where(el[0] == v1, 0, jnp.where(el[1] == v1, 1, jnp.where(el[2] == v1, 2, 3)))
    neg = jnp.full_like(v1, -jnp.inf)
    el2 = [jnp.where(i1 == j, neg, el[j]) for j in range(EXPERTS_PER_GROUP)]
    v2 = jnp.maximum(jnp.maximum(el2[0], el2[1]), jnp.maximum(el2[2], el2[3]))
    i2 = jnp.where(el2[0] == v2, 0, jnp.where(el2[1] == v2, 1, jnp.where(el2[2] == v2, 2, 3)))
    e21 = jnp.exp(v2 - v1)
    w1 = gw / (1.0 + e21)
    w2 = gw * e21 / (1.0 + e21)
    zero = jnp.zeros_like(v1)
    for g in range(N_GROUPS):
        for j in range(EXPERTS_PER_GROUP):
            wj = jnp.where(i1 == j, w1, jnp.where(i2 == j, w2, zero))
            r = g * EXPERTS_PER_GROUP + j
            comb_ref[r:r + 1, :] = jnp.where(gidx == g, wj, zero)


def _proj_ln_route_body(a_ref, w_ref, b_ref, x_ref, g_ref, beta_ref, wr_ref, br_ref, x1_ref, comb_ref):
    h = jnp.dot(a_ref[...].astype(BF16), w_ref[...], preferred_element_type=F32) + b_ref[...]
    x1 = _layer_norm(ALPHA * x_ref[...] + h, g_ref[...], beta_ref[...])
    x1_ref[...] = x1
    _route(x1, wr_ref[...], br_ref[...], comb_ref)


def proj_ln_route(a, w, b, x, ln_g, ln_b, wr, br, *, tm=512):
    t, k = a.shape
    d = D_MODEL
    const = lambda i: (0, 0)
    return pl.pallas_call(
        _proj_ln_route_body,
        grid=(t // tm,),
        in_specs=[pl.BlockSpec((tm, k), lambda i: (i, 0)),
                  pl.BlockSpec((k, d), const), pl.BlockSpec((1, d), const),
                  pl.BlockSpec((tm, d), lambda i: (i, 0)),
                  pl.BlockSpec((1, d), const), pl.BlockSpec((1, d), const),
                  pl.BlockSpec((ROUTE_ROWS, d), const), pl.BlockSpec((ROUTE_ROWS, 1), const)],
        out_specs=[pl.BlockSpec((tm, d), lambda i: (i, 0)),
                   pl.BlockSpec((N_EXPERTS, tm), lambda i: (0, i))],
        out_shape=[jax.ShapeDtypeStruct((t, d), F32), jax.ShapeDtypeStruct((N_EXPERTS, t), F32)],
        compiler_params=_cp(("parallel",), 48),
        name="proj_ln_route",
    )(a, w, b, x, ln_g, ln_b, wr, br)


def _split_hi_lo(x):
    hi = x.astype(BF16)
    return hi, (x - hi.astype(F32)).astype(BF16)


def _moe_body(x1_ref, comb_ref, uw_ref, lw_ref, eye_ref, wg_ref, wu_ref, wd_ref, g2_ref, b2_ref, p_ref, up_ref,
              gate_ref, o_ref, vrow_scr, vcol_scr, ccol_scr, xg_scr, fast_scr, *, tm, st, caps):
    e = pl.program_id(1)
    j = pl.program_id(2)
    n_sub = tm // st

    def cols(sub):
        return slice(sub * st, (sub + 1) * st)

    def rows(sub):
        return pl.ds(pl.multiple_of(j * tm + sub * st, st), st)

    @pl.when(e == 0)
    def _():
        comb = comb_ref[:, pl.ds(pl.multiple_of(j * tm, tm), tm)]
        member = jnp.where(comb > 0.0, 1.0, 0.0).astype(BF16)
        comb3 = [c.astype(BF16) for c in _split3(comb)]
        for sub in range(n_sub):
            ms = member[:, cols(sub)]
            vrow_scr[j, :, cols(sub)] = jnp.dot(ms, uw_ref[...], preferred_element_type=F32)
            vcol_scr[j, cols(sub), :] = lax.dot_general(lw_ref[...], ms, NT_DIMS, preferred_element_type=F32)
            ccol_scr[j, cols(sub), :] = sum(
                lax.dot_general(eye_ref[...], c[:, cols(sub)], NT_DIMS, preferred_element_type=F32)
                for c in comb3)
        most = ((jnp.max(vrow_scr[j]) + 1.0) * 0.5).astype(jnp.int32)
        fast_scr[j] = (most <= caps).astype(jnp.int32)

        @pl.when(most <= caps)
        def _():
            tgt = 2.0 * lax.broadcasted_iota(jnp.int32, (caps, 1), 0).astype(F32) + 1.0
            for sub in range(n_sub):
                vr = vrow_scr[j, :, cols(sub)]
                gather = jnp.concatenate(
                    [jnp.where(vr[ee:ee + 1, :] == tgt, 1.0, 0.0).astype(BF16) for ee in range(N_EXPERTS)],
                    axis=0)
                xg = jnp.dot(gather, x1_ref[rows(sub), :].astype(BF16),
                             preferred_element_type=F32).astype(BF16)
                for ee in range(N_EXPERTS):
                    xg_scr[j, ee, sub * caps:(sub + 1) * caps, :] = xg[ee * caps:(ee + 1) * caps]

        @pl.when(most > caps)
        def _():
            for sub in range(n_sub):
                o_ref[rows(sub), :] = jnp.zeros((st, o_ref.shape[1]), F32)

    fast = fast_scr[j] == 1

    def expert_ffn(xg):
        hg = jnp.dot(xg, wg_ref[...], preferred_element_type=F32)
        hu = jnp.dot(xg, wu_ref[...], preferred_element_type=F32)
        hid = hg * _sigmoid(hg) * hu
        return jnp.dot(hid.astype(BF16), wd_ref[...], preferred_element_type=F32).astype(BF16)

    @pl.when(fast)
    def _():
        xg_scr[j, e] = expert_ffn(xg_scr[j, e])

    @pl.when(jnp.logical_not(fast))
    def _():
        sel = lax.broadcasted_iota(jnp.int32, (st, N_EXPERTS), 1) == e
        for sub in range(n_sub):
            vrow = vrow_scr[j, pl.ds(e, 1), cols(sub)]
            vcol = jnp.sum(jnp.where(sel, vcol_scr[j, cols(sub), :], 0.0), axis=1, keepdims=True)
            wcol = jnp.sum(jnp.where(sel, ccol_scr[j, cols(sub), :], 0.0), axis=1, keepdims=True)
            count = ((jnp.max(vrow) + 1.0) * 0.5).astype(jnp.int32)
            xsub = x1_ref[rows(sub), :].astype(BF16)

            def chunk(c, carry):
                base = (c * caps).astype(F32)
                tgt_r = 2.0 * (base + lax.broadcasted_iota(jnp.int32, (caps, 1), 0).astype(F32)) + 1.0
                tgt_c = 2.0 * (base + lax.broadcasted_iota(jnp.int32, (1, caps), 1).astype(F32)) + 1.0
                gather = jnp.where(vrow == tgt_r, 1.0, 0.0).astype(BF16)
                scatter = _split3(jnp.where(vcol == tgt_c, wcol, 0.0))
                ob = expert_ffn(jnp.dot(gather, xsub, preferred_element_type=F32).astype(BF16))
                o_ref[rows(sub), :] += sum(jnp.dot(s.astype(BF16), ob, preferred_element_type=F32)
                                           for s in scatter)
                return carry

            lax.fori_loop(0, (count + (caps - 1)) // caps, chunk, 0)

    @pl.when(e == N_EXPERTS - 1)
    def _():
        @pl.when(fast)
        def _():
            lane = lax.broadcasted_iota(jnp.int32, (st, 2 * caps), 1)
            tgt = 2.0 * jnp.where(lane < caps, lane, lane - caps).astype(F32) + 1.0
            for sub in range(n_sub):
                vc = vcol_scr[j, cols(sub), :]
                cc = ccol_scr[j, cols(sub), :]
                pieces = []
                for pr in range(N_EXPERTS // 2):
                    val = jnp.where(lane < caps, vc[:, 2 * pr:2 * pr + 1], vc[:, 2 * pr + 1:2 * pr + 2])
                    wgt = jnp.where(lane < caps, cc[:, 2 * pr:2 * pr + 1], cc[:, 2 * pr + 1:2 * pr + 2])
                    pieces.append(jnp.where(val == tgt, wgt, 0.0))
                s_hi, s_lo = _split_hi_lo(jnp.concatenate(pieces, axis=1))
                outs = jnp.concatenate([xg_scr[j, ee, sub * caps:(sub + 1) * caps, :] for ee in range(N_EXPERTS)],
                                       axis=0)
                o_ref[rows(sub), :] = (jnp.dot(s_hi, outs, preferred_element_type=F32)
                                       + jnp.dot(s_lo, outs, preferred_element_type=F32))

        for sub in range(n_sub):
            x2 = _layer_norm(ALPHA * x1_ref[rows(sub), :] + o_ref[rows(sub), :], g2_ref[...], b2_ref[...])
            up = jnp.dot(p_ref[rows(sub), :].astype(BF16), up_ref[...], preferred_element_type=F32)
            gt = jnp.dot(x2.astype(BF16), gate_ref[...], preferred_element_type=F32)
            o_ref[rows(sub), :] = x2 + up * _sigmoid(gt)


def moe_ple(x1, comb, tables, wg, wu, wd, ln_g, ln_b, p, up, gate, *, tm=MOE_TM, st=MOE_ST, caps=MOE_CAPS):
    t, d = x1.shape
    ff = EXPERT_FF
    n_sub = tm // st
    assert 2 * caps == 128 and st % 128 == 0 and t % (2 * tm) == 0
    const = lambda i, e, j: (0, 0)
    pair = lambda i, e, j: (i, 0)
    expert = lambda i, e, j: (e, 0, 0)
    once = pl.Buffered(1)
    return pl.pallas_call(
        functools.partial(_moe_body, tm=tm, st=st, caps=caps),
        grid=(t // (2 * tm), N_EXPERTS, 2),
        in_specs=[pl.BlockSpec((2 * tm, d), pair, pipeline_mode=once),
                  pl.BlockSpec((N_EXPERTS, 2 * tm), lambda i, e, j: (0, i)),
                  pl.BlockSpec((st, st), const), pl.BlockSpec((st, st), const), pl.BlockSpec((st, st), const),
                  pl.BlockSpec((None, d, ff), expert), pl.BlockSpec((None, d, ff), expert),
                  pl.BlockSpec((None, ff, d), expert),
                  pl.BlockSpec((1, d), const), pl.BlockSpec((1, d), const),
                  pl.BlockSpec((2 * tm, PLE_DIM), pair, pipeline_mode=once),
                  pl.BlockSpec((PLE_DIM, d), const, pipeline_mode=once),
                  pl.BlockSpec((d, d), const, pipeline_mode=once)],
        out_specs=pl.BlockSpec((2 * tm, d), pair, pipeline_mode=once),
        out_shape=jax.ShapeDtypeStruct((t, d), F32),
        scratch_shapes=[pltpu.VMEM((2, N_EXPERTS, tm), F32), pltpu.VMEM((2, tm, N_EXPERTS), F32),
                        pltpu.VMEM((2, tm, N_EXPERTS), F32),
                        pltpu.VMEM((2, N_EXPERTS, n_sub * caps, d), BF16),
                        pltpu.SMEM((2,), jnp.int32)],
        compiler_params=_cp(("parallel", "arbitrary", "arbitrary"), 58),
        name="moe_ple",
    )(x1, comb, *tables, wg, wu, wd, ln_g, ln_b, p, up, gate)


def _dwconv_body(prev_ref, cur_ref, next_ref, w_ref, b_ref, g_ref, beta_ref, o_ref, buf_scr, sh_scr,
                 *, width, tm, norm_act):
    i = pl.program_id(1)
    n = pl.num_programs(1)
    pad = width // 2
    halo = jnp.zeros(prev_ref.shape, F32)
    buf_scr[0:HALO, :] = jnp.where(i > 0, prev_ref[...], halo)
    buf_scr[HALO:HALO + tm, :] = cur_ref[...]
    buf_scr[HALO + tm:2 * HALO + tm, :] = jnp.where(i < n - 1, next_ref[...], halo)
    offs = [HALO - pad + k for k in range(width)]
    n_sh = tm + 2 * HALO - 8
    for sft in sorted({off % 8 for off in offs}):
        sh_scr[sft, 0:n_sh, :] = buf_scr[sft:sft + n_sh, :]

    def rows(r, carry):
        r0 = pl.multiple_of(r * CONV_ROWS, CONV_ROWS)
        acc = jnp.zeros((CONV_ROWS, o_ref.shape[1]), F32) + b_ref[...]
        for k, off in enumerate(offs):
            start = pl.multiple_of(r0 + (off // 8) * 8, 8)
            acc = acc + sh_scr[off % 8, pl.ds(start, CONV_ROWS), :] * w_ref[k:k + 1, :]
        if norm_act:
            acc = _layer_norm(acc, g_ref[...], beta_ref[...])
            acc = acc * _sigmoid(acc)
        o_ref[pl.ds(r0, CONV_ROWS), :] = acc.astype(o_ref.dtype)
        return carry

    lax.fori_loop(0, tm // CONV_ROWS, rows, 0)


def dwconv(x, w, b, ln_g, ln_b, *, norm_act, out_dtype, tm=256):
    bsz, l, c = x.shape
    width = w.shape[0]
    tc = D_MODEL
    nb = tm // HALO
    last = l // HALO - 1
    return pl.pallas_call(
        functools.partial(_dwconv_body, width=width, tm=tm, norm_act=norm_act),
        grid=(bsz, l // tm, c // tc),
        in_specs=[pl.BlockSpec((None, HALO, tc), lambda bi, i, j: (bi, jnp.maximum(i * nb - 1, 0), j)),
                  pl.BlockSpec((None, tm, tc), lambda bi, i, j: (bi, i, j)),
                  pl.BlockSpec((None, HALO, tc), lambda bi, i, j: (bi, jnp.minimum((i + 1) * nb, last), j)),
                  pl.BlockSpec((width, tc), lambda bi, i, j: (0, j)),
                  pl.BlockSpec((1, tc), lambda bi, i, j: (0, j)),
                  pl.BlockSpec((1, tc), lambda bi, i, j: (0, 0)),
                  pl.BlockSpec((1, tc), lambda bi, i, j: (0, 0))],
        out_specs=pl.BlockSpec((None, None, tm, tc), lambda bi, i, j: (j, bi, i, 0)),
        out_shape=jax.ShapeDtypeStruct((c // tc, bsz, l, tc), out_dtype),
        scratch_shapes=[pltpu.VMEM((tm + 2 * HALO, tc), F32), pltpu.VMEM((8, tm + 2 * HALO, tc), F32)],
        compiler_params=_cp(("parallel", "parallel", "parallel"), 32),
        name="dwconv%d" % width,
    )(x, x, x, w, b, ln_g, ln_b)


def _filter_body(w1t_ref, w1c_ref, w1s_ref, b1_ref, w2_ref, b2_ref, w3_ref, b3_ref, freq_ref, wout_ref,
                 o_ref, *, tl, length):
    i = pl.program_id(0)
    pos = (i * tl + lax.broadcasted_iota(jnp.int32, (tl, 1), 0)).astype(F32)
    t = pos / float(length - 1)
    ang = (2.0 * math.pi / length) * pos
    band = lax.broadcasted_iota(jnp.int32, (1, POS_BANDS), 1).astype(F32)
    bands = 1e-4 + band * ((POS_BANDS - 1 - 1e-4) / (POS_BANDS - 1))
    fw = bands * ang
    freq = freq_ref[...]
    dot = functools.partial(jnp.dot, precision=HIGHEST, preferred_element_type=F32)
    pre = t * w1t_ref[...] + dot(jnp.cos(fw), w1c_ref[...]) - dot(jnp.sin(fw), w1s_ref[...]) + b1_ref[...]
    hcur = jnp.sin(freq * pre)
    hcur = jnp.sin(freq * (dot(hcur, w2_ref[...]) + b2_ref[...]))
    hcur = jnp.sin(freq * (dot(hcur, w3_ref[...]) + b3_ref[...]))
    ch = lax.broadcasted_iota(jnp.int32, (1, D_MODEL), 1).astype(F32)
    deltas = jnp.abs(MIN_DECAY + ch * ((MAX_DECAY - MIN_DECAY) / (D_MODEL - 1)))
    decay = jnp.exp(-t * deltas)
    for j in range(2 * HYENA_ORDER):
        o_ref[j] = dot(hcur, wout_ref[:, j * D_MODEL:(j + 1) * D_MODEL]) * decay


def hyena_filters(length, w1, b1, w2, b2, w3, b3, freq, wout, *, tl=256):
    fwid = FILTER_WIDTH
    const = lambda i: (0, 0)
    nf = 2 * HYENA_ORDER
    return pl.pallas_call(
        functools.partial(_filter_body, tl=tl, length=length),
        grid=(length // tl,),
        in_specs=[pl.BlockSpec((1, fwid), const), pl.BlockSpec((POS_BANDS, fwid), const),
                  pl.BlockSpec((POS_BANDS, fwid), const), pl.BlockSpec((1, fwid), const),
                  pl.BlockSpec((fwid, fwid), const), pl.BlockSpec((1, fwid), const),
                  pl.BlockSpec((fwid, fwid), const), pl.BlockSpec((1, fwid), const),
                  pl.BlockSpec((1, fwid), const), pl.BlockSpec((fwid, nf * D_MODEL), const)],
        out_specs=pl.BlockSpec((nf, tl, D_MODEL), lambda i: (0, i, 0)),
        out_shape=jax.ShapeDtypeStruct((nf, length, D_MODEL), F32),
        compiler_params=_cp(("parallel",), 32),
        name="hyena_filters",
    )(w1[0:1], w1[1:1 + POS_BANDS], w1[1 + POS_BANDS:], b1, w2, b2, w3, b3, freq, wout)


def _dft_tables(n1, n2s):
    n = n1 * n2s
    m1 = (n1 // 2 + 1 + 7) // 8 * 8
    k1 = jnp.arange(m1, dtype=jnp.int32)
    valid = (k1 <= n1 // 2)
    a = jnp.arange(n1 // 2, dtype=jnp.int32)
    ang_a = (2.0 * math.pi / n1) * ((k1[:, None] * a[None, :]) % n1).astype(F32)
    ca = jnp.where(valid[:, None], jnp.cos(ang_a), 0.0)
    sa = jnp.where(valid[:, None], jnp.sin(ang_a), 0.0)
    g_a = jnp.concatenate([ca, -sa], axis=0)
    g_ai = jnp.concatenate([ca.T, -sa.T], axis=1)
    weight = jnp.where(valid, jnp.where((k1 == 0) | (k1 == n1 // 2), 1.0, 2.0), 0.0)
    b = jnp.arange(n2s, dtype=jnp.int32)
    k2 = jnp.arange(n2s, dtype=jnp.int32)
    idx = (b[None, None, :] * (k2[None, :, None] * n1 + k1[:, None, None])) % n
    ang_c = (2.0 * math.pi / n) * idx.astype(F32)
    cc, sc = jnp.cos(ang_c), jnp.sin(ang_c)
    g_c = jnp.concatenate([jnp.concatenate([cc, sc], axis=2),
                           jnp.concatenate([-sc, cc], axis=2)], axis=1)
    cct = jnp.swapaxes(cc, 1, 2) * (weight / n)[:, None, None]
    sct = jnp.swapaxes(sc, 1, 2) * (weight / n)[:, None, None]
    g_ci = jnp.concatenate([jnp.concatenate([cct, -sct], axis=2),
                            jnp.concatenate([sct, cct], axis=2)], axis=1)
    return m1, g_a.astype(BF16), g_c.astype(BF16), g_ci.astype(BF16), g_ai.astype(BF16)


def _dft_dot(g_ref, x):
    return jnp.dot(g_ref[...], x.astype(BF16), preferred_element_type=F32)


def _dft_a_body(g_ref, x_ref, o_ref, *, nb, m1):
    for bb in range(nb):
        res = _dft_dot(g_ref, x_ref[:, bb, :])
        o_ref[:, 0, bb, :] = res[:m1]
        o_ref[:, 1, bb, :] = res[m1:]


def _dft_slab_body(*refs, bt, spec):
    if spec:
        g1_ref, g2_ref, x_ref, hf_ref, hb_ref, o_ref = refs
        half = g1_ref.shape[0] // 2
        hr = hf_ref[:half] + hb_ref[:half]
        hi = hf_ref[half:] - hb_ref[half:]
    else:
        g1_ref, x_ref, o_ref = refs
    for bi in range(bt):
        xs = _dft_dot(g1_ref, x_ref[bi])
        if spec:
            xr, xi = xs[:half], xs[half:]
            ys = jnp.concatenate([xr * hr - xi * hi, xr * hi + xi * hr], axis=0)
            xs = _dft_dot(g2_ref, ys)
        o_ref[bi] = xs


def _idft_gate_body(g_ref, x_ref, v_ref, gate_ref, bias_ref, o_ref, *, nb):
    for bb in range(nb):
        spec = jnp.concatenate([x_ref[:, 0, bb, :], x_ref[:, 1, bb, :]], axis=0)
        y = _dft_dot(g_ref, spec)
        o_ref[:, bb, :] = gate_ref[:, bb, :] * (y + v_ref[:, bb, :] * bias_ref[...])


def _dft_a(g, x, xsel, *, m1, nb, dt):
    _, bsz, half, n2s, d = x.shape
    return pl.pallas_call(
        functools.partial(_dft_a_body, nb=nb, m1=m1),
        grid=(bsz, n2s // nb, d // dt),
        in_specs=[pl.BlockSpec((2 * m1, half), lambda bi, bj, di: (0, 0)),
                  pl.BlockSpec((None, None, half, nb, dt), lambda bi, bj, di: (xsel, bi, 0, bj, di))],
        out_specs=pl.BlockSpec((None, m1, 2, nb, dt), lambda bi, bj, di: (bi, 0, 0, bj, di)),
        out_shape=jax.ShapeDtypeStruct((bsz, m1, 2, n2s, d), F32),
        compiler_params=_cp(("parallel", "parallel", "parallel"), 48),
        name="dft_a",
    )(g, x)


def _dft_slab(g1, g2, x, hspec, order, *, bt):
    bsz, s, r, d = x.shape
    spec = g2 is not None
    gspec = pl.BlockSpec((None, r, r), lambda si, bi: (si, 0, 0))
    xspec = pl.BlockSpec((bt, None, r, d), lambda si, bi: (bi, si, 0, 0))
    if spec:
        in_specs = [gspec, gspec, xspec,
                    pl.BlockSpec((None, None, r, d), lambda si, bi: (2 * order, si, 0, 0)),
                    pl.BlockSpec((None, None, r, d), lambda si, bi: (2 * order + 1, si, 0, 0))]
        args = (g1, g2, x, hspec, hspec)
    else:
        in_specs, args = [gspec, xspec], (g1, x)
    return pl.pallas_call(
        functools.partial(_dft_slab_body, bt=bt, spec=spec),
        grid=(s, bsz // bt),
        in_specs=in_specs,
        out_specs=xspec,
        out_shape=jax.ShapeDtypeStruct((bsz, s, r, d), F32),
        compiler_params=_cp(("parallel", "parallel"), 48),
        name="dft_spec" if spec else "dft_c",
    )(*args)


def _idft_gate(g, x, v, vsel, gates, gsel, bias, *, nb, dt):
    bsz, m1, _, n2s, d = x.shape
    half = g.shape[0]
    tspec = lambda sel: pl.BlockSpec((None, None, half, nb, dt), lambda bi, bj, di: (sel, bi, 0, bj, di))
    return pl.pallas_call(
        functools.partial(_idft_gate_body, nb=nb),
        grid=(bsz, n2s // nb, d // dt),
        in_specs=[pl.BlockSpec((half, 2 * m1), lambda bi, bj, di: (0, 0)),
                  pl.BlockSpec((None, m1, 2, nb, dt), lambda bi, bj, di: (bi, 0, 0, bj, di)),
                  tspec(vsel), tspec(gsel),
                  pl.BlockSpec((1, dt), lambda bi, bj, di: (0, di))],
        out_specs=pl.BlockSpec((None, half, nb, dt), lambda bi, bj, di: (bi, 0, bj, di)),
        out_shape=jax.ShapeDtypeStruct((bsz, half, n2s, d), F32),
        compiler_params=_cp(("parallel", "parallel", "parallel"), 48),
        name="idft_gate",
    )(g, x, v, gates, bias)


def _fft_plan(length):
    n = 2 * length
    n1 = 256 if n >= 32768 else 64
    return n1, n // n1


def hyena_long_convs(xs, filt, f_bias):
    _, bsz, length, d = xs.shape
    n1, n2s = _fft_plan(length)
    m1, g_a, g_c, g_ci, g_ai = _dft_tables(n1, n2s)
    half = n1 // 2
    big = length >= 8192
    bt = 1 if big else min(8, bsz)
    nb = 8 if big else 16
    dt = d // 2 if big else d
    fa = _dft_a(g_a, filt.reshape(1, 4, half, n2s, d), 0, m1=m1, nb=nb, dt=dt)
    hspec = _dft_slab(g_c, None, fa.reshape(4, m1, 2 * n2s, d), None, 0, bt=1)
    xs6 = xs.reshape(3, bsz, half, n2s, d)
    z, zsel = xs6, 2
    out = None
    for order in range(HYENA_ORDER):
        sa = _dft_a(g_a, z, zsel, m1=m1, nb=nb, dt=dt).reshape(bsz, m1, 2 * n2s, d)
        sb = _dft_slab(g_c, g_ci, sa, hspec, order, bt=bt).reshape(bsz, m1, 2, n2s, d)
        out = _idft_gate(g_ai, sb, z, zsel, xs6, order, f_bias[order:order + 1], nb=nb, dt=dt)
        z, zsel = out[None], 0
    return out.reshape(bsz, length, d)


def _prefix_tables(tm):
    r = jnp.arange(tm, dtype=jnp.int32)
    upper = jnp.where(r[:, None] < r[None, :], 2.0, jnp.where(r[:, None] == r[None, :], 1.0, 0.0))
    eye = jnp.where(r[:, None] == r[None, :], 1.0, 0.0)
    return upper.astype(BF16), upper.T.astype(BF16), eye.astype(BF16)


def _trunk(x, p, wts):
    bsz, length, d = x.shape
    t = bsz * length
    xf = x.reshape(t, d)
    zeros_d = jnp.zeros((1, d), F32)
    ones_d = jnp.ones((1, d), F32)
    tables = _prefix_tables(MOE_ST)
    for i in range(DEPTH):
        j, kind = i // N_MIXERS, i % N_MIXERS
        if kind == 0:
            lam_init = 0.8 - 0.6 * math.exp(-0.3 * i)
            qkv = linear(xf, wts["attn_w_qkv"][j], jnp.zeros((1, 3 * d), F32), out_dtype=BF16, name="attn_qkv")
            a = diff_attention(qkv.reshape(bsz, length, 3 * d), wts["attn_lam"][j], wts["attn_subln_g"][j],
                               lam_init=lam_init).reshape(t, d)
            w_o, b_o = wts["attn_w_o"][j], zeros_d
        elif kind == 1:
            hglu = linear(xf, wts["conv_w_pw1"][j], wts["conv_b_pw1"][j], out_dtype=F32, glu=True, name="conv_pw1")
            a = dwconv(hglu.reshape(bsz, length, d), wts["conv_w_dw"][j], wts["conv_b_dw"][j],
                       wts["conv_ln_g"][j], wts["conv_ln_b"][j], norm_act=True, out_dtype=BF16).reshape(t, d)
            w_o, b_o = wts["conv_w_pw2"][j], wts["conv_b_pw2"][j]
        else:
            u = linear(xf, wts["hy_w_in"][j], wts["hy_b_in"][j], out_dtype=F32, name="hyena_in")
            xs = dwconv(u.reshape(bsz, length, 3 * d), wts["hy_w_short"][j], wts["hy_b_short"][j],
                        ones_d, zeros_d, norm_act=False, out_dtype=F32)
            filt = hyena_filters(length, wts["hy_f_w1"][j], wts["hy_f_b1"][j], wts["hy_f_w2"][j], wts["hy_f_b2"][j],
                                 wts["hy_f_w3"][j], wts["hy_f_b3"][j], wts["hy_f_freq"][j], wts["hy_f_wout"][j])
            a = hyena_long_convs(xs, filt, wts["hy_f_bias"][j]).reshape(t, d)
            w_o, b_o = wts["hy_w_out"][j], wts["hy_b_out"][j]
        x1, comb = proj_ln_route(a, w_o, b_o, xf, wts["ln1_g"][i], wts["ln1_b"][i],
                                 wts["route_w"][i], wts["route_b"][i])
        xf = moe_ple(x1, comb, tables, wts["moe_w_gate"][i], wts["moe_w_up"][i], wts["moe_w_down"][i],
                     wts["ln2_g"][i], wts["ln2_b"][i], p[i].reshape(t, PLE_DIM),
                     wts["ple_w_up"][i], wts["ple_w_gate"][i])
    return xf.reshape(bsz, length, d)


def kernel(x_prompt, x_sample, p_prompt, p_sample, attn_w_qkv, attn_w_o, attn_lam_q1, attn_lam_k1, attn_lam_q2, attn_lam_k2, attn_subln_g, conv_w_pw1, conv_b_pw1, conv_w_dw, conv_b_dw, conv_ln_g, conv_ln_b, conv_w_pw2, conv_b_pw2, hy_w_in, hy_b_in, hy_w_short, hy_b_short, hy_f_w1, hy_f_b1, hy_f_w2, hy_f_b2, hy_f_w3, hy_f_b3, hy_f_freq, hy_f_wout, hy_f_bias, hy_w_out, hy_b_out, ln1_g, ln1_b, ln2_g, ln2_b, moe_w_group, moe_b_group, moe_w_expert, moe_b_expert, moe_w_gate, moe_w_up, moe_w_down, ple_w_up, ple_w_gate):
    d = D_MODEL
    row = lambda a: a[:, None, :]
    q_scale = jnp.concatenate([jnp.full((d,), HEAD_DIM ** -0.5 * LOG2E, F32), jnp.ones((2 * d,), F32)])
    route_w = jnp.concatenate([jnp.swapaxes(moe_w_group, 1, 2), jnp.swapaxes(moe_w_expert, 1, 2),
                               jnp.zeros((DEPTH, ROUTE_ROWS - N_GROUPS - N_EXPERTS, d), F32)], axis=1)
    route_b = jnp.concatenate([moe_b_group, moe_b_expert,
                               jnp.zeros((DEPTH, ROUTE_ROWS - N_GROUPS - N_EXPERTS), F32)], axis=1)[:, :, None]
    wts = {
        "attn_w_qkv": (attn_w_qkv * q_scale).astype(BF16),
        "attn_w_o": attn_w_o.astype(BF16),
        "attn_lam": jnp.stack([attn_lam_q1, attn_lam_k1, attn_lam_q2, attn_lam_k2], axis=1),
        "attn_subln_g": attn_subln_g[:, :, None],
        "conv_w_pw1": conv_w_pw1.astype(BF16), "conv_b_pw1": row(conv_b_pw1),
        "conv_w_dw": conv_w_dw, "conv_b_dw": row(conv_b_dw),
        "conv_ln_g": row(conv_ln_g), "conv_ln_b": row(conv_ln_b),
        "conv_w_pw2": conv_w_pw2.astype(BF16), "conv_b_pw2": row(conv_b_pw2),
        "hy_w_in": hy_w_in.astype(BF16), "hy_b_in": row(hy_b_in),
        "hy_w_short": hy_w_short, "hy_b_short": row(hy_b_short),
        "hy_f_w1": hy_f_w1, "hy_f_b1": row(hy_f_b1), "hy_f_w2": hy_f_w2, "hy_f_b2": row(hy_f_b2),
        "hy_f_w3": hy_f_w3, "hy_f_b3": row(hy_f_b3), "hy_f_freq": row(hy_f_freq), "hy_f_wout": hy_f_wout,
        "hy_f_bias": hy_f_bias,
        "hy_w_out": hy_w_out.astype(BF16), "hy_b_out": row(hy_b_out),
        "ln1_g": row(ln1_g), "ln1_b": row(ln1_b), "ln2_g": row(ln2_g), "ln2_b": row(ln2_b),
        "route_w": route_w, "route_b": route_b,
        "moe_w_gate": moe_w_gate.astype(BF16), "moe_w_up": moe_w_up.astype(BF16),
        "moe_w_down": moe_w_down.astype(BF16),
        "ple_w_up": ple_w_up.astype(BF16), "ple_w_gate": ple_w_gate.astype(BF16),
    }
    y_prompt = _trunk(x_prompt, p_prompt, wts)
    y_sample = _trunk(x_sample, p_sample, wts)
    return (y_prompt, y_sample)
```

```python
import functools
import math

import jax
import jax.numpy as jnp
from jax import lax
from jax.experimental import pallas as pl
from jax.experimental.pallas import tpu as pltpu

F32 = jnp.float32
BF16 = jnp.bfloat16
HIGHEST = lax.Precision.HIGHEST

D_MODEL = 1024
DEPTH = 4
N_MIXERS = 3
N_HEADS = 8
HEAD_DIM = 64
CONV_WIDTH = 31
HYENA_ORDER = 2
SHORT_WIDTH = 3
POS_BANDS = 16
FILTER_WIDTH = 64
MAX_DECAY = math.log(1e-2) / 0.3
MIN_DECAY = math.log(1e-2) / 1.5
N_GROUPS = 4
EXPERTS_PER_GROUP = 4
N_EXPERTS = N_GROUPS * EXPERTS_PER_GROUP
EXPERT_FF = 512
PLE_DIM = 256
ALPHA = (2 * DEPTH) ** 0.25
LN_EPS = 1e-5
LOG2E = 1.4426950408889634

HALO = 16
CONV_ROWS = 16
ROUTE_ROWS = 24
MOE_TM = 1024
MOE_ST = 256
MOE_CAPS = 64
ATT_TQ = 512
ATT_TQ_SHORT = 1024
ATT_TK = 1024
ATT_CW = 256
ATT_FILL = 512
NT_DIMS = (((1,), (1,)), ((), ()))


def _cp(sem, vmem_mb):
    return pltpu.CompilerParams(dimension_semantics=sem, vmem_limit_bytes=vmem_mb << 20)


def _layer_norm(z, g, b):
    mu = jnp.mean(z, axis=-1, keepdims=True)
    zc = z - mu
    var = jnp.mean(zc * zc, axis=-1, keepdims=True)
    return zc * lax.rsqrt(var + LN_EPS) * g + b


def _sigmoid(x):
    return 1.0 / (1.0 + jnp.exp(-x))


def _linear_body(x_ref, w_ref, b_ref, o_ref, *, nc, glu):
    x = x_ref[...].astype(BF16)
    n_out = o_ref.shape[-1]
    for n0 in range(0, n_out, nc):
        a = jnp.dot(x, w_ref[:, n0:n0 + nc], preferred_element_type=F32) + b_ref[:, n0:n0 + nc]
        if glu:
            g = (jnp.dot(x, w_ref[:, n_out + n0:n_out + n0 + nc], preferred_element_type=F32)
                 + b_ref[:, n_out + n0:n_out + n0 + nc])
            a = a * _sigmoid(g)
        o_ref[:, n0:n0 + nc] = a.astype(o_ref.dtype)


def linear(x, w, b, *, out_dtype, glu=False, tm=512, nc=512, name="linear"):
    t, k = x.shape
    n = w.shape[1]
    n_out = n // 2 if glu else n
    return pl.pallas_call(
        functools.partial(_linear_body, nc=nc, glu=glu),
        grid=(t // tm,),
        in_specs=[pl.BlockSpec((tm, k), lambda i: (i, 0)),
                  pl.BlockSpec((k, n), lambda i: (0, 0)),
                  pl.BlockSpec((1, n), lambda i: (0, 0))],
        out_specs=pl.BlockSpec((tm, n_out), lambda i: (i, 0)),
        out_shape=jax.ShapeDtypeStruct((t, n_out), out_dtype),
        compiler_params=_cp(("parallel",), 48),
        name=name,
    )(x, w, b)


def _split3(x):
    hi = x.astype(BF16).astype(F32)
    mid = (x - hi).astype(BF16).astype(F32)
    lo = (x - hi - mid).astype(BF16).astype(F32)
    return [hi, mid, lo]


def _lane_table(lane, values, first):
    out = jnp.zeros(lane.shape, F32)
    for n, val in enumerate(values):
        out = jnp.where(lane == first + n, val, out)
    return out


def _attn_body(lam_ref, g_ref, q_ref, k_ref, v_ref, o_ref, kp_scr, vt_scr, d0_scr, qq_scr, s0_scr, s1_scr,
               m_scr, l_scr, acc_scr, *, tq, tk, lam_init):
    h = pl.program_id(1)
    qi = pl.program_id(2)
    length = k_ref.shape[0]
    nk = length // tk
    n_other = nk - 1
    hd2 = 2 * HEAD_DIM
    slope2 = jnp.exp2(-(jnp.full((1, 1), h, jnp.int32) + 1).astype(F32)) * LOG2E
    c1 = _split3(slope2)
    c128 = [128.0 * c for c in c1]

    @pl.when(qi == 0)
    def _():
        d0_scr[...] = (lax.broadcasted_iota(jnp.int32, (tk, tq), 1)
                       - lax.broadcasted_iota(jnp.int32, (tk, tq), 0)).astype(F32)
        lane = lax.broadcasted_iota(jnp.int32, (ATT_FILL, hd2), 1)
        consts = _lane_table(lane, [-c for c in c128] + [-c for c in c1], 0)
        eye = jnp.where(lax.broadcasted_iota(jnp.int32, (hd2, hd2), 0)
                        == lax.broadcasted_iota(jnp.int32, (hd2, hd2), 1), 1.0, 0.0).astype(BF16)

        def fill(ci, carry):
            r0 = pl.multiple_of(ci * ATT_FILL, ATT_FILL)
            pos = r0 + lax.broadcasted_iota(jnp.int32, (ATT_FILL, hd2), 0)
            hi_digit = jnp.right_shift(pos, 7).astype(F32)
            lo_digit = jnp.bitwise_and(pos, 127).astype(F32)
            aug = jnp.where(lane < 6, consts, jnp.where(lane < 9, hi_digit, jnp.where(lane < 12, lo_digit, 0.0)))
            kp_scr[pl.ds(r0, ATT_FILL), 0:hd2] = k_ref[pl.ds(r0, ATT_FILL), :]
            kp_scr[pl.ds(r0, ATT_FILL), hd2:2 * hd2] = aug.astype(BF16)
            vt_scr[:, pl.ds(r0, ATT_FILL)] = lax.dot_general(
                eye, v_ref[pl.ds(r0, ATT_FILL), :], NT_DIMS, preferred_element_type=F32).astype(BF16)
            return carry

        lax.fori_loop(0, length // ATT_FILL, fill, 0)

    q = q_ref[...]
    lane = lax.broadcasted_iota(jnp.int32, (tq, hd2), 1)
    zero = jnp.zeros_like(q)
    tpos = qi * tq + lax.broadcasted_iota(jnp.int32, (tq, hd2), 0)
    hi_digit = jnp.right_shift(tpos, 7).astype(F32)
    lo_digit = jnp.bitwise_and(tpos, 127).astype(F32)
    consts = _lane_table(lane, c128 + c1, 6)
    augq = jnp.where(lane < 3, hi_digit, jnp.where(lane < 6, lo_digit, consts))
    for var, aug in enumerate((augq.astype(BF16), (-augq).astype(BF16))):
        qq_scr[var, 0:tq, 0:hd2] = jnp.where(lane < HEAD_DIM, q, zero)
        qq_scr[var, tq:2 * tq, 0:hd2] = jnp.where(lane >= HEAD_DIM, q, zero)
        qq_scr[var, 0:tq, hd2:2 * hd2] = aug
        qq_scr[var, tq:2 * tq, hd2:2 * hd2] = aug

    m_scr[...] = jnp.full(m_scr.shape, -jnp.inf, F32)
    l_scr[...] = jnp.zeros(l_scr.shape, F32)
    acc_scr[...] = jnp.zeros(acc_scr.shape, F32)

    kd = (qi * tq) // tk

    def other_block(n):
        return jnp.where(n < n_other, jnp.where(n < kd, n, n + 1), kd)

    def scores(blk, s_scr):
        ks = pl.multiple_of(blk * tk, tk)
        var = (blk > kd).astype(jnp.int32)
        s_scr[...] = lax.dot_general(kp_scr[pl.ds(ks, tk), :], qq_scr[var], NT_DIMS, preferred_element_type=F32)

    def softmax_pv(blk, s_scr, diag):
        ks = pl.multiple_of(blk * tk, tk)
        cdiag = (qi * tq - blk * tk).astype(F32)
        vt = vt_scr[:, pl.ds(ks, tk)]
        m_all = m_scr[...]
        l_all = l_scr[...]
        m_out, l_out = [], []
        for cg in range(2 * tq // ATT_CW):
            cols = slice(cg * ATT_CW, (cg + 1) * ATT_CW)
            s = s_scr[:, cols]
            if diag:
                dc = (cg * ATT_CW) % tq
                s = s + jnp.minimum(d0_scr[:, dc:dc + ATT_CW] + cdiag, 0.0) * (2.0 * slope2)
            m_old = m_all[:, cols]
            m_new = jnp.maximum(m_old, jnp.max(s, axis=0, keepdims=True))
            p = jnp.exp2(s - m_new)
            alpha = jnp.exp2(m_old - m_new)
            l_out.append(alpha * l_all[:, cols] + jnp.sum(p, axis=0, keepdims=True))
            m_out.append(m_new)
            pv = jnp.dot(vt, p.astype(BF16), preferred_element_type=F32)
            acc_scr[:, cols] = alpha * acc_scr[:, cols] + pv
        m_scr[...] = jnp.concatenate(m_out, axis=1)
        l_scr[...] = jnp.concatenate(l_out, axis=1)

    scores(other_block(0), s0_scr)

    def pair(n2, carry):
        n = 2 * n2
        scores(other_block(n + 1), s1_scr)
        softmax_pv(other_block(n), s0_scr, False)
        scores(other_block(n + 2), s0_scr)
        softmax_pv(other_block(n + 1), s1_scr, False)
        return carry

    lax.fori_loop(0, n_other // 2, pair, 0)
    if n_other % 2:
        scores(kd, s1_scr)
        softmax_pv(other_block(n_other - 1), s0_scr, False)
        softmax_pv(kd, s1_scr, True)
    else:
        softmax_pv(kd, s0_scr, True)

    lam = lam_ref[...]
    lam_full = (jnp.exp(jnp.sum(lam[0:1] * lam[1:2], axis=-1, keepdims=True))
                - jnp.exp(jnp.sum(lam[2:3] * lam[3:4], axis=-1, keepdims=True)) + lam_init)
    ot = acc_scr[...] / l_scr[...]
    ot = ot[:, :tq] - lam_full * ot[:, tq:]
    ms = jnp.mean(ot * ot, axis=0, keepdims=True)
    ot = ot * lax.rsqrt(ms + LN_EPS) * g_ref[...] * (1.0 - lam_init)
    o_ref[...] = jnp.transpose(ot).astype(o_ref.dtype)


def diff_attention(qkv, lam, subln_g, *, lam_init, tk=ATT_TK):
    b, l, _ = qkv.shape
    tq = ATT_TQ_SHORT if l == 2 * tk else ATT_TQ
    assert l % tk == 0 and l // tk >= 2 and tk % tq == 0 and l <= 128 * 128
    hd2 = 2 * HEAD_DIM
    return pl.pallas_call(
        functools.partial(_attn_body, tq=tq, tk=tk, lam_init=lam_init),
        grid=(b, N_HEADS, l // tq),
        in_specs=[pl.BlockSpec((4, HEAD_DIM), lambda bi, h, qi: (0, 0)),
                  pl.BlockSpec((hd2, 1), lambda bi, h, qi: (0, 0)),
                  pl.BlockSpec((None, tq, hd2), lambda bi, h, qi: (bi, qi, h)),
                  pl.BlockSpec((None, l, hd2), lambda bi, h, qi: (bi, 0, N_HEADS + h)),
                  pl.BlockSpec((None, l, hd2), lambda bi, h, qi: (bi, 0, 2 * N_HEADS + h))],
        out_specs=pl.BlockSpec((None, tq, hd2), lambda bi, h, qi: (bi, qi, h)),
        out_shape=jax.ShapeDtypeStruct((b, l, D_MODEL), BF16),
        scratch_shapes=[pltpu.VMEM((l, 2 * hd2), BF16), pltpu.VMEM((hd2, l), BF16), pltpu.VMEM((tk, tq), F32),
                        pltpu.VMEM((2, 2 * tq, 2 * hd2), BF16),
                        pltpu.VMEM((tk, 2 * tq), F32), pltpu.VMEM((tk, 2 * tq), F32),
                        pltpu.VMEM((1, 2 * tq), F32), pltpu.VMEM((1, 2 * tq), F32),
                        pltpu.VMEM((hd2, 2 * tq), F32)],
        compiler_params=_cp(("parallel", "parallel", "arbitrary"), 56),
        name="diff_attention",
    )(lam, subln_g, qkv, qkv, qkv)


def _route(x1, wr, br, comb_ref):
    lt = lax.dot_general(wr, x1, NT_DIMS, precision=HIGHEST, preferred_element_type=F32) + br
    gl = [lt[g:g + 1] for g in range(N_GROUPS)]
    gmax = jnp.maximum(jnp.maximum(gl[0], gl[1]), jnp.maximum(gl[2], gl[3]))
    gidx = jnp.where(gl[0] == gmax, 0, jnp.where(gl[1] == gmax, 1, jnp.where(gl[2] == gmax, 2, 3)))
    gw = 1.0 / (jnp.exp(gl[0] - gmax) + jnp.exp(gl[1] - gmax) + jnp.exp(gl[2] - gmax) + jnp.exp(gl[3] - gmax))
    el = []
    for j in range(EXPERTS_PER_GROUP):
        acc = jnp.zeros_like(gmax)
        for g in range(N_GROUPS):
            r = N_GROUPS + g * EXPERTS_PER_GROUP + j
            acc = jnp.where(gidx == g, lt[r:r + 1], acc)
        el.append(acc)
    v1 = jnp.maximum(jnp.maximum(el[0], el[1]), jnp.maximum(el[2], el[3]))
    i1 = jnp.where(el[0] == v1, 0, jnp.where(el[1] == v1, 1, jnp.where(el[2] == v1, 2, 3)))
    neg = jnp.full_like(v1, -jnp.inf)
    el2 = [jnp.where(i1 == j, neg, el[j]) for j in range(EXPERTS_PER_GROUP)]
    v2 = jnp.maximum(jnp.maximum(el2[0], el2[1]), jnp.maximum(el2[2], el2[3]))
    i2 = jnp.where(el2[0] == v2, 0, jnp.where(el2[1] == v2, 1, jnp.where(el2[2] == v2, 2, 3)))
    e21 = jnp.exp(v2 - v1)
    w1 = gw / (1.0 + e21)
    w2 = gw * e21 / (1.0 + e21)
    zero = jnp.zeros_like(v1)
    for g in range(N_GROUPS):
        for j in range(EXPERTS_PER_GROUP):
            wj = jnp.where(i1 == j, w1, jnp.where(i2 == j, w2, zero))
            r = g * EXPERTS_PER_GROUP + j
            comb_ref[r:r + 1, :] = jnp.where(gidx == g, wj, zero)


def _proj_ln_route_body(a_ref, w_ref, b_ref, x_ref, g_ref, beta_ref, wr_ref, br_ref, x1_ref, comb_ref):
    h = jnp.dot(a_ref[...].astype(BF16), w_ref[...], preferred_element_type=F32) + b_ref[...]
    x1 = _layer_norm(ALPHA * x_ref[...] + h, g_ref[...], beta_ref[...])
    x1_ref[...] = x1
    _route(x1, wr_ref[...], br_ref[...], comb_ref)


def proj_ln_route(a, w, b, x, ln_g, ln_b, wr, br, *, tm=512):
    t, k = a.shape
    d = D_MODEL
    const = lambda i: (0, 0)
    return pl.pallas_call(
        _proj_ln_route_body,
        grid=(t // tm,),
        in_specs=[pl.BlockSpec((tm, k), lambda i: (i, 0)),
                  pl.BlockSpec((k, d), const), pl.BlockSpec((1, d), const),
                  pl.BlockSpec((tm, d), lambda i: (i, 0)),
                  pl.BlockSpec((1, d), const), pl.BlockSpec((1, d), const),
                  pl.BlockSpec((ROUTE_ROWS, d), const), pl.BlockSpec((ROUTE_ROWS, 1), const)],
        out_specs=[pl.BlockSpec((tm, d), lambda i: (i, 0)),
                   pl.BlockSpec((N_EXPERTS, tm), lambda i: (0, i))],
        out_shape=[jax.ShapeDtypeStruct((t, d), F32), jax.ShapeDtypeStruct((N_EXPERTS, t), F32)],
        compiler_params=_cp(("parallel",), 48),
        name="proj_ln_route",
    )(a, w, b, x, ln_g, ln_b, wr, br)


def _moe_body(x1_ref, comb_ref, uw_ref, lw_ref, wg_ref, wu_ref, wd_ref, g2_ref, b2_ref, p_ref, up_ref, gate_ref,
              o_ref, vrow_scr, vcol_scr, xg_scr, og_scr, wsl_scr, fast_scr, *, tm, st, caps):
    e = pl.program_id(1)
    n_sub = tm // st

    def sub_rows(sub):
        return slice(sub * st, (sub + 1) * st)

    @pl.when(e == 0)
    def _():
        member = jnp.where(comb_ref[...] > 0.0, 1.0, 0.0).astype(BF16)
        for sub in range(n_sub):
            ms = member[:, sub_rows(sub)]
            vrow_scr[:, sub_rows(sub)] = jnp.dot(ms, uw_ref[...], preferred_element_type=F32)
            vcol_scr[sub_rows(sub), :] = lax.dot_general(lw_ref[...], ms, NT_DIMS, preferred_element_type=F32)
        most = ((jnp.max(vrow_scr[...]) + 1.0) * 0.5).astype(jnp.int32)
        fast_scr[0] = (most <= caps).astype(jnp.int32)

        @pl.when(most <= caps)
        def _():
            tgt = 2.0 * lax.broadcasted_iota(jnp.int32, (caps, 1), 0).astype(F32) + 1.0
            for sub in range(n_sub):
                vr = vrow_scr[:, sub_rows(sub)]
                cw = comb_ref[:, sub_rows(sub)]
                blocks = []
                for ee in range(N_EXPERTS):
                    hit = vr[ee:ee + 1, :] == tgt
                    blocks.append(jnp.where(hit, 1.0, 0.0).astype(BF16))
                    wsl_scr[ee, sub * caps:(sub + 1) * caps, :] = jnp.sum(
                        jnp.where(hit, cw[ee:ee + 1, :], 0.0), axis=1, keepdims=True)
                gather = jnp.concatenate(blocks, axis=0)
                xg = jnp.dot(gather, x1_ref[sub_rows(sub), :].astype(BF16),
                             preferred_element_type=F32).astype(BF16)
                for ee in range(N_EXPERTS):
                    xg_scr[ee, sub * caps:(sub + 1) * caps, :] = xg[ee * caps:(ee + 1) * caps]

        @pl.when(most > caps)
        def _():
            o_ref[...] = jnp.zeros(o_ref.shape, F32)

    fast = fast_scr[0] == 1

    def expert_ffn(xg, wslot):
        hg = jnp.dot(xg, wg_ref[...], preferred_element_type=F32)
        hu = jnp.dot(xg, wu_ref[...], preferred_element_type=F32)
        hid = hg * _sigmoid(hg) * hu * wslot
        return jnp.dot(hid.astype(BF16), wd_ref[...], preferred_element_type=F32).astype(BF16)

    @pl.when(fast)
    def _():
        ob = expert_ffn(xg_scr[e], wsl_scr[e])
        r0 = pl.multiple_of(e * caps, caps)
        for sub in range(n_sub):
            og_scr[sub, pl.ds(r0, caps), :] = ob[sub * caps:(sub + 1) * caps]

    @pl.when(jnp.logical_not(fast))
    def _():
        sel = lax.broadcasted_iota(jnp.int32, (st, N_EXPERTS), 1) == e
        prow = lax.broadcasted_iota(jnp.int32, (8, st), 0)
        for sub in range(n_sub):
            vrow = vrow_scr[pl.ds(e, 1), sub_rows(sub)]
            vcol = jnp.sum(jnp.where(sel, vcol_scr[sub_rows(sub), :], 0.0), axis=1, keepdims=True)
            w_hi, w_mid, w_lo = _split3(comb_ref[pl.ds(e, 1), sub_rows(sub)])
            w3 = jnp.where(prow == 0, w_hi, jnp.where(prow == 1, w_mid, jnp.where(prow == 2, w_lo, 0.0))
                           ).astype(BF16)
            count = ((jnp.max(vrow) + 1.0) * 0.5).astype(jnp.int32)
            xsub = x1_ref[sub_rows(sub), :].astype(BF16)

            def chunk(c, carry):
                base = (c * caps).astype(F32)
                tgt_r = 2.0 * (base + lax.broadcasted_iota(jnp.int32, (caps, 1), 0).astype(F32)) + 1.0
                tgt_c = 2.0 * (base + lax.broadcasted_iota(jnp.int32, (1, caps), 1).astype(F32)) + 1.0
                gather = jnp.where(vrow == tgt_r, 1.0, 0.0).astype(BF16)
                scatter = jnp.where(vcol == tgt_c, 1.0, 0.0).astype(BF16)
                xg = jnp.dot(gather, xsub, preferred_element_type=F32).astype(BF16)
                ws = lax.dot_general(gather, w3, NT_DIMS, preferred_element_type=F32)
                ob = expert_ffn(xg, ws[:, 0:1] + ws[:, 1:2] + ws[:, 2:3])
                o_ref[sub_rows(sub), :] += jnp.dot(scatter, ob, preferred_element_type=F32)
                return carry

            lax.fori_loop(0, (count + (caps - 1)) // caps, chunk, 0)

    @pl.when(e == N_EXPERTS - 1)
    def _():
        @pl.when(fast)
        def _():
            lane = lax.broadcasted_iota(jnp.int32, (st, 2 * caps), 1)
            tgt = 2.0 * jnp.where(lane < caps, lane, lane - caps).astype(F32) + 1.0
            for sub in range(n_sub):
                vc = vcol_scr[sub_rows(sub), :]
                pieces = []
                for pr in range(N_EXPERTS // 2):
                    val = jnp.where(lane < caps, vc[:, 2 * pr:2 * pr + 1], vc[:, 2 * pr + 1:2 * pr + 2])
                    pieces.append(jnp.where(val == tgt, 1.0, 0.0).astype(BF16))
                scatter = jnp.concatenate(pieces, axis=1)
                o_ref[sub_rows(sub), :] = jnp.dot(scatter, og_scr[sub], preferred_element_type=F32)

        x2 = _layer_norm(ALPHA * x1_ref[...] + o_ref[...], g2_ref[...], b2_ref[...])
        up = jnp.dot(p_ref[...].astype(BF16), up_ref[...], preferred_element_type=F32)
        gt = jnp.dot(x2.astype(BF16), gate_ref[...], preferred_element_type=F32)
        o_ref[...] = x2 + up * _sigmoid(gt)


def moe_ple(x1, comb, uw, lw, wg, wu, wd, ln_g, ln_b, p, up, gate, *, tm=MOE_TM, st=MOE_ST, caps=MOE_CAPS):
    t, d = x1.shape
    ff = EXPERT_FF
    n_sub = tm // st
    assert 2 * caps == 128 and st % 128 == 0
    const = lambda i, e: (0, 0)
    once = pl.Buffered(1)
    return pl.pallas_call(
        functools.partial(_moe_body, tm=tm, st=st, caps=caps),
        grid=(t // tm, N_EXPERTS),
        in_specs=[pl.BlockSpec((tm, d), lambda i, e: (i, 0)),
                  pl.BlockSpec((N_EXPERTS, tm), lambda i, e: (0, i)),
                  pl.BlockSpec((st, st), const), pl.BlockSpec((st, st), const),
                  pl.BlockSpec((None, d, ff), lambda i, e: (e, 0, 0)),
                  pl.BlockSpec((None, d, ff), lambda i, e: (e, 0, 0)),
                  pl.BlockSpec((None, ff, d), lambda i, e: (e, 0, 0)),
                  pl.BlockSpec((1, d), const), pl.BlockSpec((1, d), const),
                  pl.BlockSpec((tm, PLE_DIM), lambda i, e: (i, 0), pipeline_mode=once),
                  pl.BlockSpec((PLE_DIM, d), const, pipeline_mode=once),
                  pl.BlockSpec((d, d), const, pipeline_mode=once)],
        out_specs=pl.BlockSpec((tm, d), lambda i, e: (i, 0)),
        out_shape=jax.ShapeDtypeStruct((t, d), F32),
        scratch_shapes=[pltpu.VMEM((N_EXPERTS, tm), F32), pltpu.VMEM((tm, N_EXPERTS), F32),
                        pltpu.VMEM((N_EXPERTS, n_sub * caps, d), BF16),
                        pltpu.VMEM((n_sub, N_EXPERTS * caps, d), BF16),
                        pltpu.VMEM((N_EXPERTS, n_sub * caps, 1), F32),
                        pltpu.SMEM((1,), jnp.int32)],
        compiler_params=_cp(("parallel", "arbitrary"), 58),
        name="moe_ple",
    )(x1, comb, uw, lw, wg, wu, wd, ln_g, ln_b, p, up, gate)


def _dwconv_body(prev_ref, cur_ref, next_ref, w_ref, b_ref, g_ref, beta_ref, o_ref, buf_scr, sh_scr,
                 *, width, tm, norm_act):
    i = pl.program_id(1)
    n = pl.num_programs(1)
    pad = width // 2
    halo = jnp.zeros(prev_ref.shape, F32)
    buf_scr[0:HALO, :] = jnp.where(i > 0, prev_ref[...], halo)
    buf_scr[HALO:HALO + tm, :] = cur_ref[...]
    buf_scr[HALO + tm:2 * HALO + tm, :] = jnp.where(i < n - 1, next_ref[...], halo)
    offs = [HALO - pad + k for k in range(width)]
    n_sh = tm + 2 * HALO - 8
    for sft in sorted({off % 8 for off in offs}):
        sh_scr[sft, 0:n_sh, :] = buf_scr[sft:sft + n_sh, :]

    def rows(r, carry):
        r0 = pl.multiple_of(r * CONV_ROWS, CONV_ROWS)
        acc = jnp.zeros((CONV_ROWS, o_ref.shape[1]), F32) + b_ref[...]
        for k, off in enumerate(offs):
            start = pl.multiple_of(r0 + (off // 8) * 8, 8)
            acc = acc + sh_scr[off % 8, pl.ds(start, CONV_ROWS), :] * w_ref[k:k + 1, :]
        if norm_act:
            acc = _layer_norm(acc, g_ref[...], beta_ref[...])
            acc = acc * _sigmoid(acc)
        o_ref[pl.ds(r0, CONV_ROWS), :] = acc.astype(o_ref.dtype)
        return carry

    lax.fori_loop(0, tm // CONV_ROWS, rows, 0)


def dwconv(x, w, b, ln_g, ln_b, *, norm_act, out_dtype, tm=256):
    bsz, l, c = x.shape
    width = w.shape[0]
    tc = D_MODEL
    nb = tm // HALO
    last = l // HALO - 1
    return pl.pallas_call(
        functools.partial(_dwconv_body, width=width, tm=tm, norm_act=norm_act),
        grid=(bsz, l // tm, c // tc),
        in_specs=[pl.BlockSpec((None, HALO, tc), lambda bi, i, j: (bi, jnp.maximum(i * nb - 1, 0), j)),
                  pl.BlockSpec((None, tm, tc), lambda bi, i, j: (bi, i, j)),
                  pl.BlockSpec((None, HALO, tc), lambda bi, i, j: (bi, jnp.minimum((i + 1) * nb, last), j)),
                  pl.BlockSpec((width, tc), lambda bi, i, j: (0, j)),
                  pl.BlockSpec((1, tc), lambda bi, i, j: (0, j)),
                  pl.BlockSpec((1, tc), lambda bi, i, j: (0, 0)),
                  pl.BlockSpec((1, tc), lambda bi, i, j: (0, 0))],
        out_specs=pl.BlockSpec((None, None, tm, tc), lambda bi, i, j: (j, bi, i, 0)),
        out_shape=jax.ShapeDtypeStruct((c // tc, bsz, l, tc), out_dtype),
        scratch_shapes=[pltpu.VMEM((tm + 2 * HALO, tc), F32), pltpu.VMEM((8, tm + 2 * HALO, tc), F32)],
        compiler_params=_cp(("parallel", "parallel", "parallel"), 32),
        name="dwconv%d" % width,
    )(x, x, x, w, b, ln_g, ln_b)


def _filter_body(w1t_ref, w1c_ref, w1s_ref, b1_ref, w2_ref, b2_ref, w3_ref, b3_ref, freq_ref, wout_ref,
                 o_ref, *, tl, length):
    i = pl.program_id(0)
    pos = (i * tl + lax.broadcasted_iota(jnp.int32, (tl, 1), 0)).astype(F32)
    t = pos / float(length - 1)
    ang = (2.0 * math.pi / length) * pos
    band = lax.broadcasted_iota(jnp.int32, (1, POS_BANDS), 1).astype(F32)
    bands = 1e-4 + band * ((POS_BANDS - 1 - 1e-4) / (POS_BANDS - 1))
    fw = bands * ang
    freq = freq_ref[...]
    dot = functools.partial(jnp.dot, precision=HIGHEST, preferred_element_type=F32)
    pre = t * w1t_ref[...] + dot(jnp.cos(fw), w1c_ref[...]) - dot(jnp.sin(fw), w1s_ref[...]) + b1_ref[...]
    hcur = jnp.sin(freq * pre)
    hcur = jnp.sin(freq * (dot(hcur, w2_ref[...]) + b2_ref[...]))
    hcur = jnp.sin(freq * (dot(hcur, w3_ref[...]) + b3_ref[...]))
    ch = lax.broadcasted_iota(jnp.int32, (1, D_MODEL), 1).astype(F32)
    deltas = jnp.abs(MIN_DECAY + ch * ((MAX_DECAY - MIN_DECAY) / (D_MODEL - 1)))
    decay = jnp.exp(-t * deltas)
    for j in range(2 * HYENA_ORDER):
        o_ref[j] = dot(hcur, wout_ref[:, j * D_MODEL:(j + 1) * D_MODEL]) * decay


def hyena_filters(length, w1, b1, w2, b2, w3, b3, freq, wout, *, tl=256):
    fwid = FILTER_WIDTH
    const = lambda i: (0, 0)
    nf = 2 * HYENA_ORDER
    return pl.pallas_call(
        functools.partial(_filter_body, tl=tl, length=length),
        grid=(length // tl,),
        in_specs=[pl.BlockSpec((1, fwid), const), pl.BlockSpec((POS_BANDS, fwid), const),
                  pl.BlockSpec((POS_BANDS, fwid), const), pl.BlockSpec((1, fwid), const),
                  pl.BlockSpec((fwid, fwid), const), pl.BlockSpec((1, fwid), const),
                  pl.BlockSpec((fwid, fwid), const), pl.BlockSpec((1, fwid), const),
                  pl.BlockSpec((1, fwid), const), pl.BlockSpec((fwid, nf * D_MODEL), const)],
        out_specs=pl.BlockSpec((nf, tl, D_MODEL), lambda i: (0, i, 0)),
        out_shape=jax.ShapeDtypeStruct((nf, length, D_MODEL), F32),
        compiler_params=_cp(("parallel",), 32),
        name="hyena_filters",
    )(w1[0:1], w1[1:1 + POS_BANDS], w1[1 + POS_BANDS:], b1, w2, b2, w3, b3, freq, wout)


def _dft_tables(n1, n2s):
    n = n1 * n2s
    m1 = (n1 // 2 + 1 + 7) // 8 * 8
    k1 = jnp.arange(m1, dtype=jnp.int32)
    valid = (k1 <= n1 // 2)
    a = jnp.arange(n1 // 2, dtype=jnp.int32)
    ang_a = (2.0 * math.pi / n1) * ((k1[:, None] * a[None, :]) % n1).astype(F32)
    ca = jnp.where(valid[:, None], jnp.cos(ang_a), 0.0)
    sa = jnp.where(valid[:, None], jnp.sin(ang_a), 0.0)
    g_a = jnp.concatenate([ca, -sa], axis=0)
    g_ai = jnp.concatenate([ca.T, -sa.T], axis=1)
    weight = jnp.where(valid, jnp.where((k1 == 0) | (k1 == n1 // 2), 1.0, 2.0), 0.0)
    b = jnp.arange(n2s, dtype=jnp.int32)
    k2 = jnp.arange(n2s, dtype=jnp.int32)
    idx = (b[None, None, :] * (k2[None, :, None] * n1 + k1[:, None, None])) % n
    ang_c = (2.0 * math.pi / n) * idx.astype(F32)
    cc, sc = jnp.cos(ang_c), jnp.sin(ang_c)
    g_c = jnp.concatenate([jnp.concatenate([cc, sc], axis=2),
                           jnp.concatenate([-sc, cc], axis=2)], axis=1)
    cct = jnp.swapaxes(cc, 1, 2) * (weight / n)[:, None, None]
    sct = jnp.swapaxes(sc, 1, 2) * (weight / n)[:, None, None]
    g_ci = jnp.concatenate([jnp.concatenate([cct, -sct], axis=2),
                            jnp.concatenate([sct, cct], axis=2)], axis=1)
    return m1, g_a.astype(BF16), g_c.astype(BF16), g_ci.astype(BF16), g_ai.astype(BF16)


def _dft_dot(g_ref, x):
    return jnp.dot(g_ref[...], x.astype(BF16), preferred_element_type=F32)


def _dft_a_body(g_ref, x_ref, o_ref, *, nb, m1):
    for bb in range(nb):
        res = _dft_dot(g_ref, x_ref[:, bb, :])
        o_ref[:, 0, bb, :] = res[:m1]
        o_ref[:, 1, bb, :] = res[m1:]


def _dft_slab_body(*refs, bt, spec):
    if spec:
        g1_ref, g2_ref, x_ref, hf_ref, hb_ref, o_ref = refs
        half = g1_ref.shape[0] // 2
        hr = hf_ref[:half] + hb_ref[:half]
        hi = hf_ref[half:] - hb_ref[half:]
    else:
        g1_ref, x_ref, o_ref = refs
    for bi in range(bt):
        xs = _dft_dot(g1_ref, x_ref[bi])
        if spec:
            xr, xi = xs[:half], xs[half:]
            ys = jnp.concatenate([xr * hr - xi * hi, xr * hi + xi * hr], axis=0)
            xs = _dft_dot(g2_ref, ys)
        o_ref[bi] = xs


def _idft_gate_body(g_ref, x_ref, v_ref, gate_ref, bias_ref, o_ref, *, nb):
    for bb in range(nb):
        spec = jnp.concatenate([x_ref[:, 0, bb, :], x_ref[:, 1, bb, :]], axis=0)
        y = _dft_dot(g_ref, spec)
        o_ref[:, bb, :] = gate_ref[:, bb, :] * (y + v_ref[:, bb, :] * bias_ref[...])


def _dft_a(g, x, xsel, *, m1, nb, dt):
    _, bsz, half, n2s, d = x.shape
    return pl.pallas_call(
        functools.partial(_dft_a_body, nb=nb, m1=m1),
        grid=(bsz, n2s // nb, d // dt),
        in_specs=[pl.BlockSpec((2 * m1, half), lambda bi, bj, di: (0, 0)),
                  pl.BlockSpec((None, None, half, nb, dt), lambda bi, bj, di: (xsel, bi, 0, bj, di))],
        out_specs=pl.BlockSpec((None, m1, 2, nb, dt), lambda bi, bj, di: (bi, 0, 0, bj, di)),
        out_shape=jax.ShapeDtypeStruct((bsz, m1, 2, n2s, d), F32),
        compiler_params=_cp(("parallel", "parallel", "parallel"), 48),
        name="dft_a",
    )(g, x)


def _dft_slab(g1, g2, x, hspec, order, *, bt):
    bsz, s, r, d = x.shape
    spec = g2 is not None
    gspec = pl.BlockSpec((None, r, r), lambda si, bi: (si, 0, 0))
    xspec = pl.BlockSpec((bt, None, r, d), lambda si, bi: (bi, si, 0, 0))
    if spec:
        in_specs = [gspec, gspec, xspec,
                    pl.BlockSpec((None, None, r, d), lambda si, bi: (2 * order, si, 0, 0)),
                    pl.BlockSpec((None, None, r, d), lambda si, bi: (2 * order + 1, si, 0, 0))]
        args = (g1, g2, x, hspec, hspec)
    else:
        in_specs, args = [gspec, xspec], (g1, x)
    return pl.pallas_call(
        functools.partial(_dft_slab_body, bt=bt, spec=spec),
        grid=(s, bsz // bt),
        in_specs=in_specs,
        out_specs=xspec,
        out_shape=jax.ShapeDtypeStruct((bsz, s, r, d), F32),
        compiler_params=_cp(("parallel", "parallel"), 48),
        name="dft_spec" if spec else "dft_c",
    )(*args)


def _idft_gate(g, x, v, vsel, gates, gsel, bias, *, nb, dt):
    bsz, m1, _, n2s, d = x.shape
    half = g.shape[0]
    tspec = lambda sel: pl.BlockSpec((None, None, half, nb, dt), lambda bi, bj, di: (sel, bi, 0, bj, di))
    return pl.pallas_call(
        functools.partial(_idft_gate_body, nb=nb),
        grid=(bsz, n2s // nb, d // dt),
        in_specs=[pl.BlockSpec((half, 2 * m1), lambda bi, bj, di: (0, 0)),
                  pl.BlockSpec((None, m1, 2, nb, dt), lambda bi, bj, di: (bi, 0, 0, bj, di)),
                  tspec(vsel), tspec(gsel),
                  pl.BlockSpec((1, dt), lambda bi, bj, di: (0, di))],
        out_specs=pl.BlockSpec((None, half, nb, dt), lambda bi, bj, di: (bi, 0, bj, di)),
        out_shape=jax.ShapeDtypeStruct((bsz, half, n2s, d), F32),
        compiler_params=_cp(("parallel", "parallel", "parallel"), 48),
        name="idft_gate",
    )(g, x, v, gates, bias)


def _fft_plan(length):
    n = 2 * length
    n1 = 256 if n >= 32768 else 64
    return n1, n // n1


def hyena_long_convs(xs, filt, f_bias):
    _, bsz, length, d = xs.shape
    n1, n2s = _fft_plan(length)
    m1, g_a, g_c, g_ci, g_ai = _dft_tables(n1, n2s)
    half = n1 // 2
    big = length >= 8192
    bt = 1 if big else min(8, bsz)
    nb = 8 if big else 16
    dt = d // 2 if big else d
    fa = _dft_a(g_a, filt.reshape(1, 4, half, n2s, d), 0, m1=m1, nb=nb, dt=dt)
    hspec = _dft_slab(g_c, None, fa.reshape(4, m1, 2 * n2s, d), None, 0, bt=1)
    xs6 = xs.reshape(3, bsz, half, n2s, d)
    z, zsel = xs6, 2
    out = None
    for order in range(HYENA_ORDER):
        sa = _dft_a(g_a, z, zsel, m1=m1, nb=nb, dt=dt).reshape(bsz, m1, 2 * n2s, d)
        sb = _dft_slab(g_c, g_ci, sa, hspec, order, bt=bt).reshape(bsz, m1, 2, n2s, d)
        out = _idft_gate(g_ai, sb, z, zsel, xs6, order, f_bias[order:order + 1], nb=nb, dt=dt)
        z, zsel = out[None], 0
    return out.reshape(bsz, length, d)


def _prefix_tables(tm):
    r = jnp.arange(tm, dtype=jnp.int32)
    upper = jnp.where(r[:, None] < r[None, :], 2.0, jnp.where(r[:, None] == r[None, :], 1.0, 0.0))
    return upper.astype(BF16), upper.T.astype(BF16)


def _trunk(x, p, wts):
    bsz, length, d = x.shape
    t = bsz * length
    xf = x.reshape(t, d)
    zeros_d = jnp.zeros((1, d), F32)
    ones_d = jnp.ones((1, d), F32)
    uw, lw = _prefix_tables(MOE_ST)
    for i in range(DEPTH):
        j, kind = i // N_MIXERS, i % N_MIXERS
        if kind == 0:
            lam_init = 0.8 - 0.6 * math.exp(-0.3 * i)
            qkv = linear(xf, wts["attn_w_qkv"][j], jnp.zeros((1, 3 * d), F32), out_dtype=BF16, name="attn_qkv")
            a = diff_attention(qkv.reshape(bsz, length, 3 * d), wts["attn_lam"][j], wts["attn_subln_g"][j],
                               lam_init=lam_init).reshape(t, d)
            w_o, b_o = wts["attn_w_o"][j], zeros_d
        elif kind == 1:
            hglu = linear(xf, wts["conv_w_pw1"][j], wts["conv_b_pw1"][j], out_dtype=F32, glu=True, name="conv_pw1")
            a = dwconv(hglu.reshape(bsz, length, d), wts["conv_w_dw"][j], wts["conv_b_dw"][j],
                       wts["conv_ln_g"][j], wts["conv_ln_b"][j], norm_act=True, out_dtype=BF16).reshape(t, d)
            w_o, b_o = wts["conv_w_pw2"][j], wts["conv_b_pw2"][j]
        else:
            u = linear(xf, wts["hy_w_in"][j], wts["hy_b_in"][j], out_dtype=F32, name="hyena_in")
            xs = dwconv(u.reshape(bsz, length, 3 * d), wts["hy_w_short"][j], wts["hy_b_short"][j],
                        ones_d, zeros_d, norm_act=False, out_dtype=F32)
            filt = hyena_filters(length, wts["hy_f_w1"][j], wts["hy_f_b1"][j], wts["hy_f_w2"][j], wts["hy_f_b2"][j],
                                 wts["hy_f_w3"][j], wts["hy_f_b3"][j], wts["hy_f_freq"][j], wts["hy_f_wout"][j])
            a = hyena_long_convs(xs, filt, wts["hy_f_bias"][j]).reshape(t, d)
            w_o, b_o = wts["hy_w_out"][j], wts["hy_b_out"][j]
        x1, comb = proj_ln_route(a, w_o, b_o, xf, wts["ln1_g"][i], wts["ln1_b"][i],
                                 wts["route_w"][i], wts["route_b"][i])
        xf = moe_ple(x1, comb, uw, lw, wts["moe_w_gate"][i], wts["moe_w_up"][i], wts["moe_w_down"][i],
                     wts["ln2_g"][i], wts["ln2_b"][i], p[i].reshape(t, PLE_DIM),
                     wts["ple_w_up"][i], wts["ple_w_gate"][i])
    return xf.reshape(bsz, length, d)


def kernel(x_prompt, x_sample, p_prompt, p_sample, attn_w_qkv, attn_w_o, attn_lam_q1, attn_lam_k1, attn_lam_q2, attn_lam_k2, attn_subln_g, conv_w_pw1, conv_b_pw1, conv_w_dw, conv_b_dw, conv_ln_g, conv_ln_b, conv_w_pw2, conv_b_pw2, hy_w_in, hy_b_in, hy_w_short, hy_b_short, hy_f_w1, hy_f_b1, hy_f_w2, hy_f_b2, hy_f_w3, hy_f_b3, hy_f_freq, hy_f_wout, hy_f_bias, hy_w_out, hy_b_out, ln1_g, ln1_b, ln2_g, ln2_b, moe_w_group, moe_b_group, moe_w_expert, moe_b_expert, moe_w_gate, moe_w_up, moe_w_down, ple_w_up, ple_w_gate):
    d = D_MODEL
    row = lambda a: a[:, None, :]
    q_scale = jnp.concatenate([jnp.full((d,), HEAD_DIM ** -0.5 * LOG2E, F32), jnp.ones((2 * d,), F32)])
    route_w = jnp.concatenate([jnp.swapaxes(moe_w_group, 1, 2), jnp.swapaxes(moe_w_expert, 1, 2),
                               jnp.zeros((DEPTH, ROUTE_ROWS - N_GROUPS - N_EXPERTS, d), F32)], axis=1)
    route_b = jnp.concatenate([moe_b_group, moe_b_expert,
                               jnp.zeros((DEPTH, ROUTE_ROWS - N_GROUPS - N_EXPERTS), F32)], axis=1)[:, :, None]
    wts = {
        "attn_w_qkv": (attn_w_qkv * q_scale).astype(BF16),
        "attn_w_o": attn_w_o.astype(BF16),
        "attn_lam": jnp.stack([attn_lam_q1, attn_lam_k1, attn_lam_q2, attn_lam_k2], axis=1),
        "attn_subln_g": attn_subln_g[:, :, None],
        "conv_w_pw1": conv_w_pw1.astype(BF16), "conv_b_pw1": row(conv_b_pw1),
        "conv_w_dw": conv_w_dw, "conv_b_dw": row(conv_b_dw),
        "conv_ln_g": row(conv_ln_g), "conv_ln_b": row(conv_ln_b),
        "conv_w_pw2": conv_w_pw2.astype(BF16), "conv_b_pw2": row(conv_b_pw2),
        "hy_w_in": hy_w_in.astype(BF16), "hy_b_in": row(hy_b_in),
        "hy_w_short": hy_w_short, "hy_b_short": row(hy_b_short),
        "hy_f_w1": hy_f_w1, "hy_f_b1": row(hy_f_b1), "hy_f_w2": hy_f_w2, "hy_f_b2": row(hy_f_b2),
        "hy_f_w3": hy_f_w3, "hy_f_b3": row(hy_f_b3), "hy_f_freq": row(hy_f_freq), "hy_f_wout": hy_f_wout,
        "hy_f_bias": hy_f_bias,
        "hy_w_out": hy_w_out.astype(BF16), "hy_b_out": row(hy_b_out),
        "ln1_g": row(ln1_g), "ln1_b": row(ln1_b), "ln2_g": row(ln2_g), "ln2_b": row(ln2_b),
        "route_w": route_w, "route_b": route_b,
        "moe_w_gate": moe_w_gate.astype(BF16), "moe_w_up": moe_w_up.astype(BF16),
        "moe_w_down": moe_w_down.astype(BF16),
        "ple_w_up": ple_w_up.astype(BF16), "ple_w_gate": ple_w_gate.astype(BF16),
    }
    y_prompt = _trunk(x_prompt, p_prompt, wts)
    y_sample = _trunk(x_sample, p_sample, wts)
    return (y_prompt, y_sample)
```

```python
import functools
import math

import jax
import jax.numpy as jnp
from jax import lax
from jax.experimental import pallas as pl
from jax.experimental.pallas import tpu as pltpu

F32 = jnp.float32
BF16 = jnp.bfloat16
HIGHEST = lax.Precision.HIGHEST

D_MODEL = 1024
DEPTH = 4
N_MIXERS = 3
N_HEADS = 8
HEAD_DIM = 64
CONV_WIDTH = 31
HYENA_ORDER = 2
SHORT_WIDTH = 3
POS_BANDS = 16
FILTER_WIDTH = 64
MAX_DECAY = math.log(1e-2) / 0.3
MIN_DECAY = math.log(1e-2) / 1.5
N_GROUPS = 4
EXPERTS_PER_GROUP = 4
N_EXPERTS = N_GROUPS * EXPERTS_PER_GROUP
EXPERT_FF = 512
PLE_DIM = 256
ALPHA = (2 * DEPTH) ** 0.25
LN_EPS = 1e-5
LOG2E = 1.4426950408889634

HALO = 16
CONV_ROWS = 32
ROUTE_ROWS = 24
MOE_TM = 1024
MOE_ST = 256
MOE_CAPS = 64
ATT_TQ = 512
ATT_TQ_SHORT = 1024
ATT_TK = 1024
ATT_CW = 256
ATT_FILL = 512
NT_DIMS = (((1,), (1,)), ((), ()))


def _cp(sem, vmem_mb):
    return pltpu.CompilerParams(dimension_semantics=sem, vmem_limit_bytes=vmem_mb << 20)


def _layer_norm(z, g, b):
    mu = jnp.mean(z, axis=-1, keepdims=True)
    zc = z - mu
    var = jnp.mean(zc * zc, axis=-1, keepdims=True)
    return zc * lax.rsqrt(var + LN_EPS) * g + b


def _sigmoid(x):
    return 1.0 / (1.0 + jnp.exp(-x))


def _linear_body(x_ref, w_ref, b_ref, o_ref, *, nc, glu):
    x = x_ref[...].astype(BF16)
    n_out = o_ref.shape[-1]
    for n0 in range(0, n_out, nc):
        a = jnp.dot(x, w_ref[:, n0:n0 + nc], preferred_element_type=F32) + b_ref[:, n0:n0 + nc]
        if glu:
            g = (jnp.dot(x, w_ref[:, n_out + n0:n_out + n0 + nc], preferred_element_type=F32)
                 + b_ref[:, n_out + n0:n_out + n0 + nc])
            a = a * _sigmoid(g)
        o_ref[:, n0:n0 + nc] = a.astype(o_ref.dtype)


def linear(x, w, b, *, out_dtype, glu=False, tm=512, nc=512, name="linear"):
    t, k = x.shape
    n = w.shape[1]
    n_out = n // 2 if glu else n
    return pl.pallas_call(
        functools.partial(_linear_body, nc=nc, glu=glu),
        grid=(t // tm,),
        in_specs=[pl.BlockSpec((tm, k), lambda i: (i, 0)),
                  pl.BlockSpec((k, n), lambda i: (0, 0)),
                  pl.BlockSpec((1, n), lambda i: (0, 0))],
        out_specs=pl.BlockSpec((tm, n_out), lambda i: (i, 0)),
        out_shape=jax.ShapeDtypeStruct((t, n_out), out_dtype),
        compiler_params=_cp(("parallel",), 48),
        name=name,
    )(x, w, b)


def _split3(x):
    hi = x.astype(BF16).astype(F32)
    mid = (x - hi).astype(BF16).astype(F32)
    lo = (x - hi - mid).astype(BF16).astype(F32)
    return [hi, mid, lo]


def _lane_table(lane, values, first):
    out = jnp.zeros(lane.shape, F32)
    for n, val in enumerate(values):
        out = jnp.where(lane == first + n, val, out)
    return out


def _attn_body(lam_ref, g_ref, q_ref, k_ref, v_ref, o_ref, kp_scr, vt_scr, d0_scr, qq_scr, s0_scr, s1_scr,
               m_scr, l_scr, acc_scr, *, tq, tk, lam_init):
    h = pl.program_id(1)
    qi = pl.program_id(2)
    length = k_ref.shape[0]
    nk = length // tk
    n_other = nk - 1
    hd2 = 2 * HEAD_DIM
    slope2 = jnp.exp2(-(jnp.full((1, 1), h, jnp.int32) + 1).astype(F32)) * LOG2E
    c1 = _split3(slope2)
    c128 = [128.0 * c for c in c1]

    @pl.when(qi == 0)
    def _():
        d0_scr[...] = (lax.broadcasted_iota(jnp.int32, (tk, tq), 1)
                       - lax.broadcasted_iota(jnp.int32, (tk, tq), 0)).astype(F32)
        lane = lax.broadcasted_iota(jnp.int32, (ATT_FILL, hd2), 1)
        consts = _lane_table(lane, [-c for c in c128] + [-c for c in c1], 0)
        eye = jnp.where(lax.broadcasted_iota(jnp.int32, (hd2, hd2), 0)
                        == lax.broadcasted_iota(jnp.int32, (hd2, hd2), 1), 1.0, 0.0).astype(BF16)

        def fill(ci, carry):
            r0 = pl.multiple_of(ci * ATT_FILL, ATT_FILL)
            pos = r0 + lax.broadcasted_iota(jnp.int32, (ATT_FILL, hd2), 0)
            hi_digit = jnp.right_shift(pos, 7).astype(F32)
            lo_digit = jnp.bitwise_and(pos, 127).astype(F32)
            aug = jnp.where(lane < 6, consts, jnp.where(lane < 9, hi_digit, jnp.where(lane < 12, lo_digit, 0.0)))
            kp_scr[pl.ds(r0, ATT_FILL), 0:hd2] = k_ref[pl.ds(r0, ATT_FILL), :]
            kp_scr[pl.ds(r0, ATT_FILL), hd2:2 * hd2] = aug.astype(BF16)
            vt_scr[:, pl.ds(r0, ATT_FILL)] = lax.dot_general(
                eye, v_ref[pl.ds(r0, ATT_FILL), :], NT_DIMS, preferred_element_type=F32).astype(BF16)
            return carry

        lax.fori_loop(0, length // ATT_FILL, fill, 0)

    q = q_ref[...]
    lane = lax.broadcasted_iota(jnp.int32, (tq, hd2), 1)
    zero = jnp.zeros_like(q)
    tpos = qi * tq + lax.broadcasted_iota(jnp.int32, (tq, hd2), 0)
    hi_digit = jnp.right_shift(tpos, 7).astype(F32)
    lo_digit = jnp.bitwise_and(tpos, 127).astype(F32)
    consts = _lane_table(lane, c128 + c1, 6)
    augq = jnp.where(lane < 3, hi_digit, jnp.where(lane < 6, lo_digit, consts))
    for var, aug in enumerate((augq.astype(BF16), (-augq).astype(BF16))):
        qq_scr[var, 0:tq, 0:hd2] = jnp.where(lane < HEAD_DIM, q, zero)
        qq_scr[var, tq:2 * tq, 0:hd2] = jnp.where(lane >= HEAD_DIM, q, zero)
        qq_scr[var, 0:tq, hd2:2 * hd2] = aug
        qq_scr[var, tq:2 * tq, hd2:2 * hd2] = aug

    m_scr[...] = jnp.full(m_scr.shape, -jnp.inf, F32)
    l_scr[...] = jnp.zeros(l_scr.shape, F32)
    acc_scr[...] = jnp.zeros(acc_scr.shape, F32)

    kd = (qi * tq) // tk

    def other_block(n):
        return jnp.where(n < n_other, jnp.where(n < kd, n, n + 1), kd)

    def scores(blk, s_scr):
        ks = pl.multiple_of(blk * tk, tk)
        var = (blk > kd).astype(jnp.int32)
        s_scr[...] = lax.dot_general(kp_scr[pl.ds(ks, tk), :], qq_scr[var], NT_DIMS, preferred_element_type=F32)

    def softmax_pv(blk, s_scr, diag):
        ks = pl.multiple_of(blk * tk, tk)
        cdiag = (qi * tq - blk * tk).astype(F32)
        vt = vt_scr[:, pl.ds(ks, tk)]
        m_all = m_scr[...]
        l_all = l_scr[...]
        m_out, l_out = [], []
        for cg in range(2 * tq // ATT_CW):
            cols = slice(cg * ATT_CW, (cg + 1) * ATT_CW)
            s = s_scr[:, cols]
            if diag:
                dc = (cg * ATT_CW) % tq
                s = s + jnp.minimum(d0_scr[:, dc:dc + ATT_CW] + cdiag, 0.0) * (2.0 * slope2)
            m_old = m_all[:, cols]
            m_new = jnp.maximum(m_old, jnp.max(s, axis=0, keepdims=True))
            p = jnp.exp2(s - m_new)
            alpha = jnp.exp2(m_old - m_new)
            l_out.append(alpha * l_all[:, cols] + jnp.sum(p, axis=0, keepdims=True))
            m_out.append(m_new)
            pv = jnp.dot(vt, p.astype(BF16), preferred_element_type=F32)
            acc_scr[:, cols] = alpha * acc_scr[:, cols] + pv
        m_scr[...] = jnp.concatenate(m_out, axis=1)
        l_scr[...] = jnp.concatenate(l_out, axis=1)

    scores(other_block(0), s0_scr)

    def pair(n2, carry):
        n = 2 * n2
        scores(other_block(n + 1), s1_scr)
        softmax_pv(other_block(n), s0_scr, False)
        scores(other_block(n + 2), s0_scr)
        softmax_pv(other_block(n + 1), s1_scr, False)
        return carry

    lax.fori_loop(0, n_other // 2, pair, 0)
    if n_other % 2:
        scores(kd, s1_scr)
        softmax_pv(other_block(n_other - 1), s0_scr, False)
        softmax_pv(kd, s1_scr, True)
    else:
        softmax_pv(kd, s0_scr, True)

    lam = lam_ref[...]
    lam_full = (jnp.exp(jnp.sum(lam[0:1] * lam[1:2], axis=-1, keepdims=True))
                - jnp.exp(jnp.sum(lam[2:3] * lam[3:4], axis=-1, keepdims=True)) + lam_init)
    ot = acc_scr[...] / l_scr[...]
    ot = ot[:, :tq] - lam_full * ot[:, tq:]
    ms = jnp.mean(ot * ot, axis=0, keepdims=True)
    ot = ot * lax.rsqrt(ms + LN_EPS) * g_ref[...] * (1.0 - lam_init)
    o_ref[...] = jnp.transpose(ot).astype(o_ref.dtype)


def diff_attention(qkv, lam, subln_g, *, lam_init, tk=ATT_TK):
    b, l, _ = qkv.shape
    tq = ATT_TQ_SHORT if l == 2 * tk else ATT_TQ
    assert l % tk == 0 and l // tk >= 2 and tk % tq == 0 and l <= 128 * 128
    hd2 = 2 * HEAD_DIM
    return pl.pallas_call(
        functools.partial(_attn_body, tq=tq, tk=tk, lam_init=lam_init),
        grid=(b, N_HEADS, l // tq),
        in_specs=[pl.BlockSpec((4, HEAD_DIM), lambda bi, h, qi: (0, 0)),
                  pl.BlockSpec((hd2, 1), lambda bi, h, qi: (0, 0)),
                  pl.BlockSpec((None, tq, hd2), lambda bi, h, qi: (bi, qi, h)),
                  pl.BlockSpec((None, l, hd2), lambda bi, h, qi: (bi, 0, N_HEADS + h)),
                  pl.BlockSpec((None, l, hd2), lambda bi, h, qi: (bi, 0, 2 * N_HEADS + h))],
        out_specs=pl.BlockSpec((None, tq, hd2), lambda bi, h, qi: (bi, qi, h)),
        out_shape=jax.ShapeDtypeStruct((b, l, D_MODEL), BF16),
        scratch_shapes=[pltpu.VMEM((l, 2 * hd2), BF16), pltpu.VMEM((hd2, l), BF16), pltpu.VMEM((tk, tq), F32),
                        pltpu.VMEM((2, 2 * tq, 2 * hd2), BF16),
                        pltpu.VMEM((tk, 2 * tq), F32), pltpu.VMEM((tk, 2 * tq), F32),
                        pltpu.VMEM((1, 2 * tq), F32), pltpu.VMEM((1, 2 * tq), F32),
                        pltpu.VMEM((hd2, 2 * tq), F32)],
        compiler_params=_cp(("parallel", "parallel", "arbitrary"), 56),
        name="diff_attention",
    )(lam, subln_g, qkv, qkv, qkv)


def _route(x1, wr, br, comb_ref):
    lt = lax.dot_general(wr, x1, NT_DIMS, precision=HIGHEST, preferred_element_type=F32) + br
    gl = [lt[g:g + 1] for g in range(N_GROUPS)]
    gmax = jnp.maximum(jnp.maximum(gl[0], gl[1]), jnp.maximum(gl[2], gl[3]))
    gidx = jnp.where(gl[0] == gmax, 0, jnp.where(gl[1] == gmax, 1, jnp.where(gl[2] == gmax, 2, 3)))
    gw = 1.0 / (jnp.exp(gl[0] - gmax) + jnp.exp(gl[1] - gmax) + jnp.exp(gl[2] - gmax) + jnp.exp(gl[3] - gmax))
    el = []
    for j in range(EXPERTS_PER_GROUP):
        acc = jnp.zeros_like(gmax)
        for g in range(N_GROUPS):
            r = N_GROUPS + g * EXPERTS_PER_GROUP + j
            acc = jnp.where(gidx == g, lt[r:r + 1], acc)
        el.append(acc)
    v1 = jnp.maximum(jnp.maximum(el[0], el[1]), jnp.maximum(el[2], el[3]))
    i1 = jnp.where(el[0] == v1, 0, jnp.where(el[1] == v1, 1, jnp.where(el[2] == v1, 2, 3)))
    neg = jnp.full_like(v1, -jnp.inf)
    el2 = [jnp.where(i1 == j, neg, el[j]) for j in range(EXPERTS_PER_GROUP)]
    v2 = jnp.maximum(jnp.maximum(el2[0], el2[1]), jnp.maximum(el2[2], el2[3]))
    i2 = jnp.where(el2[0] == v2, 0, jnp.where(el2[1] == v2, 1, jnp.where(el2[2] == v2, 2, 3)))
    e21 = jnp.exp(v2 - v1)
    w1 = gw / (1.0 + e21)
    w2 = gw * e21 / (1.0 + e21)
    zero = jnp.zeros_like(v1)
    for g in range(N_GROUPS):
        for j in range(EXPERTS_PER_GROUP):
            wj = jnp.where(i1 == j, w1, jnp.where(i2 == j, w2, zero))
            r = g * EXPERTS_PER_GROUP + j
            comb_ref[r:r + 1, :] = jnp.where(gidx == g, wj, zero)


def _proj_ln_route_body(a_ref, w_ref, b_ref, x_ref, g_ref, beta_ref, wr_ref, br_ref, x1_ref, comb_ref):
    h = jnp.dot(a_ref[...].astype(BF16), w_ref[...], preferred_element_type=F32) + b_ref[...]
    x1 = _layer_norm(ALPHA * x_ref[...] + h, g_ref[...], beta_ref[...])
    x1_ref[...] = x1
    _route(x1, wr_ref[...], br_ref[...], comb_ref)


def proj_ln_route(a, w, b, x, ln_g, ln_b, wr, br, *, tm=512):
    t, k = a.shape
    d = D_MODEL
    const = lambda i: (0, 0)
    return pl.pallas_call(
        _proj_ln_route_body,
        grid=(t // tm,),
        in_specs=[pl.BlockSpec((tm, k), lambda i: (i, 0)),
                  pl.BlockSpec((k, d), const), pl.BlockSpec((1, d), const),
                  pl.BlockSpec((tm, d), lambda i: (i, 0)),
                  pl.BlockSpec((1, d), const), pl.BlockSpec((1, d), const),
                  pl.BlockSpec((ROUTE_ROWS, d), const), pl.BlockSpec((ROUTE_ROWS, 1), const)],
        out_specs=[pl.BlockSpec((tm, d), lambda i: (i, 0)),
                   pl.BlockSpec((N_EXPERTS, tm), lambda i: (0, i))],
        out_shape=[jax.ShapeDtypeStruct((t, d), F32), jax.ShapeDtypeStruct((N_EXPERTS, t), F32)],
        compiler_params=_cp(("parallel",), 48),
        name="proj_ln_route",
    )(a, w, b, x, ln_g, ln_b, wr, br)


def _moe_body(x1_ref, comb_ref, uw_ref, lw_ref, wg_ref, wu_ref, wd_ref, g2_ref, b2_ref, p_ref, up_ref, gate_ref,
              o_ref, vrow_scr, vcol_scr, xg_scr, wsl_scr, fast_scr, *, tm, st, caps):
    e = pl.program_id(1)
    n_sub = tm // st

    def sub_rows(sub):
        return slice(sub * st, (sub + 1) * st)

    @pl.when(e == 0)
    def _():
        member = jnp.where(comb_ref[...] > 0.0, 1.0, 0.0).astype(BF16)
        for sub in range(n_sub):
            ms = member[:, sub_rows(sub)]
            vrow_scr[:, sub_rows(sub)] = jnp.dot(ms, uw_ref[...], preferred_element_type=F32)
            vcol_scr[sub_rows(sub), :] = lax.dot_general(lw_ref[...], ms, NT_DIMS, preferred_element_type=F32)
        most = ((jnp.max(vrow_scr[...]) + 1.0) * 0.5).astype(jnp.int32)
        fast_scr[0] = (most <= caps).astype(jnp.int32)
        o_ref[...] = jnp.zeros(o_ref.shape, F32)

        @pl.when(most <= caps)
        def _():
            tgt = 2.0 * lax.broadcasted_iota(jnp.int32, (caps, 1), 0).astype(F32) + 1.0
            for sub in range(n_sub):
                vr = vrow_scr[:, sub_rows(sub)]
                cw = comb_ref[:, sub_rows(sub)]
                blocks = []
                for ee in range(N_EXPERTS):
                    hit = vr[ee:ee + 1, :] == tgt
                    blocks.append(jnp.where(hit, 1.0, 0.0).astype(BF16))
                    wsl_scr[ee, sub * caps:(sub + 1) * caps, :] = jnp.sum(
                        jnp.where(hit, cw[ee:ee + 1, :], 0.0), axis=1, keepdims=True)
                gather = jnp.concatenate(blocks, axis=0)
                xg = jnp.dot(gather, x1_ref[sub_rows(sub), :].astype(BF16),
                             preferred_element_type=F32).astype(BF16)
                for ee in range(N_EXPERTS):
                    xg_scr[ee, sub * caps:(sub + 1) * caps, :] = xg[ee * caps:(ee + 1) * caps]

    fast = fast_scr[0] == 1

    def expert_ffn(xg, wslot):
        hg = jnp.dot(xg, wg_ref[...], preferred_element_type=F32)
        hu = jnp.dot(xg, wu_ref[...], preferred_element_type=F32)
        hid = hg * _sigmoid(hg) * hu * wslot
        return jnp.dot(hid.astype(BF16), wd_ref[...], preferred_element_type=F32).astype(BF16)

    sel = lax.broadcasted_iota(jnp.int32, (st, N_EXPERTS), 1) == e

    @pl.when(fast)
    def _():
        ob = expert_ffn(xg_scr[e], wsl_scr[e])
        tgt_c = 2.0 * lax.broadcasted_iota(jnp.int32, (1, caps), 1).astype(F32) + 1.0
        for sub in range(n_sub):
            vcol = jnp.sum(jnp.where(sel, vcol_scr[sub_rows(sub), :], 0.0), axis=1, keepdims=True)
            scatter = jnp.where(vcol == tgt_c, 1.0, 0.0).astype(BF16)
            o_ref[sub_rows(sub), :] += jnp.dot(scatter, ob[sub * caps:(sub + 1) * caps],
                                               preferred_element_type=F32)

    @pl.when(jnp.logical_not(fast))
    def _():
        prow = lax.broadcasted_iota(jnp.int32, (8, st), 0)
        for sub in range(n_sub):
            vrow = vrow_scr[pl.ds(e, 1), sub_rows(sub)]
            vcol = jnp.sum(jnp.where(sel, vcol_scr[sub_rows(sub), :], 0.0), axis=1, keepdims=True)
            w_hi, w_mid, w_lo = _split3(comb_ref[pl.ds(e, 1), sub_rows(sub)])
            w3 = jnp.where(prow == 0, w_hi, jnp.where(prow == 1, w_mid, jnp.where(prow == 2, w_lo, 0.0))
                           ).astype(BF16)
            count = ((jnp.max(vrow) + 1.0) * 0.5).astype(jnp.int32)
            xsub = x1_ref[sub_rows(sub), :].astype(BF16)

            def chunk(c, carry):
                base = (c * caps).astype(F32)
                tgt_r = 2.0 * (base + lax.broadcasted_iota(jnp.int32, (caps, 1), 0).astype(F32)) + 1.0
                tgt_c = 2.0 * (base + lax.broadcasted_iota(jnp.int32, (1, caps), 1).astype(F32)) + 1.0
                gather = jnp.where(vrow == tgt_r, 1.0, 0.0).astype(BF16)
                scatter = jnp.where(vcol == tgt_c, 1.0, 0.0).astype(BF16)
                xg = jnp.dot(gather, xsub, preferred_element_type=F32).astype(BF16)
                ws = lax.dot_general(gather, w3, NT_DIMS, preferred_element_type=F32)
                ob = expert_ffn(xg, ws[:, 0:1] + ws[:, 1:2] + ws[:, 2:3])
                o_ref[sub_rows(sub), :] += jnp.dot(scatter, ob, preferred_element_type=F32)
                return carry

            lax.fori_loop(0, (count + (caps - 1)) // caps, chunk, 0)

    @pl.when(e == N_EXPERTS - 1)
    def _():
        x2 = _layer_norm(ALPHA * x1_ref[...] + o_ref[...], g2_ref[...], b2_ref[...])
        up = jnp.dot(p_ref[...].astype(BF16), up_ref[...], preferred_element_type=F32)
        gt = jnp.dot(x2.astype(BF16), gate_ref[...], preferred_element_type=F32)
        o_ref[...] = x2 + up * _sigmoid(gt)


def moe_ple(x1, comb, uw, lw, wg, wu, wd, ln_g, ln_b, p, up, gate, *, tm=MOE_TM, st=MOE_ST, caps=MOE_CAPS):
    t, d = x1.shape
    ff = EXPERT_FF
    n_sub = tm // st
    assert 2 * caps == 128 and st % 128 == 0
    const = lambda i, e: (0, 0)
    once = pl.Buffered(1)
    return pl.pallas_call(
        functools.partial(_moe_body, tm=tm, st=st, caps=caps),
        grid=(t // tm, N_EXPERTS),
        in_specs=[pl.BlockSpec((tm, d), lambda i, e: (i, 0)),
                  pl.BlockSpec((N_EXPERTS, tm), lambda i, e: (0, i)),
                  pl.BlockSpec((st, st), const), pl.BlockSpec((st, st), const),
                  pl.BlockSpec((None, d, ff), lambda i, e: (e, 0, 0)),
                  pl.BlockSpec((None, d, ff), lambda i, e: (e, 0, 0)),
                  pl.BlockSpec((None, ff, d), lambda i, e: (e, 0, 0)),
                  pl.BlockSpec((1, d), const), pl.BlockSpec((1, d), const),
                  pl.BlockSpec((tm, PLE_DIM), lambda i, e: (i, 0), pipeline_mode=once),
                  pl.BlockSpec((PLE_DIM, d), const, pipeline_mode=once),
                  pl.BlockSpec((d, d), const, pipeline_mode=once)],
        out_specs=pl.BlockSpec((tm, d), lambda i, e: (i, 0)),
        out_shape=jax.ShapeDtypeStruct((t, d), F32),
        scratch_shapes=[pltpu.VMEM((N_EXPERTS, tm), F32), pltpu.VMEM((tm, N_EXPERTS), F32),
                        pltpu.VMEM((N_EXPERTS, n_sub * caps, d), BF16),
                        pltpu.VMEM((N_EXPERTS, n_sub * caps, 1), F32),
                        pltpu.SMEM((1,), jnp.int32)],
        compiler_params=_cp(("parallel", "arbitrary"), 58),
        name="moe_ple",
    )(x1, comb, uw, lw, wg, wu, wd, ln_g, ln_b, p, up, gate)


def _dwconv_body(prev_ref, cur_ref, next_ref, w_ref, b_ref, g_ref, beta_ref, o_ref, buf_scr, sh_scr,
                 *, width, tm, norm_act):
    i = pl.program_id(1)
    n = pl.num_programs(1)
    pad = width // 2
    halo = jnp.zeros(prev_ref.shape, F32)
    buf_scr[0:HALO, :] = jnp.where(i > 0, prev_ref[...], halo)
    buf_scr[HALO:HALO + tm, :] = cur_ref[...]
    buf_scr[HALO + tm:2 * HALO + tm, :] = jnp.where(i < n - 1, next_ref[...], halo)
    offs = [HALO - pad + k for k in range(width)]
    n_sh = tm + 2 * HALO - 8
    for sft in sorted({off % 8 for off in offs}):
        sh_scr[sft, 0:n_sh, :] = buf_scr[sft:sft + n_sh, :]

    def rows(r, carry):
        r0 = pl.multiple_of(r * CONV_ROWS, CONV_ROWS)
        acc = jnp.zeros((CONV_ROWS, o_ref.shape[1]), F32) + b_ref[...]
        for k, off in enumerate(offs):
            start = pl.multiple_of(r0 + (off // 8) * 8, 8)
            acc = acc + sh_scr[off % 8, pl.ds(start, CONV_ROWS), :] * w_ref[k:k + 1, :]
        if norm_act:
            acc = _layer_norm(acc, g_ref[...], beta_ref[...])
            acc = acc * _sigmoid(acc)
        o_ref[pl.ds(r0, CONV_ROWS), :] = acc.astype(o_ref.dtype)
        return carry

    lax.fori_loop(0, tm // CONV_ROWS, rows, 0)


def dwconv(x, w, b, ln_g, ln_b, *, norm_act, out_dtype, tm=256):
    bsz, l, c = x.shape
    width = w.shape[0]
    tc = D_MODEL
    nb = tm // HALO
    last = l // HALO - 1
    return pl.pallas_call(
        functools.partial(_dwconv_body, width=width, tm=tm, norm_act=norm_act),
        grid=(bsz, l // tm, c // tc),
        in_specs=[pl.BlockSpec((None, HALO, tc), lambda bi, i, j: (bi, jnp.maximum(i * nb - 1, 0), j)),
                  pl.BlockSpec((None, tm, tc), lambda bi, i, j: (bi, i, j)),
                  pl.BlockSpec((None, HALO, tc), lambda bi, i, j: (bi, jnp.minimum((i + 1) * nb, last), j)),
                  pl.BlockSpec((width, tc), lambda bi, i, j: (0, j)),
                  pl.BlockSpec((1, tc), lambda bi, i, j: (0, j)),
                  pl.BlockSpec((1, tc), lambda bi, i, j: (0, 0)),
                  pl.BlockSpec((1, tc), lambda bi, i, j: (0, 0))],
        out_specs=pl.BlockSpec((None, None, tm, tc), lambda bi, i, j: (j, bi, i, 0)),
        out_shape=jax.ShapeDtypeStruct((c // tc, bsz, l, tc), out_dtype),
        scratch_shapes=[pltpu.VMEM((tm + 2 * HALO, tc), F32), pltpu.VMEM((8, tm + 2 * HALO, tc), F32)],
        compiler_params=_cp(("parallel", "parallel", "parallel"), 32),
        name="dwconv%d" % width,
    )(x, x, x, w, b, ln_g, ln_b)


def _filter_body(w1t_ref, w1c_ref, w1s_ref, b1_ref, w2_ref, b2_ref, w3_ref, b3_ref, freq_ref, wout_ref,
                 o_ref, *, tl, length):
    i = pl.program_id(0)
    pos = (i * tl + lax.broadcasted_iota(jnp.int32, (tl, 1), 0)).astype(F32)
    t = pos / float(length - 1)
    ang = (2.0 * math.pi / length) * pos
    band = lax.broadcasted_iota(jnp.int32, (1, POS_BANDS), 1).astype(F32)
    bands = 1e-4 + band * ((POS_BANDS - 1 - 1e-4) / (POS_BANDS - 1))
    fw = bands * ang
    freq = freq_ref[...]
    dot = functools.partial(jnp.dot, precision=HIGHEST, preferred_element_type=F32)
    pre = t * w1t_ref[...] + dot(jnp.cos(fw), w1c_ref[...]) - dot(jnp.sin(fw), w1s_ref[...]) + b1_ref[...]
    hcur = jnp.sin(freq * pre)
    hcur = jnp.sin(freq * (dot(hcur, w2_ref[...]) + b2_ref[...]))
    hcur = jnp.sin(freq * (dot(hcur, w3_ref[...]) + b3_ref[...]))
    ch = lax.broadcasted_iota(jnp.int32, (1, D_MODEL), 1).astype(F32)
    deltas = jnp.abs(MIN_DECAY + ch * ((MAX_DECAY - MIN_DECAY) / (D_MODEL - 1)))
    decay = jnp.exp(-t * deltas)
    for j in range(2 * HYENA_ORDER):
        o_ref[j] = dot(hcur, wout_ref[:, j * D_MODEL:(j + 1) * D_MODEL]) * decay


def hyena_filters(length, w1, b1, w2, b2, w3, b3, freq, wout, *, tl=256):
    fwid = FILTER_WIDTH
    const = lambda i: (0, 0)
    nf = 2 * HYENA_ORDER
    return pl.pallas_call(
        functools.partial(_filter_body, tl=tl, length=length),
        grid=(length // tl,),
        in_specs=[pl.BlockSpec((1, fwid), const), pl.BlockSpec((POS_BANDS, fwid), const),
                  pl.BlockSpec((POS_BANDS, fwid), const), pl.BlockSpec((1, fwid), const),
                  pl.BlockSpec((fwid, fwid), const), pl.BlockSpec((1, fwid), const),
                  pl.BlockSpec((fwid, fwid), const), pl.BlockSpec((1, fwid), const),
                  pl.BlockSpec((1, fwid), const), pl.BlockSpec((fwid, nf * D_MODEL), const)],
        out_specs=pl.BlockSpec((nf, tl, D_MODEL), lambda i: (0, i, 0)),
        out_shape=jax.ShapeDtypeStruct((nf, length, D_MODEL), F32),
        compiler_params=_cp(("parallel",), 32),
        name="hyena_filters",
    )(w1[0:1], w1[1:1 + POS_BANDS], w1[1 + POS_BANDS:], b1, w2, b2, w3, b3, freq, wout)


def _dft_tables(n1, n2s):
    n = n1 * n2s
    m1 = (n1 // 2 + 1 + 7) // 8 * 8
    k1 = jnp.arange(m1, dtype=jnp.int32)
    valid = (k1 <= n1 // 2)
    a = jnp.arange(n1 // 2, dtype=jnp.int32)
    ang_a = (2.0 * math.pi / n1) * ((k1[:, None] * a[None, :]) % n1).astype(F32)
    ca = jnp.where(valid[:, None], jnp.cos(ang_a), 0.0)
    sa = jnp.where(valid[:, None], jnp.sin(ang_a), 0.0)
    g_a = jnp.concatenate([ca, -sa], axis=0)
    g_ai = jnp.concatenate([ca.T, -sa.T], axis=1)
    weight = jnp.where(valid, jnp.where((k1 == 0) | (k1 == n1 // 2), 1.0, 2.0), 0.0)
    b = jnp.arange(n2s, dtype=jnp.int32)
    k2 = jnp.arange(n2s, dtype=jnp.int32)
    idx = (b[None, None, :] * (k2[None, :, None] * n1 + k1[:, None, None])) % n
    ang_c = (2.0 * math.pi / n) * idx.astype(F32)
    cc, sc = jnp.cos(ang_c), jnp.sin(ang_c)
    g_c = jnp.concatenate([jnp.concatenate([cc, sc], axis=2),
                           jnp.concatenate([-sc, cc], axis=2)], axis=1)
    cct = jnp.swapaxes(cc, 1, 2) * (weight / n)[:, None, None]
    sct = jnp.swapaxes(sc, 1, 2) * (weight / n)[:, None, None]
    g_ci = jnp.concatenate([jnp.concatenate([cct, -sct], axis=2),
                            jnp.concatenate([sct, cct], axis=2)], axis=1)
    return m1, g_a.astype(BF16), g_c.astype(BF16), g_ci.astype(BF16), g_ai.astype(BF16)


def _dft_dot(g_ref, x):
    return jnp.dot(g_ref[...], x.astype(BF16), preferred_element_type=F32)


def _dft_a_body(g_ref, x_ref, o_ref, *, nb, m1):
    for bb in range(nb):
        res = _dft_dot(g_ref, x_ref[:, bb, :])
        o_ref[:, 0, bb, :] = res[:m1]
        o_ref[:, 1, bb, :] = res[m1:]


def _dft_slab_body(*refs, bt, spec):
    if spec:
        g1_ref, g2_ref, x_ref, hf_ref, hb_ref, o_ref = refs
        half = g1_ref.shape[0] // 2
        hr = hf_ref[:half] + hb_ref[:half]
        hi = hf_ref[half:] - hb_ref[half:]
    else:
        g1_ref, x_ref, o_ref = refs
    for bi in range(bt):
        xs = _dft_dot(g1_ref, x_ref[bi])
        if spec:
            xr, xi = xs[:half], xs[half:]
            ys = jnp.concatenate([xr * hr - xi * hi, xr * hi + xi * hr], axis=0)
            xs = _dft_dot(g2_ref, ys)
        o_ref[bi] = xs


def _idft_gate_body(g_ref, x_ref, v_ref, gate_ref, bias_ref, o_ref, *, nb):
    for bb in range(nb):
        spec = jnp.concatenate([x_ref[:, 0, bb, :], x_ref[:, 1, bb, :]], axis=0)
        y = _dft_dot(g_ref, spec)
        o_ref[:, bb, :] = gate_ref[:, bb, :] * (y + v_ref[:, bb, :] * bias_ref[...])


def _dft_a(g, x, xsel, *, m1, nb, dt):
    _, bsz, half, n2s, d = x.shape
    return pl.pallas_call(
        functools.partial(_dft_a_body, nb=nb, m1=m1),
        grid=(bsz, n2s // nb, d // dt),
        in_specs=[pl.BlockSpec((2 * m1, half), lambda bi, bj, di: (0, 0)),
                  pl.BlockSpec((None, None, half, nb, dt), lambda bi, bj, di: (xsel, bi, 0, bj, di))],
        out_specs=pl.BlockSpec((None, m1, 2, nb, dt), lambda bi, bj, di: (bi, 0, 0, bj, di)),
        out_shape=jax.ShapeDtypeStruct((bsz, m1, 2, n2s, d), F32),
        compiler_params=_cp(("parallel", "parallel", "parallel"), 48),
        name="dft_a",
    )(g, x)


def _dft_slab(g1, g2, x, hspec, order, *, bt):
    bsz, s, r, d = x.shape
    spec = g2 is not None
    gspec = pl.BlockSpec((None, r, r), lambda si, bi: (si, 0, 0))
    xspec = pl.BlockSpec((bt, None, r, d), lambda si, bi: (bi, si, 0, 0))
    if spec:
        in_specs = [gspec, gspec, xspec,
                    pl.BlockSpec((None, None, r, d), lambda si, bi: (2 * order, si, 0, 0)),
                    pl.BlockSpec((None, None, r, d), lambda si, bi: (2 * order + 1, si, 0, 0))]
        args = (g1, g2, x, hspec, hspec)
    else:
        in_specs, args = [gspec, xspec], (g1, x)
    return pl.pallas_call(
        functools.partial(_dft_slab_body, bt=bt, spec=spec),
        grid=(s, bsz // bt),
        in_specs=in_specs,
        out_specs=xspec,
        out_shape=jax.ShapeDtypeStruct((bsz, s, r, d), F32),
        compiler_params=_cp(("parallel", "parallel"), 48),
        name="dft_spec" if spec else "dft_c",
    )(*args)


def _idft_gate(g, x, v, vsel, gates, gsel, bias, *, nb, dt):
    bsz, m1, _, n2s, d = x.shape
    half = g.shape[0]
    tspec = lambda sel: pl.BlockSpec((None, None, half, nb, dt), lambda bi, bj, di: (sel, bi, 0, bj, di))
    return pl.pallas_call(
        functools.partial(_idft_gate_body, nb=nb),
        grid=(bsz, n2s // nb, d // dt),
        in_specs=[pl.BlockSpec((half, 2 * m1), lambda bi, bj, di: (0, 0)),
                  pl.BlockSpec((None, m1, 2, nb, dt), lambda bi, bj, di: (bi, 0, 0, bj, di)),
                  tspec(vsel), tspec(gsel),
                  pl.BlockSpec((1, dt), lambda bi, bj, di: (0, di))],
        out_specs=pl.BlockSpec((None, half, nb, dt), lambda bi, bj, di: (bi, 0, bj, di)),
        out_shape=jax.ShapeDtypeStruct((bsz, half, n2s, d), F32),
        compiler_params=_cp(("parallel", "parallel", "parallel"), 48),
        name="idft_gate",
    )(g, x, v, gates, bias)


def _fft_plan(length):
    n = 2 * length
    n1 = 256 if n >= 32768 else 64
    return n1, n // n1


def hyena_long_convs(xs, filt, f_bias):
    _, bsz, length, d = xs.shape
    n1, n2s = _fft_plan(length)
    m1, g_a, g_c, g_ci, g_ai = _dft_tables(n1, n2s)
    half = n1 // 2
    big = length >= 8192
    bt = 1 if big else min(8, bsz)
    nb = 8 if big else 16
    dt = d // 2 if big else d
    fa = _dft_a(g_a, filt.reshape(1, 4, half, n2s, d), 0, m1=m1, nb=nb, dt=dt)
    hspec = _dft_slab(g_c, None, fa.reshape(4, m1, 2 * n2s, d), None, 0, bt=1)
    xs6 = xs.reshape(3, bsz, half, n2s, d)
    z, zsel = xs6, 2
    out = None
    for order in range(HYENA_ORDER):
        sa = _dft_a(g_a, z, zsel, m1=m1, nb=nb, dt=dt).reshape(bsz, m1, 2 * n2s, d)
        sb = _dft_slab(g_c, g_ci, sa, hspec, order, bt=bt).reshape(bsz, m1, 2, n2s, d)
        out = _idft_gate(g_ai, sb, z, zsel, xs6, order, f_bias[order:order + 1], nb=nb, dt=dt)
        z, zsel = out[None], 0
    return out.reshape(bsz, length, d)


def _prefix_tables(tm):
    r = jnp.arange(tm, dtype=jnp.int32)
    upper = jnp.where(r[:, None] < r[None, :], 2.0, jnp.where(r[:, None] == r[None, :], 1.0, 0.0))
    return upper.astype(BF16), upper.T.astype(BF16)


def _trunk(x, p, wts):
    bsz, length, d = x.shape
    t = bsz * length
    xf = x.reshape(t, d)
    zeros_d = jnp.zeros((1, d), F32)
    ones_d = jnp.ones((1, d), F32)
    uw, lw = _prefix_tables(MOE_ST)
    for i in range(DEPTH):
        j, kind = i // N_MIXERS, i % N_MIXERS
        if kind == 0:
            lam_init = 0.8 - 0.6 * math.exp(-0.3 * i)
            qkv = linear(xf, wts["attn_w_qkv"][j], jnp.zeros((1, 3 * d), F32), out_dtype=BF16, name="attn_qkv")
            a = diff_attention(qkv.reshape(bsz, length, 3 * d), wts["attn_lam"][j], wts["attn_subln_g"][j],
                               lam_init=lam_init).reshape(t, d)
            w_o, b_o = wts["attn_w_o"][j], zeros_d
        elif kind == 1:
            hglu = linear(xf, wts["conv_w_pw1"][j], wts["conv_b_pw1"][j], out_dtype=F32, glu=True, name="conv_pw1")
            a = dwconv(hglu.reshape(bsz, length, d), wts["conv_w_dw"][j], wts["conv_b_dw"][j],
                       wts["conv_ln_g"][j], wts["conv_ln_b"][j], norm_act=True, out_dtype=BF16).reshape(t, d)
            w_o, b_o = wts["conv_w_pw2"][j], wts["conv_b_pw2"][j]
        else:
            u = linear(xf, wts["hy_w_in"][j], wts["hy_b_in"][j], out_dtype=F32, name="hyena_in")
            xs = dwconv(u.reshape(bsz, length, 3 * d), wts["hy_w_short"][j], wts["hy_b_short"][j],
                        ones_d, zeros_d, norm_act=False, out_dtype=F32)
            filt = hyena_filters(length, wts["hy_f_w1"][j], wts["hy_f_b1"][j], wts["hy_f_w2"][j], wts["hy_f_b2"][j],
                                 wts["hy_f_w3"][j], wts["hy_f_b3"][j], wts["hy_f_freq"][j], wts["hy_f_wout"][j])
            a = hyena_long_convs(xs, filt, wts["hy_f_bias"][j]).reshape(t, d)
            w_o, b_o = wts["hy_w_out"][j], wts["hy_b_out"][j]
        x1, comb = proj_ln_route(a, w_o, b_o, xf, wts["ln1_g"][i], wts["ln1_b"][i],
                                 wts["route_w"][i], wts["route_b"][i])
        xf = moe_ple(x1, comb, uw, lw, wts["moe_w_gate"][i], wts["moe_w_up"][i], wts["moe_w_down"][i],
                     wts["ln2_g"][i], wts["ln2_b"][i], p[i].reshape(t, PLE_DIM),
                     wts["ple_w_up"][i], wts["ple_w_gate"][i])
    return xf.reshape(bsz, length, d)


def kernel(x_prompt, x_sample, p_prompt, p_sample, attn_w_qkv, attn_w_o, attn_lam_q1, attn_lam_k1, attn_lam_q2, attn_lam_k2, attn_subln_g, conv_w_pw1, conv_b_pw1, conv_w_dw, conv_b_dw, conv_ln_g, conv_ln_b, conv_w_pw2, conv_b_pw2, hy_w_in, hy_b_in, hy_w_short, hy_b_short, hy_f_w1, hy_f_b1, hy_f_w2, hy_f_b2, hy_f_w3, hy_f_b3, hy_f_freq, hy_f_wout, hy_f_bias, hy_w_out, hy_b_out, ln1_g, ln1_b, ln2_g, ln2_b, moe_w_group, moe_b_group, moe_w_expert, moe_b_expert, moe_w_gate, moe_w_up, moe_w_down, ple_w_up, ple_w_gate):
    d = D_MODEL
    row = lambda a: a[:, None, :]
    q_scale = jnp.concatenate([jnp.full((d,), HEAD_DIM ** -0.5 * LOG2E, F32), jnp.ones((2 * d,), F32)])
    route_w = jnp.concatenate([jnp.swapaxes(moe_w_group, 1, 2), jnp.swapaxes(moe_w_expert, 1, 2),
                               jnp.zeros((DEPTH, ROUTE_ROWS - N_GROUPS - N_EXPERTS, d), F32)], axis=1)
    route_b = jnp.concatenate([moe_b_group, moe_b_expert,
                               jnp.zeros((DEPTH, ROUTE_ROWS - N_GROUPS - N_EXPERTS), F32)], axis=1)[:, :, None]
    wts = {
        "attn_w_qkv": (attn_w_qkv * q_scale).astype(BF16),
        "attn_w_o": attn_w_o.astype(BF16),
        "attn_lam": jnp.stack([attn_lam_q1, attn_lam_k1, attn_lam_q2, attn_lam_k2], axis=1),
        "attn_subln_g": attn_subln_g[:, :, None],
        "conv_w_pw1": conv_w_pw1.astype(BF16), "conv_b_pw1": row(conv_b_pw1),
        "conv_w_dw": conv_w_dw, "conv_b_dw": row(conv_b_dw),
        "conv_ln_g": row(conv_ln_g), "conv_ln_b": row(conv_ln_b),
        "conv_w_pw2": conv_w_pw2.astype(BF16), "conv_b_pw2": row(conv_b_pw2),
        "hy_w_in": hy_w_in.astype(BF16), "hy_b_in": row(hy_b_in),
        "hy_w_short": hy_w_short, "hy_b_short": row(hy_b_short),
        "hy_f_w1": hy_f_w1, "hy_f_b1": row(hy_f_b1), "hy_f_w2": hy_f_w2, "hy_f_b2": row(hy_f_b2),
        "hy_f_w3": hy_f_w3, "hy_f_b3": row(hy_f_b3), "hy_f_freq": row(hy_f_freq), "hy_f_wout": hy_f_wout,
        "hy_f_bias": hy_f_bias,
        "hy_w_out": hy_w_out.astype(BF16), "hy_b_out": row(hy_b_out),
        "ln1_g": row(ln1_g), "ln1_b": row(ln1_b), "ln2_g": row(ln2_g), "ln2_b": row(ln2_b),
        "route_w": route_w, "route_b": route_b,
        "moe_w_gate": moe_w_gate.astype(BF16), "moe_w_up": moe_w_up.astype(BF16),
        "moe_w_down": moe_w_down.astype(BF16),
        "ple_w_up": ple_w_up.astype(BF16), "ple_w_gate": ple_w_gate.astype(BF16),
    }
    y_prompt = _trunk(x_prompt, p_prompt, wts)
    y_sample = _trunk(x_sample, p_sample, wts)
    return (y_prompt, y_sample)
```

```python
import functools
import math

import jax
import jax.numpy as jnp
from jax import lax
from jax.experimental import pallas as pl
from jax.experimental.pallas import tpu as pltpu

F32 = jnp.float32
BF16 = jnp.bfloat16
HIGHEST = lax.Precision.HIGHEST

D_MODEL = 1024
DEPTH = 4
N_MIXERS = 3
N_HEADS = 8
HEAD_DIM = 64
CONV_WIDTH = 31
HYENA_ORDER = 2
SHORT_WIDTH = 3
POS_BANDS = 16
FILTER_WIDTH = 64
MAX_DECAY = math.log(1e-2) / 0.3
MIN_DECAY = math.log(1e-2) / 1.5
N_GROUPS = 4
EXPERTS_PER_GROUP = 4
N_EXPERTS = N_GROUPS * EXPERTS_PER_GROUP
EXPERT_FF = 512
PLE_DIM = 256
ALPHA = (2 * DEPTH) ** 0.25
LN_EPS = 1e-5
LOG2E = 1.4426950408889634

HALO = 16
CONV_ROWS = 32
ROUTE_ROWS = 24
MOE_TM = 1024
MOE_ST = 256
MOE_CAPS = 64
ATT_TQ = 512
ATT_TQ_SHORT = 1024
ATT_TK = 1024
ATT_CW = 256
ATT_FILL = 512
NT_DIMS = (((1,), (1,)), ((), ()))


def _cp(sem, vmem_mb):
    return pltpu.CompilerParams(dimension_semantics=sem, vmem_limit_bytes=vmem_mb << 20)


def _layer_norm(z, g, b):
    mu = jnp.mean(z, axis=-1, keepdims=True)
    zc = z - mu
    var = jnp.mean(zc * zc, axis=-1, keepdims=True)
    return zc * lax.rsqrt(var + LN_EPS) * g + b


def _sigmoid(x):
    return 1.0 / (1.0 + jnp.exp(-x))


def _linear_body(x_ref, w_ref, b_ref, o_ref, *, nc, glu):
    x = x_ref[...].astype(BF16)
    n_out = o_ref.shape[-1]
    for n0 in range(0, n_out, nc):
        a = jnp.dot(x, w_ref[:, n0:n0 + nc], preferred_element_type=F32) + b_ref[:, n0:n0 + nc]
        if glu:
            g = (jnp.dot(x, w_ref[:, n_out + n0:n_out + n0 + nc], preferred_element_type=F32)
                 + b_ref[:, n_out + n0:n_out + n0 + nc])
            a = a * _sigmoid(g)
        o_ref[:, n0:n0 + nc] = a.astype(o_ref.dtype)


def linear(x, w, b, *, out_dtype, glu=False, tm=512, nc=512, name="linear"):
    t, k = x.shape
    n = w.shape[1]
    n_out = n // 2 if glu else n
    return pl.pallas_call(
        functools.partial(_linear_body, nc=nc, glu=glu),
        grid=(t // tm,),
        in_specs=[pl.BlockSpec((tm, k), lambda i: (i, 0)),
                  pl.BlockSpec((k, n), lambda i: (0, 0)),
                  pl.BlockSpec((1, n), lambda i: (0, 0))],
        out_specs=pl.BlockSpec((tm, n_out), lambda i: (i, 0)),
        out_shape=jax.ShapeDtypeStruct((t, n_out), out_dtype),
        compiler_params=_cp(("parallel",), 48),
        name=name,
    )(x, w, b)


def _split3(x):
    hi = x.astype(BF16).astype(F32)
    mid = (x - hi).astype(BF16).astype(F32)
    lo = (x - hi - mid).astype(BF16).astype(F32)
    return [hi, mid, lo]


def _lane_table(lane, values, first):
    out = jnp.zeros(lane.shape, F32)
    for n, val in enumerate(values):
        out = jnp.where(lane == first + n, val, out)
    return out


def _attn_body(lam_ref, g_ref, d0_ref, q_ref, k_ref, v_ref, o_ref, kp_scr, vt_scr, qq_scr, s0_scr, s1_scr,
               m_scr, l_scr, acc_scr, *, tq, tk, lam_init):
    h = pl.program_id(1)
    qi = pl.program_id(2)
    length = k_ref.shape[0]
    nk = length // tk
    n_other = nk - 1
    hd2 = 2 * HEAD_DIM
    slope2 = jnp.exp2(-(jnp.full((1, 1), h, jnp.int32) + 1).astype(F32)) * LOG2E
    c1 = _split3(slope2)
    c128 = [128.0 * c for c in c1]

    @pl.when(qi == 0)
    def _():
        lane = lax.broadcasted_iota(jnp.int32, (ATT_FILL, hd2), 1)
        consts = _lane_table(lane, [-c for c in c128] + [-c for c in c1], 0)
        eye = jnp.where(lax.broadcasted_iota(jnp.int32, (hd2, hd2), 0)
                        == lax.broadcasted_iota(jnp.int32, (hd2, hd2), 1), 1.0, 0.0).astype(BF16)

        def fill(ci, carry):
            r0 = pl.multiple_of(ci * ATT_FILL, ATT_FILL)
            pos = r0 + lax.broadcasted_iota(jnp.int32, (ATT_FILL, hd2), 0)
            hi_digit = jnp.right_shift(pos, 7).astype(F32)
            lo_digit = jnp.bitwise_and(pos, 127).astype(F32)
            aug = jnp.where(lane < 6, consts, jnp.where(lane < 9, hi_digit, jnp.where(lane < 12, lo_digit, 0.0)))
            kp_scr[pl.ds(r0, ATT_FILL), 0:hd2] = k_ref[pl.ds(r0, ATT_FILL), :]
            kp_scr[pl.ds(r0, ATT_FILL), hd2:2 * hd2] = aug.astype(BF16)
            vt_scr[:, pl.ds(r0, ATT_FILL)] = lax.dot_general(
                eye, v_ref[pl.ds(r0, ATT_FILL), :], NT_DIMS, preferred_element_type=F32).astype(BF16)
            return carry

        lax.fori_loop(0, length // ATT_FILL, fill, 0)

    q = q_ref[...]
    lane = lax.broadcasted_iota(jnp.int32, (tq, hd2), 1)
    zero = jnp.zeros_like(q)
    tpos = qi * tq + lax.broadcasted_iota(jnp.int32, (tq, hd2), 0)
    hi_digit = jnp.right_shift(tpos, 7).astype(F32)
    lo_digit = jnp.bitwise_and(tpos, 127).astype(F32)
    consts = _lane_table(lane, c128 + c1, 6)
    augq = jnp.where(lane < 3, hi_digit, jnp.where(lane < 6, lo_digit, consts))
    for var, aug in enumerate((augq.astype(BF16), (-augq).astype(BF16))):
        qq_scr[var, 0:tq, 0:hd2] = jnp.where(lane < HEAD_DIM, q, zero)
        qq_scr[var, tq:2 * tq, 0:hd2] = jnp.where(lane >= HEAD_DIM, q, zero)
        qq_scr[var, 0:tq, hd2:2 * hd2] = aug
        qq_scr[var, tq:2 * tq, hd2:2 * hd2] = aug

    m_scr[...] = jnp.full(m_scr.shape, -jnp.inf, F32)
    l_scr[...] = jnp.zeros(l_scr.shape, F32)
    acc_scr[...] = jnp.zeros(acc_scr.shape, F32)

    kd = (qi * tq) // tk

    def other_block(n):
        return jnp.where(n < n_other, jnp.where(n < kd, n, n + 1), kd)

    def scores(blk, s_scr):
        ks = pl.multiple_of(blk * tk, tk)
        var = (blk > kd).astype(jnp.int32)
        s_scr[...] = lax.dot_general(kp_scr[pl.ds(ks, tk), :], qq_scr[var], NT_DIMS, preferred_element_type=F32)

    def softmax_pv(blk, s_scr, diag):
        ks = pl.multiple_of(blk * tk, tk)
        cdiag = (qi * tq - blk * tk).astype(F32)
        vt = vt_scr[:, pl.ds(ks, tk)]
        m_all = m_scr[...]
        l_all = l_scr[...]
        m_out, l_out = [], []
        for cg in range(2 * tq // ATT_CW):
            cols = slice(cg * ATT_CW, (cg + 1) * ATT_CW)
            s = s_scr[:, cols]
            if diag:
                dc = (cg * ATT_CW) % tq
                s = s + jnp.minimum(d0_ref[:, dc:dc + ATT_CW] + cdiag, 0.0) * (2.0 * slope2)
            m_old = m_all[:, cols]
            m_new = jnp.maximum(m_old, jnp.max(s, axis=0, keepdims=True))
            p = jnp.exp2(s - m_new)
            alpha = jnp.exp2(m_old - m_new)
            l_out.append(alpha * l_all[:, cols] + jnp.sum(p, axis=0, keepdims=True))
            m_out.append(m_new)
            pv = jnp.dot(vt, p.astype(BF16), preferred_element_type=F32)
            acc_scr[:, cols] = alpha * acc_scr[:, cols] + pv
        m_scr[...] = jnp.concatenate(m_out, axis=1)
        l_scr[...] = jnp.concatenate(l_out, axis=1)

    scores(other_block(0), s0_scr)

    def pair(n2, carry):
        n = 2 * n2
        scores(other_block(n + 1), s1_scr)
        softmax_pv(other_block(n), s0_scr, False)
        scores(other_block(n + 2), s0_scr)
        softmax_pv(other_block(n + 1), s1_scr, False)
        return carry

    lax.fori_loop(0, n_other // 2, pair, 0)
    if n_other % 2:
        scores(kd, s1_scr)
        softmax_pv(other_block(n_other - 1), s0_scr, False)
        softmax_pv(kd, s1_scr, True)
    else:
        softmax_pv(kd, s0_scr, True)

    lam = lam_ref[...]
    lam_full = (jnp.exp(jnp.sum(lam[0:1] * lam[1:2], axis=-1, keepdims=True))
                - jnp.exp(jnp.sum(lam[2:3] * lam[3:4], axis=-1, keepdims=True)) + lam_init)
    ot = acc_scr[...] / l_scr[...]
    ot = ot[:, :tq] - lam_full * ot[:, tq:]
    ms = jnp.mean(ot * ot, axis=0, keepdims=True)
    ot = ot * lax.rsqrt(ms + LN_EPS) * g_ref[...] * (1.0 - lam_init)
    o_ref[...] = jnp.transpose(ot).astype(o_ref.dtype)


def diff_attention(qkv, lam, subln_g, *, lam_init, tk=ATT_TK):
    b, l, _ = qkv.shape
    tq = ATT_TQ_SHORT if l == 2 * tk else ATT_TQ
    assert l % tk == 0 and l // tk >= 2 and tk % tq == 0 and l <= 128 * 128
    hd2 = 2 * HEAD_DIM
    d0 = (jnp.arange(tq, dtype=jnp.int32)[None, :] - jnp.arange(tk, dtype=jnp.int32)[:, None]).astype(F32)
    return pl.pallas_call(
        functools.partial(_attn_body, tq=tq, tk=tk, lam_init=lam_init),
        grid=(b, N_HEADS, l // tq),
        in_specs=[pl.BlockSpec((4, HEAD_DIM), lambda bi, h, qi: (0, 0)),
                  pl.BlockSpec((hd2, 1), lambda bi, h, qi: (0, 0)),
                  pl.BlockSpec((tk, tq), lambda bi, h, qi: (0, 0), pipeline_mode=pl.Buffered(1)),
                  pl.BlockSpec((None, tq, hd2), lambda bi, h, qi: (bi, qi, h)),
                  pl.BlockSpec((None, l, hd2), lambda bi, h, qi: (bi, 0, N_HEADS + h)),
                  pl.BlockSpec((None, l, hd2), lambda bi, h, qi: (bi, 0, 2 * N_HEADS + h))],
        out_specs=pl.BlockSpec((None, tq, hd2), lambda bi, h, qi: (bi, qi, h)),
        out_shape=jax.ShapeDtypeStruct((b, l, D_MODEL), BF16),
        scratch_shapes=[pltpu.VMEM((l, 2 * hd2), BF16), pltpu.VMEM((hd2, l), BF16),
                        pltpu.VMEM((2, 2 * tq, 2 * hd2), BF16),
                        pltpu.VMEM((tk, 2 * tq), F32), pltpu.VMEM((tk, 2 * tq), F32),
                        pltpu.VMEM((1, 2 * tq), F32), pltpu.VMEM((1, 2 * tq), F32),
                        pltpu.VMEM((hd2, 2 * tq), F32)],
        compiler_params=_cp(("parallel", "parallel", "arbitrary"), 56),
        name="diff_attention",
    )(lam, subln_g, d0, qkv, qkv, qkv)


def _route(x1, wr, br, comb_ref):
    lt = lax.dot_general(wr, x1, NT_DIMS, precision=HIGHEST, preferred_element_type=F32) + br
    gl = [lt[g:g + 1] for g in range(N_GROUPS)]
    gmax = jnp.maximum(jnp.maximum(gl[0], gl[1]), jnp.maximum(gl[2], gl[3]))
    gidx = jnp.where(gl[0] == gmax, 0, jnp.where(gl[1] == gmax, 1, jnp.where(gl[2] == gmax, 2, 3)))
    gw = 1.0 / (jnp.exp(gl[0] - gmax) + jnp.exp(gl[1] - gmax) + jnp.exp(gl[2] - gmax) + jnp.exp(gl[3] - gmax))
    el = []
    for j in range(EXPERTS_PER_GROUP):
        acc = jnp.zeros_like(gmax)
        for g in range(N_GROUPS):
            r = N_GROUPS + g * EXPERTS_PER_GROUP + j
            acc = jnp.where(gidx == g, lt[r:r + 1], acc)
        el.append(acc)
    v1 = jnp.maximum(jnp.maximum(el[0], el[1]), jnp.maximum(el[2], el[3]))
    i1 = jnp.where(el[0] == v1, 0, jnp.where(el[1] == v1, 1, jnp.where(el[2] == v1, 2, 3)))
    neg = jnp.full_like(v1, -jnp.inf)
    el2 = [jnp.where(i1 == j, neg, el[j]) for j in range(EXPERTS_PER_GROUP)]
    v2 = jnp.maximum(jnp.maximum(el2[0], el2[1]), jnp.maximum(el2[2], el2[3]))
    i2 = jnp.where(el2[0] == v2, 0, jnp.where(el2[1] == v2, 1, jnp.where(el2[2] == v2, 2, 3)))
    e21 = jnp.exp(v2 - v1)
    w1 = gw / (1.0 + e21)
    w2 = gw * e21 / (1.0 + e21)
    zero = jnp.zeros_like(v1)
    for g in range(N_GROUPS):
        for j in range(EXPERTS_PER_GROUP):
            wj = jnp.where(i1 == j, w1, jnp.where(i2 == j, w2, zero))
            r = g * EXPERTS_PER_GROUP + j
            comb_ref[r:r + 1, :] = jnp.where(gidx == g, wj, zero)


def _proj_ln_route_body(a_ref, w_ref, b_ref, x_ref, g_ref, beta_ref, wr_ref, br_ref, x1_ref, comb_ref):
    h = jnp.dot(a_ref[...].astype(BF16), w_ref[...], preferred_element_type=F32) + b_ref[...]
    x1 = _layer_norm(ALPHA * x_ref[...] + h, g_ref[...], beta_ref[...])
    x1_ref[...] = x1
    _route(x1, wr_ref[...], br_ref[...], comb_ref)


def proj_ln_route(a, w, b, x, ln_g, ln_b, wr, br, *, tm=512):
    t, k = a.shape
    d = D_MODEL
    const = lambda i: (0, 0)
    return pl.pallas_call(
        _proj_ln_route_body,
        grid=(t // tm,),
        in_specs=[pl.BlockSpec((tm, k), lambda i: (i, 0)),
                  pl.BlockSpec((k, d), const), pl.BlockSpec((1, d), const),
                  pl.BlockSpec((tm, d), lambda i: (i, 0)),
                  pl.BlockSpec((1, d), const), pl.BlockSpec((1, d), const),
                  pl.BlockSpec((ROUTE_ROWS, d), const), pl.BlockSpec((ROUTE_ROWS, 1), const)],
        out_specs=[pl.BlockSpec((tm, d), lambda i: (i, 0)),
                   pl.BlockSpec((N_EXPERTS, tm), lambda i: (0, i))],
        out_shape=[jax.ShapeDtypeStruct((t, d), F32), jax.ShapeDtypeStruct((N_EXPERTS, t), F32)],
        compiler_params=_cp(("parallel",), 48),
        name="proj_ln_route",
    )(a, w, b, x, ln_g, ln_b, wr, br)


def _moe_body(x1_ref, comb_ref, uw_ref, lw_ref, wg_ref, wu_ref, wd_ref, g2_ref, b2_ref, p_ref, up_ref, gate_ref,
              o_ref, vrow_scr, vcol_scr, xg_scr, og_scr, wsl_scr, fast_scr, *, tm, st, caps):
    e = pl.program_id(1)
    n_sub = tm // st

    def sub_rows(sub):
        return slice(sub * st, (sub + 1) * st)

    @pl.when(e == 0)
    def _():
        member = jnp.where(comb_ref[...] > 0.0, 1.0, 0.0).astype(BF16)
        for sub in range(n_sub):
            ms = member[:, sub_rows(sub)]
            vrow_scr[:, sub_rows(sub)] = jnp.dot(ms, uw_ref[...], preferred_element_type=F32)
            vcol_scr[sub_rows(sub), :] = lax.dot_general(lw_ref[...], ms, NT_DIMS, preferred_element_type=F32)
        most = ((jnp.max(vrow_scr[...]) + 1.0) * 0.5).astype(jnp.int32)
        fast_scr[0] = (most <= caps).astype(jnp.int32)

        @pl.when(most <= caps)
        def _():
            tgt = 2.0 * lax.broadcasted_iota(jnp.int32, (caps, 1), 0).astype(F32) + 1.0
            for sub in range(n_sub):
                vr = vrow_scr[:, sub_rows(sub)]
                cw = comb_ref[:, sub_rows(sub)]
                blocks = []
                for ee in range(N_EXPERTS):
                    hit = vr[ee:ee + 1, :] == tgt
                    blocks.append(jnp.where(hit, 1.0, 0.0).astype(BF16))
                    wsl_scr[ee, sub * caps:(sub + 1) * caps, :] = jnp.sum(
                        jnp.where(hit, cw[ee:ee + 1, :], 0.0), axis=1, keepdims=True)
                gather = jnp.concatenate(blocks, axis=0)
                xg = jnp.dot(gather, x1_ref[sub_rows(sub), :].astype(BF16),
                             preferred_element_type=F32).astype(BF16)
                for ee in range(N_EXPERTS):
                    xg_scr[ee, sub * caps:(sub + 1) * caps, :] = xg[ee * caps:(ee + 1) * caps]

        @pl.when(most > caps)
        def _():
            o_ref[...] = jnp.zeros(o_ref.shape, F32)

    fast = fast_scr[0] == 1

    def expert_ffn(xg, wslot):
        hg = jnp.dot(xg, wg_ref[...], preferred_element_type=F32)
        hu = jnp.dot(xg, wu_ref[...], preferred_element_type=F32)
        hid = hg * _sigmoid(hg) * hu * wslot
        return jnp.dot(hid.astype(BF16), wd_ref[...], preferred_element_type=F32).astype(BF16)

    @pl.when(fast)
    def _():
        ob = expert_ffn(xg_scr[e], wsl_scr[e])
        r0 = pl.multiple_of(e * caps, caps)
        for sub in range(n_sub):
            og_scr[sub, pl.ds(r0, caps), :] = ob[sub * caps:(sub + 1) * caps]

    @pl.when(jnp.logical_not(fast))
    def _():
        sel = lax.broadcasted_iota(jnp.int32, (st, N_EXPERTS), 1) == e
        prow = lax.broadcasted_iota(jnp.int32, (8, st), 0)
        for sub in range(n_sub):
            vrow = vrow_scr[pl.ds(e, 1), sub_rows(sub)]
            vcol = jnp.sum(jnp.where(sel, vcol_scr[sub_rows(sub), :], 0.0), axis=1, keepdims=True)
            w_hi, w_mid, w_lo = _split3(comb_ref[pl.ds(e, 1), sub_rows(sub)])
            w3 = jnp.where(prow == 0, w_hi, jnp.where(prow == 1, w_mid, jnp.where(prow == 2, w_lo, 0.0))
                           ).astype(BF16)
            count = ((jnp.max(vrow) + 1.0) * 0.5).astype(jnp.int32)
            xsub = x1_ref[sub_rows(sub), :].astype(BF16)

            def chunk(c, carry):
                base = (c * caps).astype(F32)
                tgt_r = 2.0 * (base + lax.broadcasted_iota(jnp.int32, (caps, 1), 0).astype(F32)) + 1.0
                tgt_c = 2.0 * (base + lax.broadcasted_iota(jnp.int32, (1, caps), 1).astype(F32)) + 1.0
                gather = jnp.where(vrow == tgt_r, 1.0, 0.0).astype(BF16)
                scatter = jnp.where(vcol == tgt_c, 1.0, 0.0).astype(BF16)
                xg = jnp.dot(gather, xsub, preferred_element_type=F32).astype(BF16)
                ws = lax.dot_general(gather, w3, NT_DIMS, preferred_element_type=F32)
                ob = expert_ffn(xg, ws[:, 0:1] + ws[:, 1:2] + ws[:, 2:3])
                o_ref[sub_rows(sub), :] += jnp.dot(scatter, ob, preferred_element_type=F32)
                return carry

            lax.fori_loop(0, (count + (caps - 1)) // caps, chunk, 0)

    @pl.when(e == N_EXPERTS - 1)
    def _():
        @pl.when(fast)
        def _():
            lane = lax.broadcasted_iota(jnp.int32, (st, 2 * caps), 1)
            tgt = 2.0 * jnp.where(lane < caps, lane, lane - caps).astype(F32) + 1.0
            for sub in range(n_sub):
                vc = vcol_scr[sub_rows(sub), :]
                pieces = []
                for pr in range(N_EXPERTS // 2):
                    val = jnp.where(lane < caps, vc[:, 2 * pr:2 * pr + 1], vc[:, 2 * pr + 1:2 * pr + 2])
                    pieces.append(jnp.where(val == tgt, 1.0, 0.0).astype(BF16))
                scatter = jnp.concatenate(pieces, axis=1)
                o_ref[sub_rows(sub), :] = jnp.dot(scatter, og_scr[sub], preferred_element_type=F32)

        x2 = _layer_norm(ALPHA * x1_ref[...] + o_ref[...], g2_ref[...], b2_ref[...])
        up = jnp.dot(p_ref[...].astype(BF16), up_ref[...], preferred_element_type=F32)
        gt = jnp.dot(x2.astype(BF16), gate_ref[...], preferred_element_type=F32)
        o_ref[...] = x2 + up * _sigmoid(gt)


def moe_ple(x1, comb, uw, lw, wg, wu, wd, ln_g, ln_b, p, up, gate, *, tm=MOE_TM, st=MOE_ST, caps=MOE_CAPS):
    t, d = x1.shape
    ff = EXPERT_FF
    n_sub = tm // st
    assert 2 * caps == 128 and st % 128 == 0
    const = lambda i, e: (0, 0)
    once = pl.Buffered(1)
    return pl.pallas_call(
        functools.partial(_moe_body, tm=tm, st=st, caps=caps),
        grid=(t // tm, N_EXPERTS),
        in_specs=[pl.BlockSpec((tm, d), lambda i, e: (i, 0)),
                  pl.BlockSpec((N_EXPERTS, tm), lambda i, e: (0, i)),
                  pl.BlockSpec((st, st), const), pl.BlockSpec((st, st), const),
                  pl.BlockSpec((None, d, ff), lambda i, e: (e, 0, 0)),
                  pl.BlockSpec((None, d, ff), lambda i, e: (e, 0, 0)),
                  pl.BlockSpec((None, ff, d), lambda i, e: (e, 0, 0)),
                  pl.BlockSpec((1, d), const), pl.BlockSpec((1, d), const),
                  pl.BlockSpec((tm, PLE_DIM), lambda i, e: (i, 0), pipeline_mode=once),
                  pl.BlockSpec((PLE_DIM, d), const, pipeline_mode=once),
                  pl.BlockSpec((d, d), const, pipeline_mode=once)],
        out_specs=pl.BlockSpec((tm, d), lambda i, e: (i, 0)),
        out_shape=jax.ShapeDtypeStruct((t, d), F32),
        scratch_shapes=[pltpu.VMEM((N_EXPERTS, tm), F32), pltpu.VMEM((tm, N_EXPERTS), F32),
                        pltpu.VMEM((N_EXPERTS, n_sub * caps, d), BF16),
                        pltpu.VMEM((n_sub, N_EXPERTS * caps, d), BF16),
                        pltpu.VMEM((N_EXPERTS, n_sub * caps, 1), F32),
                        pltpu.SMEM((1,), jnp.int32)],
        compiler_params=_cp(("parallel", "arbitrary"), 58),
        name="moe_ple",
    )(x1, comb, uw, lw, wg, wu, wd, ln_g, ln_b, p, up, gate)


def _dwconv_body(prev_ref, cur_ref, next_ref, w_ref, b_ref, g_ref, beta_ref, o_ref, buf_scr, sh_scr,
                 *, width, tm, norm_act):
    i = pl.program_id(1)
    n = pl.num_programs(1)
    pad = width // 2
    halo = jnp.zeros(prev_ref.shape, F32)
    buf_scr[0:HALO, :] = jnp.where(i > 0, prev_ref[...], halo)
    buf_scr[HALO:HALO + tm, :] = cur_ref[...]
    buf_scr[HALO + tm:2 * HALO + tm, :] = jnp.where(i < n - 1, next_ref[...], halo)
    offs = [HALO - pad + k for k in range(width)]
    n_sh = tm + 2 * HALO - 8
    for sft in sorted({off % 8 for off in offs}):
        sh_scr[sft, 0:n_sh, :] = buf_scr[sft:sft + n_sh, :]

    def rows(r, carry):
        r0 = pl.multiple_of(r * CONV_ROWS, CONV_ROWS)
        acc = jnp.zeros((CONV_ROWS, o_ref.shape[1]), F32) + b_ref[...]
        for k, off in enumerate(offs):
            start = pl.multiple_of(r0 + (off // 8) * 8, 8)
            acc = acc + sh_scr[off % 8, pl.ds(start, CONV_ROWS), :] * w_ref[k:k + 1, :]
        if norm_act:
            acc = _layer_norm(acc, g_ref[...], beta_ref[...])
            acc = acc * _sigmoid(acc)
        o_ref[pl.ds(r0, CONV_ROWS), :] = acc.astype(o_ref.dtype)
        return carry

    lax.fori_loop(0, tm // CONV_ROWS, rows, 0)


def dwconv(x, w, b, ln_g, ln_b, *, norm_act, out_dtype, tm=512):
    bsz, l, c = x.shape
    width = w.shape[0]
    tc = D_MODEL
    nb = tm // HALO
    last = l // HALO - 1
    return pl.pallas_call(
        functools.partial(_dwconv_body, width=width, tm=tm, norm_act=norm_act),
        grid=(bsz, l // tm, c // tc),
        in_specs=[pl.BlockSpec((None, HALO, tc), lambda bi, i, j: (bi, jnp.maximum(i * nb - 1, 0), j)),
                  pl.BlockSpec((None, tm, tc), lambda bi, i, j: (bi, i, j)),
                  pl.BlockSpec((None, HALO, tc), lambda bi, i, j: (bi, jnp.minimum((i + 1) * nb, last), j)),
                  pl.BlockSpec((width, tc), lambda bi, i, j: (0, j)),
                  pl.BlockSpec((1, tc), lambda bi, i, j: (0, j)),
                  pl.BlockSpec((1, tc), lambda bi, i, j: (0, 0)),
                  pl.BlockSpec((1, tc), lambda bi, i, j: (0, 0))],
        out_specs=pl.BlockSpec((None, None, tm, tc), lambda bi, i, j: (j, bi, i, 0)),
        out_shape=jax.ShapeDtypeStruct((c // tc, bsz, l, tc), out_dtype),
        scratch_shapes=[pltpu.VMEM((tm + 2 * HALO, tc), F32), pltpu.VMEM((8, tm + 2 * HALO, tc), F32)],
        compiler_params=_cp(("parallel", "parallel", "parallel"), 40),
        name="dwconv%d" % width,
    )(x, x, x, w, b, ln_g, ln_b)


def _filter_body(w1t_ref, w1c_ref, w1s_ref, b1_ref, w2_ref, b2_ref, w3_ref, b3_ref, freq_ref, wout_ref,
                 o_ref, *, tl, length):
    i = pl.program_id(0)
    pos = (i * tl + lax.broadcasted_iota(jnp.int32, (tl, 1), 0)).astype(F32)
    t = pos / float(length - 1)
    ang = (2.0 * math.pi / length) * pos
    band = lax.broadcasted_iota(jnp.int32, (1, POS_BANDS), 1).astype(F32)
    bands = 1e-4 + band * ((POS_BANDS - 1 - 1e-4) / (POS_BANDS - 1))
    fw = bands * ang
    freq = freq_ref[...]
    dot = functools.partial(jnp.dot, precision=HIGHEST, preferred_element_type=F32)
    pre = t * w1t_ref[...] + dot(jnp.cos(fw), w1c_ref[...]) - dot(jnp.sin(fw), w1s_ref[...]) + b1_ref[...]
    hcur = jnp.sin(freq * pre)
    hcur = jnp.sin(freq * (dot(hcur, w2_ref[...]) + b2_ref[...]))
    hcur = jnp.sin(freq * (dot(hcur, w3_ref[...]) + b3_ref[...]))
    ch = lax.broadcasted_iota(jnp.int32, (1, D_MODEL), 1).astype(F32)
    deltas = jnp.abs(MIN_DECAY + ch * ((MAX_DECAY - MIN_DECAY) / (D_MODEL - 1)))
    decay = jnp.exp(-t * deltas)
    for j in range(2 * HYENA_ORDER):
        o_ref[j] = dot(hcur, wout_ref[:, j * D_MODEL:(j + 1) * D_MODEL]) * decay


def hyena_filters(length, w1, b1, w2, b2, w3, b3, freq, wout, *, tl=256):
    fwid = FILTER_WIDTH
    const = lambda i: (0, 0)
    nf = 2 * HYENA_ORDER
    return pl.pallas_call(
        functools.partial(_filter_body, tl=tl, length=length),
        grid=(length // tl,),
        in_specs=[pl.BlockSpec((1, fwid), const), pl.BlockSpec((POS_BANDS, fwid), const),
                  pl.BlockSpec((POS_BANDS, fwid), const), pl.BlockSpec((1, fwid), const),
                  pl.BlockSpec((fwid, fwid), const), pl.BlockSpec((1, fwid), const),
                  pl.BlockSpec((fwid, fwid), const), pl.BlockSpec((1, fwid), const),
                  pl.BlockSpec((1, fwid), const), pl.BlockSpec((fwid, nf * D_MODEL), const)],
        out_specs=pl.BlockSpec((nf, tl, D_MODEL), lambda i: (0, i, 0)),
        out_shape=jax.ShapeDtypeStruct((nf, length, D_MODEL), F32),
        compiler_params=_cp(("parallel",), 32),
        name="hyena_filters",
    )(w1[0:1], w1[1:1 + POS_BANDS], w1[1 + POS_BANDS:], b1, w2, b2, w3, b3, freq, wout)


def _dft_tables(n1, n2s):
    n = n1 * n2s
    m1 = (n1 // 2 + 1 + 7) // 8 * 8
    k1 = jnp.arange(m1, dtype=jnp.int32)
    valid = (k1 <= n1 // 2)
    a = jnp.arange(n1 // 2, dtype=jnp.int32)
    ang_a = (2.0 * math.pi / n1) * ((k1[:, None] * a[None, :]) % n1).astype(F32)
    ca = jnp.where(valid[:, None], jnp.cos(ang_a), 0.0)
    sa = jnp.where(valid[:, None], jnp.sin(ang_a), 0.0)
    g_a = jnp.concatenate([ca, -sa], axis=0)
    g_ai = jnp.concatenate([ca.T, -sa.T], axis=1)
    weight = jnp.where(valid, jnp.where((k1 == 0) | (k1 == n1 // 2), 1.0, 2.0), 0.0)
    b = jnp.arange(n2s, dtype=jnp.int32)
    k2 = jnp.arange(n2s, dtype=jnp.int32)
    idx = (b[None, None, :] * (k2[None, :, None] * n1 + k1[:, None, None])) % n
    ang_c = (2.0 * math.pi / n) * idx.astype(F32)
    cc, sc = jnp.cos(ang_c), jnp.sin(ang_c)
    g_c = jnp.concatenate([jnp.concatenate([cc, sc], axis=2),
                           jnp.concatenate([-sc, cc], axis=2)], axis=1)
    cct = jnp.swapaxes(cc, 1, 2) * (weight / n)[:, None, None]
    sct = jnp.swapaxes(sc, 1, 2) * (weight / n)[:, None, None]
    g_ci = jnp.concatenate([jnp.concatenate([cct, -sct], axis=2),
                            jnp.concatenate([sct, cct], axis=2)], axis=1)
    return m1, g_a.astype(BF16), g_c.astype(BF16), g_ci.astype(BF16), g_ai.astype(BF16)


def _dft_dot(g_ref, x):
    return jnp.dot(g_ref[...], x.astype(BF16), preferred_element_type=F32)


def _dft_a_body(g_ref, x_ref, o_ref, *, nb, m1):
    for bb in range(nb):
        res = _dft_dot(g_ref, x_ref[:, bb, :])
        o_ref[:, 0, bb, :] = res[:m1]
        o_ref[:, 1, bb, :] = res[m1:]


def _dft_slab_body(g1_ref, g2_ref, x_ref, ff_ref, fb_ref, o_ref, *, bt):
    half = g1_ref.shape[0] // 2
    hf = _dft_dot(g1_ref, ff_ref[...])
    hb = _dft_dot(g1_ref, fb_ref[...])
    hr = hf[:half] + hb[:half]
    hi = hf[half:] - hb[half:]
    for bi in range(bt):
        xs = _dft_dot(g1_ref, x_ref[bi])
        xr, xi = xs[:half], xs[half:]
        ys = jnp.concatenate([xr * hr - xi * hi, xr * hi + xi * hr], axis=0)
        o_ref[bi] = _dft_dot(g2_ref, ys)


def _idft_gate_body(g_ref, x_ref, v_ref, gate_ref, bias_ref, o_ref, *, nb):
    for bb in range(nb):
        spec = jnp.concatenate([x_ref[:, 0, bb, :], x_ref[:, 1, bb, :]], axis=0)
        y = _dft_dot(g_ref, spec)
        o_ref[:, bb, :] = gate_ref[:, bb, :] * (y + v_ref[:, bb, :] * bias_ref[...])


def _dft_a(g, x, xsel, *, m1, nb, dt):
    _, bsz, half, n2s, d = x.shape
    return pl.pallas_call(
        functools.partial(_dft_a_body, nb=nb, m1=m1),
        grid=(bsz, n2s // nb, d // dt),
        in_specs=[pl.BlockSpec((2 * m1, half), lambda bi, bj, di: (0, 0)),
                  pl.BlockSpec((None, None, half, nb, dt), lambda bi, bj, di: (xsel, bi, 0, bj, di))],
        out_specs=pl.BlockSpec((None, m1, 2, nb, dt), lambda bi, bj, di: (bi, 0, 0, bj, di)),
        out_shape=jax.ShapeDtypeStruct((bsz, m1, 2, n2s, d), F32),
        compiler_params=_cp(("parallel", "parallel", "parallel"), 48),
        name="dft_a",
    )(g, x)


def _dft_slab(g1, g2, x, fa, order, *, bt):
    bsz, s, r, d = x.shape
    gspec = pl.BlockSpec((None, r, r), lambda si, bi: (si, 0, 0))
    xspec = pl.BlockSpec((bt, None, r, d), lambda si, bi: (bi, si, 0, 0))
    return pl.pallas_call(
        functools.partial(_dft_slab_body, bt=bt),
        grid=(s, bsz // bt),
        in_specs=[gspec, gspec, xspec,
                  pl.BlockSpec((None, None, r, d), lambda si, bi: (2 * order, si, 0, 0)),
                  pl.BlockSpec((None, None, r, d), lambda si, bi: (2 * order + 1, si, 0, 0))],
        out_specs=xspec,
        out_shape=jax.ShapeDtypeStruct((bsz, s, r, d), F32),
        compiler_params=_cp(("parallel", "parallel"), 48),
        name="dft_spec",
    )(g1, g2, x, fa, fa)


def _idft_gate(g, x, v, vsel, gates, gsel, bias, *, nb, dt):
    bsz, m1, _, n2s, d = x.shape
    half = g.shape[0]
    tspec = lambda sel: pl.BlockSpec((None, None, half, nb, dt), lambda bi, bj, di: (sel, bi, 0, bj, di))
    return pl.pallas_call(
        functools.partial(_idft_gate_body, nb=nb),
        grid=(bsz, n2s // nb, d // dt),
        in_specs=[pl.BlockSpec((half, 2 * m1), lambda bi, bj, di: (0, 0)),
                  pl.BlockSpec((None, m1, 2, nb, dt), lambda bi, bj, di: (bi, 0, 0, bj, di)),
                  tspec(vsel), tspec(gsel),
                  pl.BlockSpec((1, dt), lambda bi, bj, di: (0, di))],
        out_specs=pl.BlockSpec((None, half, nb, dt), lambda bi, bj, di: (bi, 0, bj, di)),
        out_shape=jax.ShapeDtypeStruct((bsz, half, n2s, d), F32),
        compiler_params=_cp(("parallel", "parallel", "parallel"), 48),
        name="idft_gate",
    )(g, x, v, gates, bias)


def _fft_plan(length):
    n = 2 * length
    n1 = 256 if n >= 32768 else 64
    return n1, n // n1


def hyena_long_convs(xs, filt, f_bias):
    _, bsz, length, d = xs.shape
    n1, n2s = _fft_plan(length)
    m1, g_a, g_c, g_ci, g_ai = _dft_tables(n1, n2s)
    half = n1 // 2
    big = length >= 8192
    bt = 1 if big else min(8, bsz)
    nb = 8 if big else 16
    dt = d // 2 if big else d
    fa = _dft_a(g_a, filt.reshape(1, 4, half, n2s, d), 0, m1=m1, nb=nb, dt=dt).reshape(4, m1, 2 * n2s, d)
    xs6 = xs.reshape(3, bsz, half, n2s, d)
    z, zsel = xs6, 2
    out = None
    for order in range(HYENA_ORDER):
        sa = _dft_a(g_a, z, zsel, m1=m1, nb=nb, dt=dt).reshape(bsz, m1, 2 * n2s, d)
        sb = _dft_slab(g_c, g_ci, sa, fa, order, bt=bt).reshape(bsz, m1, 2, n2s, d)
        out = _idft_gate(g_ai, sb, z, zsel, xs6, order, f_bias[order:order + 1], nb=nb, dt=dt)
        z, zsel = out[None], 0
    return out.reshape(bsz, length, d)


def _prefix_tables(tm):
    r = jnp.arange(tm, dtype=jnp.int32)
    upper = jnp.where(r[:, None] < r[None, :], 2.0, jnp.where(r[:, None] == r[None, :], 1.0, 0.0))
    return upper.astype(BF16), upper.T.astype(BF16)


def _trunk(x, p, wts):
    bsz, length, d = x.shape
    t = bsz * length
    xf = x.reshape(t, d)
    zeros_d = jnp.zeros((1, d), F32)
    ones_d = jnp.ones((1, d), F32)
    uw, lw = _prefix_tables(MOE_ST)
    for i in range(DEPTH):
        j, kind = i // N_MIXERS, i % N_MIXERS
        if kind == 0:
            lam_init = 0.8 - 0.6 * math.exp(-0.3 * i)
            qkv = linear(xf, wts["attn_w_qkv"][j], jnp.zeros((1, 3 * d), F32), out_dtype=BF16, name="attn_qkv")
            a = diff_attention(qkv.reshape(bsz, length, 3 * d), wts["attn_lam"][j], wts["attn_subln_g"][j],
                               lam_init=lam_init).reshape(t, d)
            w_o, b_o = wts["attn_w_o"][j], zeros_d
        elif kind == 1:
            hglu = linear(xf, wts["conv_w_pw1"][j], wts["conv_b_pw1"][j], out_dtype=F32, glu=True, name="conv_pw1")
            a = dwconv(hglu.reshape(bsz, length, d), wts["conv_w_dw"][j], wts["conv_b_dw"][j],
                       wts["conv_ln_g"][j], wts["conv_ln_b"][j], norm_act=True, out_dtype=BF16).reshape(t, d)
            w_o, b_o = wts["conv_w_pw2"][j], wts["conv_b_pw2"][j]
        else:
            u = linear(xf, wts["hy_w_in"][j], wts["hy_b_in"][j], out_dtype=F32, name="hyena_in")
            xs = dwconv(u.reshape(bsz, length, 3 * d), wts["hy_w_short"][j], wts["hy_b_short"][j],
                        ones_d, zeros_d, norm_act=False, out_dtype=F32)
            filt = hyena_filters(length, wts["hy_f_w1"][j], wts["hy_f_b1"][j], wts["hy_f_w2"][j], wts["hy_f_b2"][j],
                                 wts["hy_f_w3"][j], wts["hy_f_b3"][j], wts["hy_f_freq"][j], wts["hy_f_wout"][j])
            a = hyena_long_convs(xs, filt, wts["hy_f_bias"][j]).reshape(t, d)
            w_o, b_o = wts["hy_w_out"][j], wts["hy_b_out"][j]
        x1, comb = proj_ln_route(a, w_o, b_o, xf, wts["ln1_g"][i], wts["ln1_b"][i],
                                 wts["route_w"][i], wts["route_b"][i])
        xf = moe_ple(x1, comb, uw, lw, wts["moe_w_gate"][i], wts["moe_w_up"][i], wts["moe_w_down"][i],
                     wts["ln2_g"][i], wts["ln2_b"][i], p[i].reshape(t, PLE_DIM),
                     wts["ple_w_up"][i], wts["ple_w_gate"][i])
    return xf.reshape(bsz, length, d)


def kernel(x_prompt, x_sample, p_prompt, p_sample, attn_w_qkv, attn_w_o, attn_lam_q1, attn_lam_k1, attn_lam_q2, attn_lam_k2, attn_subln_g, conv_w_pw1, conv_b_pw1, conv_w_dw, conv_b_dw, conv_ln_g, conv_ln_b, conv_w_pw2, conv_b_pw2, hy_w_in, hy_b_in, hy_w_short, hy_b_short, hy_f_w1, hy_f_b1, hy_f_w2, hy_f_b2, hy_f_w3, hy_f_b3, hy_f_freq, hy_f_wout, hy_f_bias, hy_w_out, hy_b_out, ln1_g, ln1_b, ln2_g, ln2_b, moe_w_group, moe_b_group, moe_w_expert, moe_b_expert, moe_w_gate, moe_w_up, moe_w_down, ple_w_up, ple_w_gate):
    d = D_MODEL
    row = lambda a: a[:, None, :]
    q_scale = jnp.concatenate([jnp.full((d,), HEAD_DIM ** -0.5 * LOG2E, F32), jnp.ones((2 * d,), F32)])
    route_w = jnp.concatenate([jnp.swapaxes(moe_w_group, 1, 2), jnp.swapaxes(moe_w_expert, 1, 2),
                               jnp.zeros((DEPTH, ROUTE_ROWS - N_GROUPS - N_EXPERTS, d), F32)], axis=1)
    route_b = jnp.concatenate([moe_b_group, moe_b_expert,
                               jnp.zeros((DEPTH, ROUTE_ROWS - N_GROUPS - N_EXPERTS), F32)], axis=1)[:, :, None]
    wts = {
        "attn_w_qkv": (attn_w_qkv * q_scale).astype(BF16),
        "attn_w_o": attn_w_o.astype(BF16),
        "attn_lam": jnp.stack([attn_lam_q1, attn_lam_k1, attn_lam_q2, attn_lam_k2], axis=1),
        "attn_subln_g": attn_subln_g[:, :, None],
        "conv_w_pw1": conv_w_pw1.astype(BF16), "conv_b_pw1": row(conv_b_pw1),
        "conv_w_dw": conv_w_dw, "conv_b_dw": row(conv_b_dw),
        "conv_ln_g": row(conv_ln_g), "conv_ln_b": row(conv_ln_b),
        "conv_w_pw2": conv_w_pw2.astype(BF16), "conv_b_pw2": row(conv_b_pw2),
        "hy_w_in": hy_w_in.astype(BF16), "hy_b_in": row(hy_b_in),
        "hy_w_short": hy_w_short, "hy_b_short": row(hy_b_short),
        "hy_f_w1": hy_f_w1, "hy_f_b1": row(hy_f_b1), "hy_f_w2": hy_f_w2, "hy_f_b2": row(hy_f_b2),
        "hy_f_w3": hy_f_w3, "hy_f_b3": row(hy_f_b3), "hy_f_freq": row(hy_f_freq), "hy_f_wout": hy_f_wout,
        "hy_f_bias": hy_f_bias,
        "hy_w_out": hy_w_out.astype(BF16), "hy_b_out": row(hy_b_out),
        "ln1_g": row(ln1_g), "ln1_b": row(ln1_b), "ln2_g": row(ln2_g), "ln2_b": row(ln2_b),
        "route_w": route_w, "route_b": route_b,
        "moe_w_gate": moe_w_gate.astype(BF16), "moe_w_up": moe_w_up.astype(BF16),
        "moe_w_down": moe_w_down.astype(BF16),
        "ple_w_up": ple_w_up.astype(BF16), "ple_w_gate": ple_w_gate.astype(BF16),
    }
    y_prompt = _trunk(x_prompt, p_prompt, wts)
    y_sample = _trunk(x_sample, p_sample, wts)
    return (y_prompt, y_sample)
```

```python
import functools
import math

import jax
import jax.numpy as jnp
from jax import lax
from jax.experimental import pallas as pl
from jax.experimental.pallas import tpu as pltpu

F32 = jnp.float32
BF16 = jnp.bfloat16
HIGHEST = lax.Precision.HIGHEST

D_MODEL = 1024
DEPTH = 4
N_MIXERS = 3
N_HEADS = 8
HEAD_DIM = 64
CONV_WIDTH = 31
HYENA_ORDER = 2
SHORT_WIDTH = 3
POS_BANDS = 16
FILTER_WIDTH = 64
MAX_DECAY = math.log(1e-2) / 0.3
MIN_DECAY = math.log(1e-2) / 1.5
N_GROUPS = 4
EXPERTS_PER_GROUP = 4
N_EXPERTS = N_GROUPS * EXPERTS_PER_GROUP
EXPERT_FF = 512
PLE_DIM = 256
ALPHA = (2 * DEPTH) ** 0.25
LN_EPS = 1e-5
LOG2E = 1.4426950408889634

HALO = 16
CONV_ROWS = 32
ROUTE_ROWS = 24
MOE_TM = 1024
MOE_ST = 256
MOE_CAPS = 64
ATT_TQ = 512
ATT_TQ_SHORT = 1024
ATT_TK = 1024
ATT_CW = 256
ATT_FILL = 512
NT_DIMS = (((1,), (1,)), ((), ()))


def _cp(sem, vmem_mb):
    return pltpu.CompilerParams(dimension_semantics=sem, vmem_limit_bytes=vmem_mb << 20)


def _layer_norm(z, g, b):
    mu = jnp.mean(z, axis=-1, keepdims=True)
    zc = z - mu
    var = jnp.mean(zc * zc, axis=-1, keepdims=True)
    return zc * lax.rsqrt(var + LN_EPS) * g + b


def _sigmoid(x):
    return 1.0 / (1.0 + jnp.exp(-x))


def _linear_body(x_ref, w_ref, b_ref, o_ref, *, nc, glu):
    x = x_ref[...].astype(BF16)
    n_out = o_ref.shape[-1]
    for n0 in range(0, n_out, nc):
        a = jnp.dot(x, w_ref[:, n0:n0 + nc], preferred_element_type=F32) + b_ref[:, n0:n0 + nc]
        if glu:
            g = (jnp.dot(x, w_ref[:, n_out + n0:n_out + n0 + nc], preferred_element_type=F32)
                 + b_ref[:, n_out + n0:n_out + n0 + nc])
            a = a * _sigmoid(g)
        o_ref[:, n0:n0 + nc] = a.astype(o_ref.dtype)


def linear(x, w, b, *, out_dtype, glu=False, tm=512, nc=512, name="linear"):
    t, k = x.shape
    n = w.shape[1]
    n_out = n // 2 if glu else n
    return pl.pallas_call(
        functools.partial(_linear_body, nc=nc, glu=glu),
        grid=(t // tm,),
        in_specs=[pl.BlockSpec((tm, k), lambda i: (i, 0)),
                  pl.BlockSpec((k, n), lambda i: (0, 0)),
                  pl.BlockSpec((1, n), lambda i: (0, 0))],
        out_specs=pl.BlockSpec((tm, n_out), lambda i: (i, 0)),
        out_shape=jax.ShapeDtypeStruct((t, n_out), out_dtype),
        compiler_params=_cp(("parallel",), 48),
        name=name,
    )(x, w, b)


def _split3(x):
    hi = x.astype(BF16).astype(F32)
    mid = (x - hi).astype(BF16).astype(F32)
    lo = (x - hi - mid).astype(BF16).astype(F32)
    return [hi, mid, lo]


def _lane_table(lane, values, first):
    out = jnp.zeros(lane.shape, F32)
    for n, val in enumerate(values):
        out = jnp.where(lane == first + n, val, out)
    return out


def _attn_body(lam_ref, g_ref, d0_ref, q_ref, k_ref, v_ref, o_ref, kp_scr, vt_scr, qq_scr, s0_scr, s1_scr,
               m_scr, l_scr, acc_scr, *, tq, tk, lam_init):
    h = pl.program_id(1)
    qi = pl.program_id(2)
    length = k_ref.shape[0]
    nk = length // tk
    n_other = nk - 1
    hd2 = 2 * HEAD_DIM
    slope2 = jnp.exp2(-(jnp.full((1, 1), h, jnp.int32) + 1).astype(F32)) * LOG2E
    c1 = _split3(slope2)
    c128 = [128.0 * c for c in c1]

    @pl.when(qi == 0)
    def _():
        lane = lax.broadcasted_iota(jnp.int32, (ATT_FILL, hd2), 1)
        consts = _lane_table(lane, [-c for c in c128] + [-c for c in c1], 0)
        eye = jnp.where(lax.broadcasted_iota(jnp.int32, (hd2, hd2), 0)
                        == lax.broadcasted_iota(jnp.int32, (hd2, hd2), 1), 1.0, 0.0).astype(BF16)

        def fill(ci, carry):
            r0 = pl.multiple_of(ci * ATT_FILL, ATT_FILL)
            pos = r0 + lax.broadcasted_iota(jnp.int32, (ATT_FILL, hd2), 0)
            hi_digit = jnp.right_shift(pos, 7).astype(F32)
            lo_digit = jnp.bitwise_and(pos, 127).astype(F32)
            aug = jnp.where(lane < 6, consts, jnp.where(lane < 9, hi_digit, jnp.where(lane < 12, lo_digit, 0.0)))
            kp_scr[pl.ds(r0, ATT_FILL), 0:hd2] = k_ref[pl.ds(r0, ATT_FILL), :]
            kp_scr[pl.ds(r0, ATT_FILL), hd2:2 * hd2] = aug.astype(BF16)
            vt_scr[:, pl.ds(r0, ATT_FILL)] = lax.dot_general(
                eye, v_ref[pl.ds(r0, ATT_FILL), :], NT_DIMS, preferred_element_type=F32).astype(BF16)
            return carry

        lax.fori_loop(0, length // ATT_FILL, fill, 0)

    q = q_ref[...]
    lane = lax.broadcasted_iota(jnp.int32, (tq, hd2), 1)
    zero = jnp.zeros_like(q)
    tpos = qi * tq + lax.broadcasted_iota(jnp.int32, (tq, hd2), 0)
    hi_digit = jnp.right_shift(tpos, 7).astype(F32)
    lo_digit = jnp.bitwise_and(tpos, 127).astype(F32)
    consts = _lane_table(lane, c128 + c1, 6)
    augq = jnp.where(lane < 3, hi_digit, jnp.where(lane < 6, lo_digit, consts))
    for var, aug in enumerate((augq.astype(BF16), (-augq).astype(BF16))):
        qq_scr[var, 0:tq, 0:hd2] = jnp.where(lane < HEAD_DIM, q, zero)
        qq_scr[var, tq:2 * tq, 0:hd2] = jnp.where(lane >= HEAD_DIM, q, zero)
        qq_scr[var, 0:tq, hd2:2 * hd2] = aug
        qq_scr[var, tq:2 * tq, hd2:2 * hd2] = aug

    m_scr[...] = jnp.full(m_scr.shape, -jnp.inf, F32)
    l_scr[...] = jnp.zeros(l_scr.shape, F32)
    acc_scr[...] = jnp.zeros(acc_scr.shape, F32)

    kd = (qi * tq) // tk

    def other_block(n):
        return jnp.where(n < n_other, jnp.where(n < kd, n, n + 1), kd)

    def scores(blk, s_scr):
        ks = pl.multiple_of(blk * tk, tk)
        var = (blk > kd).astype(jnp.int32)
        s_scr[...] = lax.dot_general(kp_scr[pl.ds(ks, tk), :], qq_scr[var], NT_DIMS, preferred_element_type=F32)

    def softmax_pv(blk, s_scr, diag):
        ks = pl.multiple_of(blk * tk, tk)
        cdiag = (qi * tq - blk * tk).astype(F32)
        vt = vt_scr[:, pl.ds(ks, tk)]
        m_all = m_scr[...]
        l_all = l_scr[...]
        m_out, l_out = [], []
        for cg in range(2 * tq // ATT_CW):
            cols = slice(cg * ATT_CW, (cg + 1) * ATT_CW)
            s = s_scr[:, cols]
            if diag:
                dc = (cg * ATT_CW) % tq
                s = s + jnp.minimum(d0_ref[:, dc:dc + ATT_CW] + cdiag, 0.0) * (2.0 * slope2)
            m_old = m_all[:, cols]
            m_new = jnp.maximum(m_old, jnp.max(s, axis=0, keepdims=True))
            p = jnp.exp2(s - m_new)
            alpha = jnp.exp2(m_old - m_new)
            l_out.append(alpha * l_all[:, cols] + jnp.sum(p, axis=0, keepdims=True))
            m_out.append(m_new)
            pv = jnp.dot(vt, p.astype(BF16), preferred_element_type=F32)
            acc_scr[:, cols] = alpha * acc_scr[:, cols] + pv
        m_scr[...] = jnp.concatenate(m_out, axis=1)
        l_scr[...] = jnp.concatenate(l_out, axis=1)

    scores(other_block(0), s0_scr)

    def pair(n2, carry):
        n = 2 * n2
        scores(other_block(n + 1), s1_scr)
        softmax_pv(other_block(n), s0_scr, False)
        scores(other_block(n + 2), s0_scr)
        softmax_pv(other_block(n + 1), s1_scr, False)
        return carry

    lax.fori_loop(0, n_other // 2, pair, 0)
    if n_other % 2:
        scores(kd, s1_scr)
        softmax_pv(other_block(n_other - 1), s0_scr, False)
        softmax_pv(kd, s1_scr, True)
    else:
        softmax_pv(kd, s0_scr, True)

    lam = lam_ref[...]
    lam_full = (jnp.exp(jnp.sum(lam[0:1] * lam[1:2], axis=-1, keepdims=True))
                - jnp.exp(jnp.sum(lam[2:3] * lam[3:4], axis=-1, keepdims=True)) + lam_init)
    ot = acc_scr[...] / l_scr[...]
    ot = ot[:, :tq] - lam_full * ot[:, tq:]
    ms = jnp.mean(ot * ot, axis=0, keepdims=True)
    ot = ot * lax.rsqrt(ms + LN_EPS) * g_ref[...] * (1.0 - lam_init)
    o_ref[...] = jnp.transpose(ot).astype(o_ref.dtype)


def diff_attention(qkv, lam, subln_g, *, lam_init, tk=ATT_TK):
    b, l, _ = qkv.shape
    tq = ATT_TQ_SHORT if l == 2 * tk else ATT_TQ
    assert l % tk == 0 and l // tk >= 2 and tk % tq == 0 and l <= 128 * 128
    hd2 = 2 * HEAD_DIM
    d0 = (jnp.arange(tq, dtype=jnp.int32)[None, :] - jnp.arange(tk, dtype=jnp.int32)[:, None]).astype(F32)
    return pl.pallas_call(
        functools.partial(_attn_body, tq=tq, tk=tk, lam_init=lam_init),
        grid=(b, N_HEADS, l // tq),
        in_specs=[pl.BlockSpec((4, HEAD_DIM), lambda bi, h, qi: (0, 0)),
                  pl.BlockSpec((hd2, 1), lambda bi, h, qi: (0, 0)),
                  pl.BlockSpec((tk, tq), lambda bi, h, qi: (0, 0), pipeline_mode=pl.Buffered(1)),
                  pl.BlockSpec((None, tq, hd2), lambda bi, h, qi: (bi, qi, h)),
                  pl.BlockSpec((None, l, hd2), lambda bi, h, qi: (bi, 0, N_HEADS + h)),
                  pl.BlockSpec((None, l, hd2), lambda bi, h, qi: (bi, 0, 2 * N_HEADS + h))],
        out_specs=pl.BlockSpec((None, tq, hd2), lambda bi, h, qi: (bi, qi, h)),
        out_shape=jax.ShapeDtypeStruct((b, l, D_MODEL), BF16),
        scratch_shapes=[pltpu.VMEM((l, 2 * hd2), BF16), pltpu.VMEM((hd2, l), BF16),
                        pltpu.VMEM((2, 2 * tq, 2 * hd2), BF16),
                        pltpu.VMEM((tk, 2 * tq), F32), pltpu.VMEM((tk, 2 * tq), F32),
                        pltpu.VMEM((1, 2 * tq), F32), pltpu.VMEM((1, 2 * tq), F32),
                        pltpu.VMEM((hd2, 2 * tq), F32)],
        compiler_params=_cp(("parallel", "parallel", "arbitrary"), 56),
        name="diff_attention",
    )(lam, subln_g, d0, qkv, qkv, qkv)


def _route(x1, wr, br, comb_ref):
    lt = lax.dot_general(wr, x1, NT_DIMS, precision=HIGHEST, preferred_element_type=F32) + br
    gl = [lt[g:g + 1] for g in range(N_GROUPS)]
    gmax = jnp.maximum(jnp.maximum(gl[0], gl[1]), jnp.maximum(gl[2], gl[3]))
    gidx = jnp.where(gl[0] == gmax, 0, jnp.where(gl[1] == gmax, 1, jnp.where(gl[2] == gmax, 2, 3)))
    gw = 1.0 / (jnp.exp(gl[0] - gmax) + jnp.exp(gl[1] - gmax) + jnp.exp(gl[2] - gmax) + jnp.exp(gl[3] - gmax))
    el = []
    for j in range(EXPERTS_PER_GROUP):
        acc = jnp.zeros_like(gmax)
        for g in range(N_GROUPS):
            r = N_GROUPS + g * EXPERTS_PER_GROUP + j
            acc = jnp.where(gidx == g, lt[r:r + 1], acc)
        el.append(acc)
    v1 = jnp.maximum(jnp.maximum(el[0], el[1]), jnp.maximum(el[2], el[3]))
    i1 = jnp.where(el[0] == v1, 0, jnp.where(el[1] == v1, 1, jnp.where(el[2] == v1, 2, 3)))
    neg = jnp.full_like(v1, -jnp.inf)
    el2 = [jnp.where(i1 == j, neg, el[j]) for j in range(EXPERTS_PER_GROUP)]
    v2 = jnp.maximum(jnp.maximum(el2[0], el2[1]), jnp.maximum(el2[2], el2[3]))
    i2 = jnp.where(el2[0] == v2, 0, jnp.where(el2[1] == v2, 1, jnp.where(el2[2] == v2, 2, 3)))
    e21 = jnp.exp(v2 - v1)
    w1 = gw / (1.0 + e21)
    w2 = gw * e21 / (1.0 + e21)
    zero = jnp.zeros_like(v1)
    for g in range(N_GROUPS):
        for j in range(EXPERTS_PER_GROUP):
            wj = jnp.where(i1 == j, w1, jnp.where(i2 == j, w2, zero))
            r = g * EXPERTS_PER_GROUP + j
            comb_ref[r:r + 1, :] = jnp.where(gidx == g, wj, zero)


def _proj_ln_route_body(a_ref, w_ref, b_ref, x_ref, g_ref, beta_ref, wr_ref, br_ref, x1_ref, comb_ref):
    h = jnp.dot(a_ref[...].astype(BF16), w_ref[...], preferred_element_type=F32) + b_ref[...]
    x1 = _layer_norm(ALPHA * x_ref[...] + h, g_ref[...], beta_ref[...])
    x1_ref[...] = x1
    _route(x1, wr_ref[...], br_ref[...], comb_ref)


def proj_ln_route(a, w, b, x, ln_g, ln_b, wr, br, *, tm=512):
    t, k = a.shape
    d = D_MODEL
    const = lambda i: (0, 0)
    return pl.pallas_call(
        _proj_ln_route_body,
        grid=(t // tm,),
        in_specs=[pl.BlockSpec((tm, k), lambda i: (i, 0)),
                  pl.BlockSpec((k, d), const), pl.BlockSpec((1, d), const),
                  pl.BlockSpec((tm, d), lambda i: (i, 0)),
                  pl.BlockSpec((1, d), const), pl.BlockSpec((1, d), const),
                  pl.BlockSpec((ROUTE_ROWS, d), const), pl.BlockSpec((ROUTE_ROWS, 1), const)],
        out_specs=[pl.BlockSpec((tm, d), lambda i: (i, 0)),
                   pl.BlockSpec((N_EXPERTS, tm), lambda i: (0, i))],
        out_shape=[jax.ShapeDtypeStruct((t, d), F32), jax.ShapeDtypeStruct((N_EXPERTS, t), F32)],
        compiler_params=_cp(("parallel",), 48),
        name="proj_ln_route",
    )(a, w, b, x, ln_g, ln_b, wr, br)


def _moe_body(x1_ref, comb_ref, uw_ref, lw_ref, wg_ref, wu_ref, wd_ref, g2_ref, b2_ref, p_ref, up_ref, gate_ref,
              o_ref, vrow_scr, vcol_scr, xg_scr, og_scr, wsl_scr, fast_scr, *, tm, st, caps):
    e = pl.program_id(1)
    n_sub = tm // st

    def sub_rows(sub):
        return slice(sub * st, (sub + 1) * st)

    @pl.when(e == 0)
    def _():
        member = jnp.where(comb_ref[...] > 0.0, 1.0, 0.0).astype(BF16)
        for sub in range(n_sub):
            ms = member[:, sub_rows(sub)]
            vrow_scr[:, sub_rows(sub)] = jnp.dot(ms, uw_ref[...], preferred_element_type=F32)
            vcol_scr[sub_rows(sub), :] = lax.dot_general(lw_ref[...], ms, NT_DIMS, preferred_element_type=F32)
        most = ((jnp.max(vrow_scr[...]) + 1.0) * 0.5).astype(jnp.int32)
        fast_scr[0] = (most <= caps).astype(jnp.int32)

        @pl.when(most <= caps)
        def _():
            tgt = 2.0 * lax.broadcasted_iota(jnp.int32, (caps, 1), 0).astype(F32) + 1.0
            for sub in range(n_sub):
                vr = vrow_scr[:, sub_rows(sub)]
                cw = comb_ref[:, sub_rows(sub)]
                blocks = []
                for ee in range(N_EXPERTS):
                    hit = vr[ee:ee + 1, :] == tgt
                    blocks.append(jnp.where(hit, 1.0, 0.0).astype(BF16))
                    wsl_scr[ee, sub * caps:(sub + 1) * caps, :] = jnp.sum(
                        jnp.where(hit, cw[ee:ee + 1, :], 0.0), axis=1, keepdims=True)
                gather = jnp.concatenate(blocks, axis=0)
                xg = jnp.dot(gather, x1_ref[sub_rows(sub), :].astype(BF16),
                             preferred_element_type=F32).astype(BF16)
                for ee in range(N_EXPERTS):
                    xg_scr[ee, sub * caps:(sub + 1) * caps, :] = xg[ee * caps:(ee + 1) * caps]

        @pl.when(most > caps)
        def _():
            o_ref[...] = jnp.zeros(o_ref.shape, F32)

    fast = fast_scr[0] == 1

    def expert_ffn(xg, wslot):
        hg = jnp.dot(xg, wg_ref[...], preferred_element_type=F32)
        hu = jnp.dot(xg, wu_ref[...], preferred_element_type=F32)
        hid = hg * _sigmoid(hg) * hu * wslot
        return jnp.dot(hid.astype(BF16), wd_ref[...], preferred_element_type=F32).astype(BF16)

    @pl.when(fast)
    def _():
        ob = expert_ffn(xg_scr[e], wsl_scr[e])
        r0 = pl.multiple_of(e * caps, caps)
        for sub in range(n_sub):
            og_scr[sub, pl.ds(r0, caps), :] = ob[sub * caps:(sub + 1) * caps]

    @pl.when(jnp.logical_not(fast))
    def _():
        sel = lax.broadcasted_iota(jnp.int32, (st, N_EXPERTS), 1) == e
        prow = lax.broadcasted_iota(jnp.int32, (8, st), 0)
        for sub in range(n_sub):
            vrow = vrow_scr[pl.ds(e, 1), sub_rows(sub)]
            vcol = jnp.sum(jnp.where(sel, vcol_scr[sub_rows(sub), :], 0.0), axis=1, keepdims=True)
            w_hi, w_mid, w_lo = _split3(comb_ref[pl.ds(e, 1), sub_rows(sub)])
            w3 = jnp.where(prow == 0, w_hi, jnp.where(prow == 1, w_mid, jnp.where(prow == 2, w_lo, 0.0))
                           ).astype(BF16)
            count = ((jnp.max(vrow) + 1.0) * 0.5).astype(jnp.int32)
            xsub = x1_ref[sub_rows(sub), :].astype(BF16)

            def chunk(c, carry):
                base = (c * caps).astype(F32)
                tgt_r = 2.0 * (base + lax.broadcasted_iota(jnp.int32, (caps, 1), 0).astype(F32)) + 1.0
                tgt_c = 2.0 * (base + lax.broadcasted_iota(jnp.int32, (1, caps), 1).astype(F32)) + 1.0
                gather = jnp.where(vrow == tgt_r, 1.0, 0.0).astype(BF16)
                scatter = jnp.where(vcol == tgt_c, 1.0, 0.0).astype(BF16)
                xg = jnp.dot(gather, xsub, preferred_element_type=F32).astype(BF16)
                ws = lax.dot_general(gather, w3, NT_DIMS, preferred_element_type=F32)
                ob = expert_ffn(xg, ws[:, 0:1] + ws[:, 1:2] + ws[:, 2:3])
                o_ref[sub_rows(sub), :] += jnp.dot(scatter, ob, preferred_element_type=F32)
                return carry

            lax.fori_loop(0, (count + (caps - 1)) // caps, chunk, 0)

    @pl.when(e == N_EXPERTS - 1)
    def _():
        @pl.when(fast)
        def _():
            lane = lax.broadcasted_iota(jnp.int32, (st, 2 * caps), 1)
            tgt = 2.0 * jnp.where(lane < caps, lane, lane - caps).astype(F32) + 1.0
            for sub in range(n_sub):
                vc = vcol_scr[sub_rows(sub), :]
                pieces = []
                for pr in range(N_EXPERTS // 2):
                    val = jnp.where(lane < caps, vc[:, 2 * pr:2 * pr + 1], vc[:, 2 * pr + 1:2 * pr + 2])
                    pieces.append(jnp.where(val == tgt, 1.0, 0.0).astype(BF16))
                scatter = jnp.concatenate(pieces, axis=1)
                o_ref[sub_rows(sub), :] = jnp.dot(scatter, og_scr[sub], preferred_element_type=F32)

        x2 = _layer_norm(ALPHA * x1_ref[...] + o_ref[...], g2_ref[...], b2_ref[...])
        up = jnp.dot(p_ref[...].astype(BF16), up_ref[...], preferred_element_type=F32)
        gt = jnp.dot(x2.astype(BF16), gate_ref[...], preferred_element_type=F32)
        o_ref[...] = x2 + up * _sigmoid(gt)


def moe_ple(x1, comb, uw, lw, wg, wu, wd, ln_g, ln_b, p, up, gate, *, tm=MOE_TM, st=MOE_ST, caps=MOE_CAPS):
    t, d = x1.shape
    ff = EXPERT_FF
    n_sub = tm // st
    assert 2 * caps == 128 and st % 128 == 0
    const = lambda i, e: (0, 0)
    once = pl.Buffered(1)
    return pl.pallas_call(
        functools.partial(_moe_body, tm=tm, st=st, caps=caps),
        grid=(t // tm, N_EXPERTS),
        in_specs=[pl.BlockSpec((tm, d), lambda i, e: (i, 0)),
                  pl.BlockSpec((N_EXPERTS, tm), lambda i, e: (0, i)),
                  pl.BlockSpec((st, st), const), pl.BlockSpec((st, st), const),
                  pl.BlockSpec((None, d, ff), lambda i, e: (e, 0, 0)),
                  pl.BlockSpec((None, d, ff), lambda i, e: (e, 0, 0)),
                  pl.BlockSpec((None, ff, d), lambda i, e: (e, 0, 0)),
                  pl.BlockSpec((1, d), const), pl.BlockSpec((1, d), const),
                  pl.BlockSpec((tm, PLE_DIM), lambda i, e: (i, 0), pipeline_mode=once),
                  pl.BlockSpec((PLE_DIM, d), const, pipeline_mode=once),
                  pl.BlockSpec((d, d), const, pipeline_mode=once)],
        out_specs=pl.BlockSpec((tm, d), lambda i, e: (i, 0)),
        out_shape=jax.ShapeDtypeStruct((t, d), F32),
        scratch_shapes=[pltpu.VMEM((N_EXPERTS, tm), F32), pltpu.VMEM((tm, N_EXPERTS), F32),
                        pltpu.VMEM((N_EXPERTS, n_sub * caps, d), BF16),
                        pltpu.VMEM((n_sub, N_EXPERTS * caps, d), BF16),
                        pltpu.VMEM((N_EXPERTS, n_sub * caps, 1), F32),
                        pltpu.SMEM((1,), jnp.int32)],
        compiler_params=_cp(("parallel", "arbitrary"), 58),
        name="moe_ple",
    )(x1, comb, uw, lw, wg, wu, wd, ln_g, ln_b, p, up, gate)


def _dwconv_body(prev_ref, cur_ref, next_ref, w_ref, b_ref, g_ref, beta_ref, o_ref, buf_scr, sh_scr,
                 *, width, tm, norm_act):
    i = pl.program_id(1)
    n = pl.num_programs(1)
    pad = width // 2
    halo = jnp.zeros(prev_ref.shape, F32)
    buf_scr[0:HALO, :] = jnp.where(i > 0, prev_ref[...], halo)
    buf_scr[HALO:HALO + tm, :] = cur_ref[...]
    buf_scr[HALO + tm:2 * HALO + tm, :] = jnp.where(i < n - 1, next_ref[...], halo)
    offs = [HALO - pad + k for k in range(width)]
    n_sh = tm + 2 * HALO - 8
    slot = {sft: n for n, sft in enumerate(sorted({off % 8 for off in offs}))}
    for sft, n_slot in slot.items():
        sh_scr[n_slot, 0:n_sh, :] = buf_scr[sft:sft + n_sh, :]

    def rows(r, carry):
        r0 = pl.multiple_of(r * CONV_ROWS, CONV_ROWS)
        acc = jnp.zeros((CONV_ROWS, o_ref.shape[1]), F32) + b_ref[...]
        for k, off in enumerate(offs):
            start = pl.multiple_of(r0 + (off // 8) * 8, 8)
            acc = acc + sh_scr[slot[off % 8], pl.ds(start, CONV_ROWS), :] * w_ref[k:k + 1, :]
        if norm_act:
            acc = _layer_norm(acc, g_ref[...], beta_ref[...])
            acc = acc * _sigmoid(acc)
        o_ref[pl.ds(r0, CONV_ROWS), :] = acc.astype(o_ref.dtype)
        return carry

    lax.fori_loop(0, tm // CONV_ROWS, rows, 0)


def dwconv(x, w, b, ln_g, ln_b, *, norm_act, out_dtype, tm=512):
    bsz, l, c = x.shape
    width = w.shape[0]
    tc = D_MODEL
    nb = tm // HALO
    last = l // HALO - 1
    return pl.pallas_call(
        functools.partial(_dwconv_body, width=width, tm=tm, norm_act=norm_act),
        grid=(bsz, l // tm, c // tc),
        in_specs=[pl.BlockSpec((None, HALO, tc), lambda bi, i, j: (bi, jnp.maximum(i * nb - 1, 0), j)),
                  pl.BlockSpec((None, tm, tc), lambda bi, i, j: (bi, i, j)),
                  pl.BlockSpec((None, HALO, tc), lambda bi, i, j: (bi, jnp.minimum((i + 1) * nb, last), j)),
                  pl.BlockSpec((width, tc), lambda bi, i, j: (0, j)),
                  pl.BlockSpec((1, tc), lambda bi, i, j: (0, j)),
                  pl.BlockSpec((1, tc), lambda bi, i, j: (0, 0)),
                  pl.BlockSpec((1, tc), lambda bi, i, j: (0, 0))],
        out_specs=pl.BlockSpec((None, None, tm, tc), lambda bi, i, j: (j, bi, i, 0)),
        out_shape=jax.ShapeDtypeStruct((c // tc, bsz, l, tc), out_dtype),
        scratch_shapes=[pltpu.VMEM((tm + 2 * HALO, tc), F32),
                        pltpu.VMEM((min(width, 8), tm + 2 * HALO, tc), F32)],
        compiler_params=_cp(("parallel", "parallel", "parallel"), 40),
        name="dwconv%d" % width,
    )(x, x, x, w, b, ln_g, ln_b)


def _filter_body(w1t_ref, w1c_ref, w1s_ref, b1_ref, w2_ref, b2_ref, w3_ref, b3_ref, freq_ref, wout_ref,
                 o_ref, *, tl, length):
    i = pl.program_id(0)
    pos = (i * tl + lax.broadcasted_iota(jnp.int32, (tl, 1), 0)).astype(F32)
    t = pos / float(length - 1)
    ang = (2.0 * math.pi / length) * pos
    band = lax.broadcasted_iota(jnp.int32, (1, POS_BANDS), 1).astype(F32)
    bands = 1e-4 + band * ((POS_BANDS - 1 - 1e-4) / (POS_BANDS - 1))
    fw = bands * ang
    freq = freq_ref[...]
    dot = functools.partial(jnp.dot, precision=HIGHEST, preferred_element_type=F32)
    pre = t * w1t_ref[...] + dot(jnp.cos(fw), w1c_ref[...]) - dot(jnp.sin(fw), w1s_ref[...]) + b1_ref[...]
    hcur = jnp.sin(freq * pre)
    hcur = jnp.sin(freq * (dot(hcur, w2_ref[...]) + b2_ref[...]))
    hcur = jnp.sin(freq * (dot(hcur, w3_ref[...]) + b3_ref[...]))
    ch = lax.broadcasted_iota(jnp.int32, (1, D_MODEL), 1).astype(F32)
    deltas = jnp.abs(MIN_DECAY + ch * ((MAX_DECAY - MIN_DECAY) / (D_MODEL - 1)))
    decay = jnp.exp(-t * deltas)
    for j in range(2 * HYENA_ORDER):
        o_ref[j] = dot(hcur, wout_ref[:, j * D_MODEL:(j + 1) * D_MODEL]) * decay


def hyena_filters(length, w1, b1, w2, b2, w3, b3, freq, wout, *, tl=512):
    fwid = FILTER_WIDTH
    const = lambda i: (0, 0)
    nf = 2 * HYENA_ORDER
    return pl.pallas_call(
        functools.partial(_filter_body, tl=tl, length=length),
        grid=(length // tl,),
        in_specs=[pl.BlockSpec((1, fwid), const), pl.BlockSpec((POS_BANDS, fwid), const),
                  pl.BlockSpec((POS_BANDS, fwid), const), pl.BlockSpec((1, fwid), const),
                  pl.BlockSpec((fwid, fwid), const), pl.BlockSpec((1, fwid), const),
                  pl.BlockSpec((fwid, fwid), const), pl.BlockSpec((1, fwid), const),
                  pl.BlockSpec((1, fwid), const), pl.BlockSpec((fwid, nf * D_MODEL), const)],
        out_specs=pl.BlockSpec((nf, tl, D_MODEL), lambda i: (0, i, 0)),
        out_shape=jax.ShapeDtypeStruct((nf, length, D_MODEL), F32),
        compiler_params=_cp(("parallel",), 32),
        name="hyena_filters",
    )(w1[0:1], w1[1:1 + POS_BANDS], w1[1 + POS_BANDS:], b1, w2, b2, w3, b3, freq, wout)


def _dft_tables(n1, n2s):
    n = n1 * n2s
    m1 = (n1 // 2 + 1 + 7) // 8 * 8
    k1 = jnp.arange(m1, dtype=jnp.int32)
    valid = (k1 <= n1 // 2)
    a = jnp.arange(n1 // 2, dtype=jnp.int32)
    ang_a = (2.0 * math.pi / n1) * ((k1[:, None] * a[None, :]) % n1).astype(F32)
    ca = jnp.where(valid[:, None], jnp.cos(ang_a), 0.0)
    sa = jnp.where(valid[:, None], jnp.sin(ang_a), 0.0)
    g_a = jnp.concatenate([ca, -sa], axis=0)
    g_ai = jnp.concatenate([ca.T, -sa.T], axis=1)
    weight = jnp.where(valid, jnp.where((k1 == 0) | (k1 == n1 // 2), 1.0, 2.0), 0.0)
    b = jnp.arange(n2s, dtype=jnp.int32)
    k2 = jnp.arange(n2s, dtype=jnp.int32)
    idx = (b[None, None, :] * (k2[None, :, None] * n1 + k1[:, None, None])) % n
    ang_c = (2.0 * math.pi / n) * idx.astype(F32)
    cc, sc = jnp.cos(ang_c), jnp.sin(ang_c)
    g_c = jnp.concatenate([jnp.concatenate([cc, sc], axis=2),
                           jnp.concatenate([-sc, cc], axis=2)], axis=1)
    cct = jnp.swapaxes(cc, 1, 2) * (weight / n)[:, None, None]
    sct = jnp.swapaxes(sc, 1, 2) * (weight / n)[:, None, None]
    g_ci = jnp.concatenate([jnp.concatenate([cct, -sct], axis=2),
                            jnp.concatenate([sct, cct], axis=2)], axis=1)
    return m1, g_a.astype(BF16), g_c.astype(BF16), g_ci.astype(BF16), g_ai.astype(BF16)


def _dft_dot(g_ref, x):
    return jnp.dot(g_ref[...], x.astype(BF16), preferred_element_type=F32)


def _dft_a_body(g_ref, x_ref, o_ref, *, nb, m1):
    for bb in range(nb):
        res = _dft_dot(g_ref, x_ref[:, bb, :])
        o_ref[:, 0, bb, :] = res[:m1]
        o_ref[:, 1, bb, :] = res[m1:]


def _dft_slab_body(g1_ref, g2_ref, x_ref, ff_ref, fb_ref, o_ref, *, bt):
    half = g1_ref.shape[0] // 2
    hf = _dft_dot(g1_ref, ff_ref[...])
    hb = _dft_dot(g1_ref, fb_ref[...])
    hr = hf[:half] + hb[:half]
    hi = hf[half:] - hb[half:]
    for bi in range(bt):
        xs = _dft_dot(g1_ref, x_ref[bi])
        xr, xi = xs[:half], xs[half:]
        ys = jnp.concatenate([xr * hr - xi * hi, xr * hi + xi * hr], axis=0)
        o_ref[bi] = _dft_dot(g2_ref, ys)


def _idft_gate_body(g_ref, x_ref, v_ref, gate_ref, bias_ref, o_ref, *, nb):
    for bb in range(nb):
        spec = jnp.concatenate([x_ref[:, 0, bb, :], x_ref[:, 1, bb, :]], axis=0)
        y = _dft_dot(g_ref, spec)
        o_ref[:, bb, :] = gate_ref[:, bb, :] * (y + v_ref[:, bb, :] * bias_ref[...])


def _dft_a(g, x, xsel, *, m1, nb, dt):
    _, bsz, half, n2s, d = x.shape
    return pl.pallas_call(
        functools.partial(_dft_a_body, nb=nb, m1=m1),
        grid=(bsz, n2s // nb, d // dt),
        in_specs=[pl.BlockSpec((2 * m1, half), lambda bi, bj, di: (0, 0)),
                  pl.BlockSpec((None, None, half, nb, dt), lambda bi, bj, di: (xsel, bi, 0, bj, di))],
        out_specs=pl.BlockSpec((None, m1, 2, nb, dt), lambda bi, bj, di: (bi, 0, 0, bj, di)),
        out_shape=jax.ShapeDtypeStruct((bsz, m1, 2, n2s, d), F32),
        compiler_params=_cp(("parallel", "parallel", "parallel"), 48),
        name="dft_a",
    )(g, x)


def _dft_slab(g1, g2, x, fa, order, *, bt):
    bsz, s, r, d = x.shape
    gspec = pl.BlockSpec((None, r, r), lambda si, bi: (si, 0, 0))
    xspec = pl.BlockSpec((bt, None, r, d), lambda si, bi: (bi, si, 0, 0))
    return pl.pallas_call(
        functools.partial(_dft_slab_body, bt=bt),
        grid=(s, bsz // bt),
        in_specs=[gspec, gspec, xspec,
                  pl.BlockSpec((None, None, r, d), lambda si, bi: (2 * order, si, 0, 0)),
                  pl.BlockSpec((None, None, r, d), lambda si, bi: (2 * order + 1, si, 0, 0))],
        out_specs=xspec,
        out_shape=jax.ShapeDtypeStruct((bsz, s, r, d), F32),
        compiler_params=_cp(("parallel", "parallel"), 48),
        name="dft_spec",
    )(g1, g2, x, fa, fa)


def _idft_gate(g, x, v, vsel, gates, gsel, bias, *, nb, dt):
    bsz, m1, _, n2s, d = x.shape
    half = g.shape[0]
    tspec = lambda sel: pl.BlockSpec((None, None, half, nb, dt), lambda bi, bj, di: (sel, bi, 0, bj, di))
    return pl.pallas_call(
        functools.partial(_idft_gate_body, nb=nb),
        grid=(bsz, n2s // nb, d // dt),
        in_specs=[pl.BlockSpec((half, 2 * m1), lambda bi, bj, di: (0, 0)),
                  pl.BlockSpec((None, m1, 2, nb, dt), lambda bi, bj, di: (bi, 0, 0, bj, di)),
                  tspec(vsel), tspec(gsel),
                  pl.BlockSpec((1, dt), lambda bi, bj, di: (0, di))],
        out_specs=pl.BlockSpec((None, half, nb, dt), lambda bi, bj, di: (bi, 0, bj, di)),
        out_shape=jax.ShapeDtypeStruct((bsz, half, n2s, d), F32),
        compiler_params=_cp(("parallel", "parallel", "parallel"), 48),
        name="idft_gate",
    )(g, x, v, gates, bias)


def _fft_plan(length):
    n = 2 * length
    n1 = 256 if n >= 32768 else 64
    return n1, n // n1


def hyena_long_convs(xs, filt, f_bias):
    _, bsz, length, d = xs.shape
    n1, n2s = _fft_plan(length)
    m1, g_a, g_c, g_ci, g_ai = _dft_tables(n1, n2s)
    half = n1 // 2
    big = length >= 8192
    bt = 1 if big else min(8, bsz)
    nb = 8 if big else 32
    dt = d // 2 if big else d
    fa = _dft_a(g_a, filt.reshape(1, 4, half, n2s, d), 0, m1=m1, nb=nb, dt=dt).reshape(4, m1, 2 * n2s, d)
    xs6 = xs.reshape(3, bsz, half, n2s, d)
    z, zsel = xs6, 2
    out = None
    for order in range(HYENA_ORDER):
        sa = _dft_a(g_a, z, zsel, m1=m1, nb=nb, dt=dt).reshape(bsz, m1, 2 * n2s, d)
        sb = _dft_slab(g_c, g_ci, sa, fa, order, bt=bt).reshape(bsz, m1, 2, n2s, d)
        out = _idft_gate(g_ai, sb, z, zsel, xs6, order, f_bias[order:order + 1], nb=nb, dt=dt)
        z, zsel = out[None], 0
    return out.reshape(bsz, length, d)


def _prefix_tables(tm):
    r = jnp.arange(tm, dtype=jnp.int32)
    upper = jnp.where(r[:, None] < r[None, :], 2.0, jnp.where(r[:, None] == r[None, :], 1.0, 0.0))
    return upper.astype(BF16), upper.T.astype(BF16)


def _trunk(x, p, wts):
    bsz, length, d = x.shape
    t = bsz * length
    xf = x.reshape(t, d)
    zeros_d = jnp.zeros((1, d), F32)
    ones_d = jnp.ones((1, d), F32)
    uw, lw = _prefix_tables(MOE_ST)
    for i in range(DEPTH):
        j, kind = i // N_MIXERS, i % N_MIXERS
        if kind == 0:
            lam_init = 0.8 - 0.6 * math.exp(-0.3 * i)
            qkv = linear(xf, wts["attn_w_qkv"][j], jnp.zeros((1, 3 * d), F32), out_dtype=BF16, name="attn_qkv")
            a = diff_attention(qkv.reshape(bsz, length, 3 * d), wts["attn_lam"][j], wts["attn_subln_g"][j],
                               lam_init=lam_init).reshape(t, d)
            w_o, b_o = wts["attn_w_o"][j], zeros_d
        elif kind == 1:
            hglu = linear(xf, wts["conv_w_pw1"][j], wts["conv_b_pw1"][j], out_dtype=F32, glu=True, name="conv_pw1")
            a = dwconv(hglu.reshape(bsz, length, d), wts["conv_w_dw"][j], wts["conv_b_dw"][j],
                       wts["conv_ln_g"][j], wts["conv_ln_b"][j], norm_act=True, out_dtype=BF16).reshape(t, d)
            w_o, b_o = wts["conv_w_pw2"][j], wts["conv_b_pw2"][j]
        else:
            u = linear(xf, wts["hy_w_in"][j], wts["hy_b_in"][j], out_dtype=F32, name="hyena_in")
            xs = dwconv(u.reshape(bsz, length, 3 * d), wts["hy_w_short"][j], wts["hy_b_short"][j],
                        ones_d, zeros_d, norm_act=False, out_dtype=F32, tm=1024)
            filt = hyena_filters(length, wts["hy_f_w1"][j], wts["hy_f_b1"][j], wts["hy_f_w2"][j], wts["hy_f_b2"][j],
                                 wts["hy_f_w3"][j], wts["hy_f_b3"][j], wts["hy_f_freq"][j], wts["hy_f_wout"][j])
            a = hyena_long_convs(xs, filt, wts["hy_f_bias"][j]).reshape(t, d)
            w_o, b_o = wts["hy_w_out"][j], wts["hy_b_out"][j]
        x1, comb = proj_ln_route(a, w_o, b_o, xf, wts["ln1_g"][i], wts["ln1_b"][i],
                                 wts["route_w"][i], wts["route_b"][i])
        xf = moe_ple(x1, comb, uw, lw, wts["moe_w_gate"][i], wts["moe_w_up"][i], wts["moe_w_down"][i],
                     wts["ln2_g"][i], wts["ln2_b"][i], p[i].reshape(t, PLE_DIM),
                     wts["ple_w_up"][i], wts["ple_w_gate"][i])
    return xf.reshape(bsz, length, d)


def kernel(x_prompt, x_sample, p_prompt, p_sample, attn_w_qkv, attn_w_o, attn_lam_q1, attn_lam_k1, attn_lam_q2, attn_lam_k2, attn_subln_g, conv_w_pw1, conv_b_pw1, conv_w_dw, conv_b_dw, conv_ln_g, conv_ln_b, conv_w_pw2, conv_b_pw2, hy_w_in, hy_b_in, hy_w_short, hy_b_short, hy_f_w1, hy_f_b1, hy_f_w2, hy_f_b2, hy_f_w3, hy_f_b3, hy_f_freq, hy_f_wout, hy_f_bias, hy_w_out, hy_b_out, ln1_g, ln1_b, ln2_g, ln2_b, moe_w_group, moe_b_group, moe_w_expert, moe_b_expert, moe_w_gate, moe_w_up, moe_w_down, ple_w_up, ple_w_gate):
    d = D_MODEL
    row = lambda a: a[:, None, :]
    q_scale = jnp.concatenate([jnp.full((d,), HEAD_DIM ** -0.5 * LOG2E, F32), jnp.ones((2 * d,), F32)])
    route_w = jnp.concatenate([jnp.swapaxes(moe_w_group, 1, 2), jnp.swapaxes(moe_w_expert, 1, 2),
                               jnp.zeros((DEPTH, ROUTE_ROWS - N_GROUPS - N_EXPERTS, d), F32)], axis=1)
    route_b = jnp.concatenate([moe_b_group, moe_b_expert,
                               jnp.zeros((DEPTH, ROUTE_ROWS - N_GROUPS - N_EXPERTS), F32)], axis=1)[:, :, None]
    wts = {
        "attn_w_qkv": (attn_w_qkv * q_scale).astype(BF16),
        "attn_w_o": attn_w_o.astype(BF16),
        "attn_lam": jnp.stack([attn_lam_q1, attn_lam_k1, attn_lam_q2, attn_lam_k2], axis=1),
        "attn_subln_g": attn_subln_g[:, :, None],
        "conv_w_pw1": conv_w_pw1.astype(BF16), "conv_b_pw1": row(conv_b_pw1),
        "conv_w_dw": conv_w_dw, "conv_b_dw": row(conv_b_dw),
        "conv_ln_g": row(conv_ln_g), "conv_ln_b": row(conv_ln_b),
        "conv_w_pw2": conv_w_pw2.astype(BF16), "conv_b_pw2": row(conv_b_pw2),
        "hy_w_in": hy_w_in.astype(BF16), "hy_b_in": row(hy_b_in),
        "hy_w_short": hy_w_short, "hy_b_short": row(hy_b_short),
        "hy_f_w1": hy_f_w1, "hy_f_b1": row(hy_f_b1), "hy_f_w2": hy_f_w2, "hy_f_b2": row(hy_f_b2),
        "hy_f_w3": hy_f_w3, "hy_f_b3": row(hy_f_b3), "hy_f_freq": row(hy_f_freq), "hy_f_wout": hy_f_wout,
        "hy_f_bias": hy_f_bias,
        "hy_w_out": hy_w_out.astype(BF16), "hy_b_out": row(hy_b_out),
        "ln1_g": row(ln1_g), "ln1_b": row(ln1_b), "ln2_g": row(ln2_g), "ln2_b": row(ln2_b),
        "route_w": route_w, "route_b": route_b,
        "moe_w_gate": moe_w_gate.astype(BF16), "moe_w_up": moe_w_up.astype(BF16),
        "moe_w_down": moe_w_down.astype(BF16),
        "ple_w_up": ple_w_up.astype(BF16), "ple_w_gate": ple_w_gate.astype(BF16),
    }
    y_prompt = _trunk(x_prompt, p_prompt, wts)
    y_sample = _trunk(x_sample, p_sample, wts)
    return (y_prompt, y_sample)
```
